```python
import math
import jax, jax.numpy as jnp
from jax import lax
import numpy as np

D_MODEL = 2048
BATCH = 4
SEQ = 2048
DEPTH = 4
DEC_BATCH = 8
DEC_SEQ = 8
PAST_LEN = 16384
PAGE_SIZE = 128

RET_HEADS = 8
RET_DK = 128
RET_DV = 256
RET_CHUNK = 128
N_HEADS = 16
N_KV_HEADS = 4
HEAD_DIM = 128
GROUP = N_HEADS // N_KV_HEADS
CMP_BLOCK = 32
CMP_STRIDE = 16
SEL_BLOCK = 64
N_SEL = 16
WINDOW = 512
WIN_BLOCK = 128
SEL_QBLOCK = 32
ROPE_THETA = 10000.0
D_FF = 5504
EPS = 1e-6
NEG = -1e30
BIG = 1e30

R_QK = RET_HEADS * RET_DK
R_V = RET_HEADS * RET_DV
A_Q = N_HEADS * HEAD_DIM
A_KV = N_KV_HEADS * HEAD_DIM
A_G = N_HEADS * 3
SPLIT_SIZES = (R_QK, R_QK, R_V, R_V, A_Q, A_KV, A_KV, A_KV, A_KV, A_KV, A_KV, A_G, D_MODEL, D_MODEL)
SPLIT_POINTS = tuple(int(s) for s in np.cumsum(SPLIT_SIZES)[:-1])
D_IN = int(sum(SPLIT_SIZES))

kernel_name = 'hybrid_retention_nsa_macaron_step'


def _rms_norm(x, g):
    xf = x.astype(jnp.float32)
    y = xf * lax.rsqrt(jnp.mean(xf * xf, axis=-1, keepdims=True) + EPS)
    return (y * g.astype(jnp.float32)).astype(x.dtype)


def _head_group_norm(o, g, dtype):
    mu = jnp.mean(o, axis=-1, keepdims=True)
    c = o - mu
    var = jnp.mean(c * c, axis=-1, keepdims=True)
    return (c * lax.rsqrt(var + EPS) * g.astype(jnp.float32)).astype(dtype)


def _rope(x, pos):
    half = x.shape[-1] // 2
    inv = 1.0 / (ROPE_THETA ** (jnp.arange(half, dtype=jnp.float32) / half))
    ang = pos.astype(jnp.float32)[:, None] * inv[None, :]
    cos = jnp.cos(ang)[:, None, :]
    sin = jnp.sin(ang)[:, None, :]
    xf = x.astype(jnp.float32)
    x1, x2 = xf[..., :half], xf[..., half:]
    return jnp.concatenate([x1 * cos - x2 * sin, x2 * cos + x1 * sin], axis=-1).astype(x.dtype)


def _swiglu(x, wg, wu, wd):
    return (jax.nn.silu(x @ wg) * (x @ wu)) @ wd


def _masked_softmax(s, mask):
    s = jnp.where(mask, s, NEG)
    m = jnp.max(s, axis=-1, keepdims=True)
    e = jnp.where(mask, jnp.exp(s - m), 0.0)
    return e / jnp.maximum(jnp.sum(e, axis=-1, keepdims=True), 1e-30)


def _gqa_attend(q, k, v, mask):
    s = jnp.einsum('bntgrd,bnsgd->bngrts', q.astype(jnp.float32), k.astype(jnp.float32)) * HEAD_DIM ** -0.5
    p = _masked_softmax(s, mask[None, :, None, None])
    o = jnp.einsum('bngrts,bnsgd->bntgrd', p, v.astype(jnp.float32))
    return o.astype(q.dtype), p


def _window_mask(q_pos, k_pos):
    d = q_pos[..., :, None] - k_pos[..., None, :]
    return (d >= 0) & (d < WINDOW) & (k_pos[..., None, :] >= 0)


def _ret_log_decay():
    return jnp.log(1.0 - 2.0 ** (-5.0 - jnp.arange(RET_HEADS, dtype=jnp.float32)))


def _retention_chunk(state, q, k, v, log_g):
    q, k, v = q.astype(jnp.float32), k.astype(jnp.float32), v.astype(jnp.float32)
    C = q.shape[1]
    i = jnp.arange(C)
    diff = i[:, None] - i[None, :]
    dmat = jnp.where(diff >= 0, jnp.exp(jnp.maximum(diff, 0)[None] * log_g[:, None, None]), 0.0)
    inner = jnp.einsum('bihd,bjhd->bhij', q, k) * dmat[None]
    o_in = jnp.einsum('bhij,bjhe->bihe', inner, v)
    xi = jnp.exp((i + 1)[None, :] * log_g[:, None])
    o_x = jnp.einsum('bihd,bhde->bihe', q, state) * xi.T[None, :, :, None]
    wk = jnp.exp((C - 1 - i)[None, :] * log_g[:, None])
    new_state = state * jnp.exp(C * log_g)[None, :, None, None] + jnp.einsum('bjhd,bjhe,hj->bhde', k, v, wk)
    return new_state, o_in + o_x


def _retention_prompt(q, k, v, log_g):
    B, T, H, _ = q.shape
    C = math.gcd(T, RET_CHUNK)
    nc = T // C
    to_chunks = lambda a: a.reshape((B, nc, C) + a.shape[2:]).transpose(1, 0, 2, 3, 4)
    s0 = jnp.zeros((B, H, RET_DK, RET_DV), jnp.float32)

    def step(s, inp):
        return _retention_chunk(s, inp[0], inp[1], inp[2], log_g)

    s_fin, o = lax.scan(step, s0, (to_chunks(q), to_chunks(k), to_chunks(v)))
    return s_fin, o.transpose(1, 0, 2, 3, 4).reshape(B, T, H, RET_DV)


def _compress(k, w):
    B, L, G, d = k.shape
    nsub = L // CMP_STRIDE
    n_c = (L - CMP_BLOCK) // CMP_STRIDE + 1
    ks = k[:, :nsub * CMP_STRIDE].reshape(B, nsub, CMP_STRIDE, G, d).astype(jnp.float32)
    wf = w.astype(jnp.float32)
    a = jnp.einsum('bnpgd,pg->bngd', ks, wf[:CMP_STRIDE])
    b = jnp.einsum('bnpgd,pg->bngd', ks, wf[CMP_STRIDE:])
    return (a[:, :n_c] + b[:, 1:n_c + 1]).astype(k.dtype)


def _cmp_to_sel(n_c, n_sel):
    cs = np.arange(n_c) * CMP_STRIDE
    ce = cs + CMP_BLOCK - 1
    js = np.arange(n_sel) * SEL_BLOCK
    je = js + SEL_BLOCK - 1
    m = (cs[:, None] <= je[None, :]) & (ce[:, None] >= js[None, :])
    return jnp.asarray(m.astype(np.float32))


def _nsa_cmp_sel(q, q_pos, kv, w_cmp):
    B, T, G, R, d = q.shape
    L = kv.shape[1]
    kc = _compress(kv[:, :, 0], w_cmp[0])
    vc = _compress(kv[:, :, 1], w_cmp[1])
    n_c = kc.shape[1]
    c_end = jnp.arange(n_c) * CMP_STRIDE + (CMP_BLOCK - 1)
    mask_c = c_end[None, :] <= q_pos[:, None]
    o_c, p_c = _gqa_attend(q[:, None], kc[:, None], vc[:, None], mask_c[None])
    o_c = o_c[:, 0]
    n_sel = -(-L // SEL_BLOCK)
    imp = jnp.einsum('bgrtc,cj->bgtj', p_c[:, 0], _cmp_to_sel(n_c, n_sel))
    j = jnp.arange(n_sel)[None, :]
    valid = j * SEL_BLOCK <= q_pos[:, None]
    forced = (j == 0) | (j == (q_pos // SEL_BLOCK)[:, None])
    imp = jnp.where(forced, BIG, jnp.where(valid, imp, NEG))
    n_top = min(N_SEL, n_sel)
    idx = lax.top_k(imp, n_top)[1]
    pad = n_sel * SEL_BLOCK - L
    sel = jnp.pad(kv[:, :, 2:4], ((0, 0), (0, pad), (0, 0), (0, 0), (0, 0)))
    k_t = sel[:, :, 0].transpose(0, 2, 1, 3)
    v_t = sel[:, :, 1].transpose(0, 2, 1, 3)
    gather = jax.vmap(jax.vmap(lambda a, i: a[i]))

    def sel_block(args):
        qb, idxb, posb = args
        qn = qb.shape[1]
        tok = (idxb[..., None] * SEL_BLOCK + jnp.arange(SEL_BLOCK)).reshape(B, G, -1)
        kg = gather(k_t, tok).reshape(B, G, qn, -1, d).astype(jnp.float32)
        vg = gather(v_t, tok).reshape(B, G, qn, -1, d).astype(jnp.float32)
        mask = tok.reshape(B, G, qn, -1) <= posb[None, None, :, None]
        s = jnp.einsum('bqgrd,bgqsd->bgrqs', qb.astype(jnp.float32), kg) * HEAD_DIM ** -0.5
        p = _masked_softmax(s, mask[:, :, None])
        return jnp.einsum('bgrqs,bgqsd->bqgrd', p, vg).astype(qb.dtype)

    qn = math.gcd(T, SEL_QBLOCK)
    nb = T // qn
    q_m = q.reshape(B, nb, qn, G, R, d).transpose(1, 0, 2, 3, 4, 5)
    idx_m = idx.reshape(B, G, nb, qn, n_top).transpose(2, 0, 1, 3, 4)
    pos_m = q_pos.reshape(nb, qn)
    o_s = lax.map(sel_block, (q_m, idx_m, pos_m))
    o_s = o_s.transpose(1, 0, 2, 3, 4, 5).reshape(B, T, G, R, d)
    return o_c, o_s


def _window_prompt(q, win):
    B, T, G, R, d = q.shape
    nb = T // WIN_BLOCK
    nw = WINDOW // WIN_BLOCK + 1
    kp = jnp.pad(win, ((0, 0), (WINDOW, 0), (0, 0), (0, 0), (0, 0)))
    kp = kp.reshape((B, nb + nw - 1, WIN_BLOCK) + win.shape[2:])
    bi = jnp.arange(nb)[:, None] + jnp.arange(nw)[None, :]
    kb = kp[:, bi].reshape((B, nb, nw * WIN_BLOCK) + win.shape[2:])
    k_pos = (bi[:, :, None] * WIN_BLOCK + jnp.arange(WIN_BLOCK)).reshape(nb, -1) - WINDOW
    q_pos = jnp.arange(T).reshape(nb, WIN_BLOCK)
    o, _ = _gqa_attend(q.reshape(B, nb, WIN_BLOCK, G, R, d), kb[:, :, :, 0], kb[:, :, :, 1],
                       _window_mask(q_pos, k_pos))
    return o.reshape(B, T, G, R, d)


def _window_sample(q, q_pos, buf, win):
    wc = buf.shape[1]
    keys = jnp.concatenate([buf, win.astype(buf.dtype)], axis=1)
    k_pos = q_pos[0] - wc + jnp.arange(keys.shape[1])
    o, _ = _gqa_attend(q[:, None], keys[:, None, :, 0], keys[:, None, :, 1],
                       _window_mask(q_pos, k_pos)[None])
    return o[:, 0], keys[:, keys.shape[1] - wc:]


def _mixer_inputs(h, w_in_l, qk_g, pos):
    B, T, _ = h.shape
    (rq, rk, rv, rg, aq, akc, avc, aks, avs, akw, avw, ag, ga, gb) = jnp.split(h @ w_in_l, SPLIT_POINTS, axis=-1)
    ret_q = _rope(rq.reshape(B, T, RET_HEADS, RET_DK), pos)
    ret_k = _rope(rk.reshape(B, T, RET_HEADS, RET_DK), pos) * RET_DK ** -0.5
    ret_v = rv.reshape(B, T, RET_HEADS, RET_DV)
    q = _rope(_rms_norm(aq.reshape(B, T, N_HEADS, HEAD_DIM), qk_g[0]), pos)
    q = q.reshape(B, T, N_KV_HEADS, GROUP, HEAD_DIM)
    kvh = lambda z: z.reshape(B, T, N_KV_HEADS, HEAD_DIM)
    kc = _rope(_rms_norm(kvh(akc), qk_g[1]), pos)
    ks = _rope(_rms_norm(kvh(aks), qk_g[2]), pos)
    kw = _rope(_rms_norm(kvh(akw), qk_g[3]), pos)
    kv_rows = jnp.stack([kc, kvh(avc), ks, kvh(avs)], axis=2)
    win_rows = jnp.stack([kw, kvh(avw)], axis=2)
    return ret_q, ret_k, ret_v, rg, q, kv_rows, win_rows, ag, ga, gb


def _merge(o_ret, rg, gn_g, o_c, o_s, o_w, ag, ga, gb, w_pa, w_pb, w_out):
    B, T = rg.shape[:2]
    o_r = _head_group_norm(o_ret, gn_g, rg.dtype).reshape(B, T, -1) * jax.nn.silu(rg)
    g = jax.nn.sigmoid(ag).reshape(B, T, N_KV_HEADS, GROUP, 3)
    o_a = (g[..., 0:1] * o_c + g[..., 1:2] * o_s + g[..., 2:3] * o_w).reshape(B, T, -1)
    m = jax.nn.sigmoid(ga) * (o_r @ w_pa) + jax.nn.sigmoid(gb) * (o_a @ w_pb)
    return m @ w_out


def setup_inputs(seed: int = 0) -> dict:
    key = jax.random.key(seed)
    ks = jax.random.split(key, 20)
    f32 = jnp.float32
    n_pages = PAST_LEN // PAGE_SIZE
    n_used = DEC_BATCH * n_pages
    n_pool = n_used + -(-n_used // 4)
    page_table = jax.random.permutation(ks[0], n_pool)[:n_used].reshape(DEC_BATCH, n_pages).astype(jnp.int32)
    win_len = min(WINDOW, PAST_LEN)
    nrm = lambda k, shape, scale: jax.random.normal(k, shape, f32) * scale
    return {
        'x_prompt': nrm(ks[1], (BATCH, SEQ, D_MODEL), 1.0),
        'x_sample': nrm(ks[2], (DEC_BATCH, DEC_SEQ, D_MODEL), 1.0),
        'cache_kv': nrm(ks[3], (DEPTH, n_pool, PAGE_SIZE, 4, N_KV_HEADS, HEAD_DIM), 1.0),
        'state_win': nrm(ks[4], (DEPTH, DEC_BATCH, win_len, 2, N_KV_HEADS, HEAD_DIM), 1.0),
        'state_ret': nrm(ks[5], (DEPTH, DEC_BATCH, RET_HEADS, RET_DK, RET_DV), RET_DK ** -0.5),
        'page_table': page_table,
        'norm_gain': 1.0 + nrm(ks[6], (DEPTH, 3, D_MODEL), 0.02),
        'ffn_gate': nrm(ks[7], (DEPTH, 2, D_MODEL, D_FF), D_MODEL ** -0.5),
        'ffn_up': nrm(ks[8], (DEPTH, 2, D_MODEL, D_FF), D_MODEL ** -0.5),
        'ffn_down': nrm(ks[9], (DEPTH, 2, D_FF, D_MODEL), D_FF ** -0.5),
        'w_in': nrm(ks[10], (DEPTH, D_MODEL, D_IN), D_MODEL ** -0.5),
        'qk_norm': 1.0 + nrm(ks[11], (DEPTH, 4, HEAD_DIM), 0.02),
        'cmp_w': (1.0 + nrm(ks[12], (DEPTH, 2, CMP_BLOCK, N_KV_HEADS), 0.1)) * CMP_BLOCK ** -0.5,
        'ret_gn': 1.0 + nrm(ks[13], (DEPTH, RET_HEADS, RET_DV), 0.02),
        'w_pa': nrm(ks[14], (DEPTH, R_V, D_MODEL), R_V ** -0.5),
        'w_pb': nrm(ks[15], (DEPTH, A_Q, D_MODEL), A_Q ** -0.5),
        'w_out': nrm(ks[16], (DEPTH, D_MODEL, D_MODEL), D_MODEL ** -0.5),
    }


def reference(x_prompt, x_sample, cache_kv, state_win, state_ret, page_table, norm_gain, ffn_gate,
              ffn_up, ffn_down, w_in, qk_norm, cmp_w, ret_gn, w_pa, w_pb, w_out):
    B, T, _ = x_prompt.shape
    DB, TS, _ = x_sample.shape
    n_pages = page_table.shape[1]
    past = n_pages * cache_kv.shape[2]
    pos_p = jnp.arange(T, dtype=jnp.int32)
    pos_s = past + jnp.arange(TS, dtype=jnp.int32)
    log_g = _ret_log_decay()
    keep_p = min(WINDOW, T)
    xp, xs = x_prompt, x_sample
    kv_p, kv_s, win_p, win_s, ret_p, ret_s = [], [], [], [], [], []
    for l in range(DEPTH):
        xp = xp + 0.5 * _swiglu(_rms_norm(xp, norm_gain[l, 0]), ffn_gate[l, 0], ffn_up[l, 0], ffn_down[l, 0])
        xs = xs + 0.5 * _swiglu(_rms_norm(xs, norm_gain[l, 0]), ffn_gate[l, 0], ffn_up[l, 0], ffn_down[l, 0])

        rq, rk, rv, rg, q, kv, win, ag, ga, gb = _mixer_inputs(_rms_norm(xp, norm_gain[l, 1]), w_in[l], qk_norm[l], pos_p)
        s_fin, o_r = _retention_prompt(rq, rk, rv, log_g)
        o_c, o_s = _nsa_cmp_sel(q, pos_p, kv, cmp_w[l])
        o_w = _window_prompt(q, win)
        xp = xp + _merge(o_r, rg, ret_gn[l], o_c, o_s, o_w, ag, ga, gb, w_pa[l], w_pb[l], w_out[l])
        kv_p.append(kv)
        win_p.append(win[:, T - keep_p:])
        ret_p.append(s_fin.astype(xp.dtype))

        rq, rk, rv, rg, q, kv, win, ag, ga, gb = _mixer_inputs(_rms_norm(xs, norm_gain[l, 1]), w_in[l], qk_norm[l], pos_s)
        s_new, o_r = _retention_chunk(state_ret[l].astype(jnp.float32), rq, rk, rv, log_g)
        past_kv = cache_kv[l][page_table].reshape((DB, past) + cache_kv.shape[3:])
        full_kv = jnp.concatenate([past_kv, kv.astype(past_kv.dtype)], axis=1)
        o_c, o_s = _nsa_cmp_sel(q, pos_s, full_kv, cmp_w[l])
        o_w, new_win = _window_sample(q, pos_s, state_win[l], win)
        xs = xs + _merge(o_r, rg, ret_gn[l], o_c, o_s, o_w, ag, ga, gb, w_pa[l], w_pb[l], w_out[l])
        kv_s.append(kv)
        win_s.append(new_win)
        ret_s.append(s_new.astype(xs.dtype))

        xp = xp + 0.5 * _swiglu(_rms_norm(xp, norm_gain[l, 2]), ffn_gate[l, 1], ffn_up[l, 1], ffn_down[l, 1])
        xs = xs + 0.5 * _swiglu(_rms_norm(xs, norm_gain[l, 2]), ffn_gate[l, 1], ffn_up[l, 1], ffn_down[l, 1])
    return (xp, xs, jnp.stack(kv_p), jnp.stack(kv_s), jnp.stack(win_p), jnp.stack(win_s), jnp.stack(ret_p), jnp.stack(ret_s))
```

```python
import functools
import math

import jax
import jax.numpy as jnp
import numpy as np
from jax import lax
from jax.experimental import pallas as pl
from jax.experimental.pallas import tpu as pltpu

D_MODEL = 2048
D_FF = 5504
RET_HEADS = 8
RET_DK = 128
RET_DV = 256
RET_CHUNK = 128
N_HEADS = 16
N_KV_HEADS = 4
HEAD_DIM = 128
GROUP = N_HEADS // N_KV_HEADS
CMP_BLOCK = 32
CMP_STRIDE = 16
SEL_BLOCK = 64
N_SEL = 16
WINDOW = 512
ROPE_THETA = 10000.0
EPS = 1e-6
NEG = -1e30
BIG = 1e30
SCALE = HEAD_DIM ** -0.5

LANE = 128
VMEM_LIMIT = 56 * 1024 * 1024

TN = 512
C_RQ, C_RK, C_RV, C_RG = 0, 1024, 2048, 4096
C_AQ = 6144
C_KC, C_VC, C_KS, C_VS, C_KW, C_VW = 8192, 8704, 9216, 9728, 10240, 10752
C_AG = 11264
C_GA, C_GB = 11776, 13824
DZ = 15872
N_ZT = DZ // TN
W_IN_SPLIT = 11312
_TILE_MODE = [1, 1, 1, 1] + [0] * 8 + [2, 2, 2, 2] + [2, 0, 2, 0, 2, 0] + [0] * 9

F_PAD = 5632
TF = 512
PAGES_PER_STEP = 8

_RET_LOG_G = [float(np.log(np.float32(1.0) - np.float32(2.0) ** np.float32(-5.0 - h))) for h in range(RET_HEADS)]

_NT = (((1,), (1,)), ((), ()))
_TN = (((0,), (0,)), ((), ()))

bf16 = jnp.bfloat16
f32 = jnp.float32


def _sigmoid(x):
    return 1.0 / (1.0 + jnp.exp(-x))


def _call(kernel, *, grid, in_specs, out_specs, out_shape, scratch=(), nsp=1, sem=None, name=None):
    return pl.pallas_call(
        kernel,
        grid_spec=pltpu.PrefetchScalarGridSpec(num_scalar_prefetch=nsp, grid=grid, in_specs=in_specs,
                                               out_specs=out_specs, scratch_shapes=list(scratch)),
        out_shape=out_shape,
        compiler_params=pltpu.CompilerParams(dimension_semantics=sem or ("arbitrary",) * len(grid),
                                             vmem_limit_bytes=VMEM_LIMIT),
        name=name,
    )


def _ffn_kernel(l_ref, x_ref, g_ref, wg_ref, wu_ref, wd_ref, o_ref, h_ref, acc_ref):
    f = pl.program_id(1)

    @pl.when(f == 0)
    def _():
        x = x_ref[...]
        ms = jnp.mean(x * x, axis=-1, keepdims=True)
        h_ref[...] = (x * lax.rsqrt(ms + EPS) * g_ref[...]).astype(bf16)
        acc_ref[...] = jnp.zeros_like(acc_ref)

    h = h_ref[...]
    a = jnp.dot(h, wg_ref[...], preferred_element_type=f32)
    b = jnp.dot(h, wu_ref[...], preferred_element_type=f32)
    s = (a * _sigmoid(a)) * b
    acc_ref[...] += jnp.dot(s.astype(bf16), wd_ref[...], preferred_element_type=f32)

    @pl.when(f == pl.num_programs(1) - 1)
    def _():
        o_ref[...] = x_ref[...] + 0.5 * acc_ref[...]


def _ffn(larr, x, gains, which_gain, wg, wu, wd, which_w, tm):
    m = x.shape[0]
    grid = (m // tm, F_PAD // TF)
    return _call(
        _ffn_kernel, grid=grid,
        in_specs=[
            pl.BlockSpec((tm, D_MODEL), lambda i, f, l: (i, 0)),
            pl.BlockSpec((None, None, 1, D_MODEL), lambda i, f, l: (l[0], which_gain, 0, 0)),
            pl.BlockSpec((None, None, D_MODEL, TF), lambda i, f, l: (l[0], which_w, 0, f)),
            pl.BlockSpec((None, None, D_MODEL, TF), lambda i, f, l: (l[0], which_w, 0, f)),
            pl.BlockSpec((None, None, TF, D_MODEL), lambda i, f, l: (l[0], which_w, f, 0)),
        ],
        out_specs=pl.BlockSpec((tm, D_MODEL), lambda i, f, l: (i, 0)),
        out_shape=jax.ShapeDtypeStruct((m, D_MODEL), f32),
        scratch=[pltpu.VMEM((tm, D_MODEL), bf16), pltpu.VMEM((tm, D_MODEL), f32)],
        name="ffn",
    )(larr, x, gains, wg, wu, wd)


def _inproj_kernel(l_ref, mode_ref, x_ref, g_ref, w_ref, gain_ref, cos_ref, sin_ref, o_ref, h_ref):
    j = pl.program_id(1)

    @pl.when(j == 0)
    def _():
        x = x_ref[...]
        ms = jnp.mean(x * x, axis=-1, keepdims=True)
        h_ref[...] = (x * lax.rsqrt(ms + EPS) * g_ref[...]).astype(bf16)

    acc = jnp.dot(h_ref[...], w_ref[...], preferred_element_type=f32)
    mode = mode_ref[j]

    @pl.when(mode == 0)
    def _():
        o_ref[...] = acc

    @pl.when(mode != 0)
    def _():
        cos = cos_ref[...]
        sin = sin_ref[...]
        gain = gain_ref[...]
        for hd in range(TN // HEAD_DIM):
            y = acc[:, hd * HEAD_DIM:(hd + 1) * HEAD_DIM]
            ms = jnp.mean(y * y, axis=-1, keepdims=True)
            inv = jnp.where(mode == 2, lax.rsqrt(ms + EPS), 1.0)
            y = y * inv * gain
            o_ref[:, hd * HEAD_DIM:(hd + 1) * HEAD_DIM] = y * cos + pltpu.roll(y, HEAD_DIM // 2, 1) * sin


def _inproj(larr, modes, x, gains, w_in, tile_gain, cos2, sin2, tm):
    m = x.shape[0]
    nt = cos2.shape[0] // tm
    grid = (m // tm, N_ZT)
    return _call(
        _inproj_kernel, grid=grid, nsp=2,
        in_specs=[
            pl.BlockSpec((tm, D_MODEL), lambda i, j, l, md: (i, 0)),
            pl.BlockSpec((None, None, 1, D_MODEL), lambda i, j, l, md: (l[0], 1, 0, 0)),
            pl.BlockSpec((None, D_MODEL, TN), lambda i, j, l, md: (l[0], 0, j)),
            pl.BlockSpec((None, None, 1, HEAD_DIM), lambda i, j, l, md: (l[0], j, 0, 0)),
            pl.BlockSpec((tm, HEAD_DIM), lambda i, j, l, md: (i % nt, 0)),
            pl.BlockSpec((tm, HEAD_DIM), lambda i, j, l, md: (i % nt, 0)),
        ],
        out_specs=pl.BlockSpec((tm, TN), lambda i, j, l, md: (i, j)),
        out_shape=jax.ShapeDtypeStruct((m, DZ), f32),
        scratch=[pltpu.VMEM((tm, D_MODEL), bf16)],
        name="inproj",
    )(larr, modes, x, gains, w_in, tile_gain, cos2, sin2)


def _ret_kernel(l_ref, q_ref, k_ref, v_ref, rg_ref, s0_ref, gn_ref, o_ref, sout_ref, s_scr, *, c_true, c_pad):
    c = pl.program_id(1)

    @pl.when(c == 0)
    def _():
        s_scr[...] = s0_ref[...]

    ri = lax.broadcasted_iota(jnp.int32, (c_pad, c_pad), 0)
    ci = lax.broadcasted_iota(jnp.int32, (c_pad, c_pad), 1)
    diff = ri - ci
    row = lax.broadcasted_iota(jnp.int32, (c_pad, 1), 0)

    def padded(a):
        if c_pad == c_true:
            return a
        return jnp.concatenate([a, jnp.zeros((c_pad - c_true, a.shape[1]), a.dtype)], axis=0)

    for h in range(RET_HEADS):
        lg = _RET_LOG_G[h]
        dmat = jnp.where(diff >= 0, jnp.exp(jnp.maximum(diff, 0).astype(f32) * lg), 0.0)
        q = padded(q_ref[:, h * RET_DK:(h + 1) * RET_DK])
        k = padded(k_ref[:, h * RET_DK:(h + 1) * RET_DK])
        v = padded(v_ref[:, h * RET_DV:(h + 1) * RET_DV])
        qb, kb, vb = q.astype(bf16), k.astype(bf16), v.astype(bf16)
        inner = lax.dot_general(qb, kb, _NT, preferred_element_type=f32) * dmat
        xi = jnp.exp((row + 1).astype(f32) * lg)
        state = s_scr[h]
        o = (jnp.dot(inner.astype(bf16), vb, preferred_element_type=f32)
             + jnp.dot(qb, state.astype(bf16), preferred_element_type=f32) * xi)
        wk = jnp.exp((c_true - 1 - row).astype(f32) * lg)
        kw = (k * wk).astype(bf16)
        s_scr[h] = state * float(np.exp(np.float32(c_true * lg))) + lax.dot_general(
            kw, vb, _TN, preferred_element_type=f32)
        o = o[:c_true]
        mu = jnp.mean(o, axis=-1, keepdims=True)
        cen = o - mu
        var = jnp.mean(cen * cen, axis=-1, keepdims=True)
        y = cen * lax.rsqrt(var + EPS) * gn_ref[h]
        rg = rg_ref[:, h * RET_DV:(h + 1) * RET_DV]
        o_ref[:, h * RET_DV:(h + 1) * RET_DV] = (y * (rg * _sigmoid(rg))).astype(o_ref.dtype)

    @pl.when(c == pl.num_programs(1) - 1)
    def _():
        sout_ref[...] = s_scr[...]


def _retention(larr, z, s0, s0_per_layer, ret_gn, nb, nc, c_true, out_dtype):
    c_pad = max(c_true, RET_CHUNK)
    m = z.shape[0]
    r_qk = RET_HEADS * RET_DK
    r_v = RET_HEADS * RET_DV
    kern = functools.partial(_ret_kernel, c_true=c_true, c_pad=c_pad)
    return _call(
        kern, grid=(nb, nc),
        in_specs=[
            pl.BlockSpec((c_true, r_qk), lambda b, c, l: (b * nc + c, C_RQ // r_qk)),
            pl.BlockSpec((c_true, r_qk), lambda b, c, l: (b * nc + c, C_RK // r_qk)),
            pl.BlockSpec((c_true, r_v), lambda b, c, l: (b * nc + c, C_RV // r_v)),
            pl.BlockSpec((c_true, r_v), lambda b, c, l: (b * nc + c, C_RG // r_v)),
            pl.BlockSpec((None, RET_HEADS, RET_DK, RET_DV), lambda b, c, l: (l[0] * s0_per_layer + b, 0, 0, 0)),
            pl.BlockSpec((None, RET_HEADS, 1, RET_DV), lambda b, c, l: (l[0], 0, 0, 0)),
        ],
        out_specs=[
            pl.BlockSpec((c_true, r_v), lambda b, c, l: (b * nc + c, 0)),
            pl.BlockSpec((None, RET_HEADS, RET_DK, RET_DV), lambda b, c, l: (b, 0, 0, 0)),
        ],
        out_shape=[jax.ShapeDtypeStruct((m, r_v), out_dtype),
                   jax.ShapeDtypeStruct((nb, RET_HEADS, RET_DK, RET_DV), f32)],
        scratch=[pltpu.VMEM((RET_HEADS, RET_DK, RET_DV), f32)],
        name="retention",
    )(larr, z, z, z, z, s0, ret_gn)


def _cmp_kernel(*refs, n_in, n_scalar):
    x_refs = refs[n_scalar:n_scalar + n_in]
    w_ref, a_ref, b_ref = refs[n_scalar + n_in:]
    wa = w_ref[0:CMP_STRIDE, :]
    wb = w_ref[CMP_STRIDE:CMP_BLOCK, :]
    for k in range(n_in):
        x = x_refs[k][...]
        r = x.shape[0] // CMP_STRIDE
        x3 = x.reshape(r, CMP_STRIDE, x.shape[1])
        a_ref[k * r:(k + 1) * r, :] = jnp.sum(x3 * wa[None], axis=1)
        b_ref[k * r:(k + 1) * r, :] = jnp.sum(x3 * wb[None], axis=1)


def _compress_prompt(larr, z, w_cmp, rows):
    m = z.shape[0]
    wcols = 2 * N_KV_HEADS * HEAD_DIM
    kern = functools.partial(_cmp_kernel, n_in=1, n_scalar=1)
    shp = jax.ShapeDtypeStruct((m // CMP_STRIDE, wcols), f32)
    return _call(
        kern, grid=(m // rows,),
        in_specs=[pl.BlockSpec((rows, wcols), lambda i, l: (i, C_KC // wcols)),
                  pl.BlockSpec((None, CMP_BLOCK, wcols), lambda i, l: (l[0], 0, 0))],
        out_specs=[pl.BlockSpec((rows // CMP_STRIDE, wcols), lambda i, l: (i, 0))] * 2,
        out_shape=[shp, shp], name="compress_prompt",
    )(larr, z, w_cmp)


def _compress_pages(larr, page_table, cache, w_cmp, n_pool):
    db, n_pages = page_table.shape
    page = cache.shape[1]
    wcols = 2 * N_KV_HEADS * HEAD_DIM
    pps = PAGES_PER_STEP
    steps = n_pages // pps
    sub = page // CMP_STRIDE
    kern = functools.partial(_cmp_kernel, n_in=pps, n_scalar=2)

    def page_spec(k):
        return pl.BlockSpec((None, page, wcols), lambda b, n, l, pt: (l[0] * n_pool + pt[b, n * pps + k], 0, 0))

    shp = jax.ShapeDtypeStruct((db * n_pages * sub, wcols), f32)
    return _call(
        kern, grid=(db, steps), nsp=2,
        in_specs=[page_spec(k) for k in range(pps)]
        + [pl.BlockSpec((None, CMP_BLOCK, wcols), lambda b, n, l, pt: (l[0], 0, 0))],
        out_specs=[pl.BlockSpec((pps * sub, wcols), lambda b, n, l, pt: (b * steps + n, 0))] * 2,
        out_shape=[shp, shp], name="compress_pages",
    )(larr, page_table, *([cache] * pps), w_cmp)


def _combine_cmp(a, b):
    n = a.shape[0]
    row = lax.broadcasted_iota(jnp.int32, (n, 1), 0)
    return a + jnp.where(row == n - 1, 0.0, pltpu.roll(b, n - 1, 0))


def _masked_exp(s, mask):
    s = jnp.where(mask, s, NEG)
    m = jnp.max(s, axis=-1, keepdims=True)
    e = jnp.where(mask, jnp.exp(s - m), 0.0)
    return e, jnp.sum(e, axis=-1, keepdims=True)


def _dot_hilo(p, m01):
    hi = p.astype(bf16)
    lo = (p - hi.astype(f32)).astype(bf16)
    return jnp.dot(hi, m01, preferred_element_type=f32) + jnp.dot(lo, m01, preferred_element_type=f32)


def _topk_mask(imp, k):
    lane = lax.broadcasted_iota(jnp.int32, imp.shape, 1).astype(f32)
    sel = jnp.zeros(imp.shape, f32)
    for _ in range(k):
        m = jnp.max(imp, axis=-1, keepdims=True)
        idx = jnp.min(jnp.where(imp == m, lane, 1e9), axis=-1, keepdims=True)
        hit = lane == idx
        sel = jnp.where(hit, 1.0, sel)
        imp = jnp.where(hit, -jnp.inf, imp)
    return sel


def _block_importance(p_sum, mcs, tpos, n_sel):
    imp = _dot_hilo(p_sum, mcs)
    jj = lax.broadcasted_iota(jnp.int32, (1, imp.shape[1]), 1)
    forced = jnp.where(jj == 0, 1, jnp.where(jj == (tpos >> 6), 1, 0))
    imp = jnp.where(jj * SEL_BLOCK <= tpos, imp, NEG)
    imp = jnp.where(forced == 1, BIG, imp)
    return jnp.where(jj < n_sel, imp, -jnp.inf)


def _gate(sg, idx):
    lane = lax.broadcasted_iota(jnp.int32, (1, sg.shape[1]), 1)
    return jnp.sum(jnp.where(lane == idx, sg, 0.0), axis=-1, keepdims=True)


def _nsa_prompt_kernel(l_ref, q_ref, ka_ref, kb_ref, va_ref, vb_ref, ks_ref, vs_ref, kw_ref, vw_ref, ag_ref,
                       mcs_ref, e_ref, o_ref, *, tq, t_len, wlen):
    g = pl.program_id(1)
    i = pl.program_id(2)
    t0 = i * tq
    tpos = t0 + lax.broadcasted_iota(jnp.int32, (tq, 1), 0)
    n_c = t_len // CMP_STRIDE
    n_sel = t_len // SEL_BLOCK

    kc = _combine_cmp(ka_ref[...], kb_ref[...]).astype(bf16)
    vc = _combine_cmp(va_ref[...], vb_ref[...]).astype(bf16)
    cend = lax.broadcasted_iota(jnp.int32, (1, n_c), 1) * CMP_STRIDE + (CMP_BLOCK - 1)
    mask_c = cend <= tpos

    qs = [q_ref[:, r * HEAD_DIM:(r + 1) * HEAD_DIM].astype(bf16) for r in range(GROUP)]

    p_sum = jnp.zeros((tq, n_c), f32)
    o_c = []
    for r in range(GROUP):
        s = lax.dot_general(qs[r], kc, _NT, preferred_element_type=f32) * SCALE
        e, den = _masked_exp(s, mask_c)
        p = e / jnp.maximum(den, 1e-30)
        p_sum = p_sum + p
        o_c.append(jnp.dot(p.astype(bf16), vc, preferred_element_type=f32))

    imp = _block_importance(p_sum, mcs_ref[...], tpos, n_sel)
    sel = _topk_mask(imp, min(N_SEL, n_sel))
    selk = jnp.dot(sel.astype(bf16), e_ref[...], preferred_element_type=f32)
    kpos = lax.broadcasted_iota(jnp.int32, (1, t_len), 1)
    mask_s = jnp.where(kpos <= tpos, selk, 0.0) > 0.5

    ks = ks_ref[...].astype(bf16)
    vs = vs_ref[...].astype(bf16)
    o_s = []
    for r in range(GROUP):
        s = lax.dot_general(qs[r], ks, _NT, preferred_element_type=f32) * SCALE
        e, den = _masked_exp(s, mask_s)
        o = jnp.dot(e.astype(bf16), vs, preferred_element_type=f32)
        o_s.append(o / jnp.maximum(den, 1e-30))

    start = pl.multiple_of(jnp.maximum(t0 + tq - wlen, 0), tq)
    kw = kw_ref[pl.ds(start, wlen), :].astype(bf16)
    vw = vw_ref[pl.ds(start, wlen), :].astype(bf16)
    d = tpos - (start + lax.broadcasted_iota(jnp.int32, (1, wlen), 1))
    mask_w = jnp.where(d >= 0, d, WINDOW) < WINDOW
    o_w = []
    for r in range(GROUP):
        s = lax.dot_general(qs[r], kw, _NT, preferred_element_type=f32) * SCALE
        e, den = _masked_exp(s, mask_w)
        o = jnp.dot(e.astype(bf16), vw, preferred_element_type=f32)
        o_w.append(o / jnp.maximum(den, 1e-30))

    sg = _sigmoid(ag_ref[...])
    for r in range(GROUP):
        base = (g * GROUP + r) * 3
        out = _gate(sg, base) * o_c[r] + _gate(sg, base + 1) * o_s[r] + _gate(sg, base + 2) * o_w[r]
        o_ref[:, r * HEAD_DIM:(r + 1) * HEAD_DIM] = out.astype(o_ref.dtype)


def _nsa_prompt(larr, z, cmp_a, cmp_b, mcs, emat, nb, t_len, tq):
    m = z.shape[0]
    nq = t_len // tq
    n_c = t_len // CMP_STRIDE
    wlen = min(WINDOW + tq, t_len)
    gw = GROUP * HEAD_DIM
    kern = functools.partial(_nsa_prompt_kernel, tq=tq, t_len=t_len, wlen=wlen)

    def head_cols(c0):
        return pl.BlockSpec((t_len, HEAD_DIM), lambda b, g, i, l: (b, c0 // HEAD_DIM + g))

    def cmp_spec(off):
        return pl.BlockSpec((n_c, HEAD_DIM), lambda b, g, i, l: (b, off + g))

    return _call(
        kern, grid=(nb, N_KV_HEADS, nq),
        in_specs=[
            pl.BlockSpec((tq, gw), lambda b, g, i, l: (b * nq + i, C_AQ // gw + g)),
            cmp_spec(0), cmp_spec(0), cmp_spec(N_KV_HEADS), cmp_spec(N_KV_HEADS),
            head_cols(C_KS), head_cols(C_VS), head_cols(C_KW), head_cols(C_VW),
            pl.BlockSpec((tq, LANE), lambda b, g, i, l: (b * nq + i, C_AG // LANE)),
            pl.BlockSpec(mcs.shape, lambda b, g, i, l: (0, 0)),
            pl.BlockSpec(emat.shape, lambda b, g, i, l: (0, 0)),
        ],
        out_specs=pl.BlockSpec((tq, gw), lambda b, g, i, l: (b * nq + i, g)),
        out_shape=jax.ShapeDtypeStruct((m, N_HEADS * HEAD_DIM), bf16),
        name="nsa_prompt",
    )(larr, z, cmp_a, cmp_b, cmp_a, cmp_b, z, z, z, z, z, mcs, emat)


def _rows_rt(ref, g_off, tn):
    return jnp.concatenate(
        [ref[:, (g_off + r) * HEAD_DIM:(g_off + r + 1) * HEAD_DIM] for r in range(GROUP)], axis=0)


def _nsa_s_cmp_kernel(l_ref, q_ref, ka_ref, kb_ref, va_ref, vb_ref, mcs_ref, oc_ref, sel_ref, *, tn, past, n_sel):
    n_c = ka_ref.shape[0]
    rows = GROUP * tn
    q = _rows_rt(q_ref, 0, tn).astype(bf16)
    kc = _combine_cmp(ka_ref[...], kb_ref[...]).astype(bf16)
    vc = _combine_cmp(va_ref[...], vb_ref[...]).astype(bf16)
    tpos_r = past + (lax.broadcasted_iota(jnp.int32, (rows, 1), 0) % tn)
    cend = lax.broadcasted_iota(jnp.int32, (1, n_c), 1) * CMP_STRIDE + (CMP_BLOCK - 1)
    s = lax.dot_general(q, kc, _NT, preferred_element_type=f32) * SCALE
    e, den = _masked_exp(s, cend <= tpos_r)
    p = e / jnp.maximum(den, 1e-30)
    oc_ref[...] = jnp.dot(p.astype(bf16), vc, preferred_element_type=f32)
    p_sum = p[0:tn]
    for r in range(1, GROUP):
        p_sum = p_sum + p[r * tn:(r + 1) * tn]
    p_sum = jnp.concatenate([p_sum, jnp.zeros_like(p_sum)], axis=0)
    tpos = past + (lax.broadcasted_iota(jnp.int32, (2 * tn, 1), 0) % tn)
    imp = _block_importance(p_sum, mcs_ref[...], tpos, n_sel)
    sel_ref[...] = _topk_mask(imp, min(N_SEL, n_sel))[0:tn]


def _nsa_s_cmp(larr, z, cmp_a, cmp_b, mcs, db, tn, past, n_sel):
    n_c = cmp_a.shape[0] // db
    gw = GROUP * HEAD_DIM
    kern = functools.partial(_nsa_s_cmp_kernel, tn=tn, past=past, n_sel=n_sel)

    def cmp_spec(off):
        return pl.BlockSpec((n_c, HEAD_DIM), lambda b, g, l: (b, off + g))

    return _call(
        kern, grid=(db, N_KV_HEADS),
        in_specs=[pl.BlockSpec((tn, gw), lambda b, g, l: (b, C_AQ // gw + g)),
                  cmp_spec(0), cmp_spec(0), cmp_spec(N_KV_HEADS), cmp_spec(N_KV_HEADS),
                  pl.BlockSpec(mcs.shape, lambda b, g, l: (0, 0))],
        out_specs=[pl.BlockSpec((None, None, GROUP * tn, HEAD_DIM), lambda b, g, l: (b, g, 0, 0)),
                   pl.BlockSpec((None, None, tn, mcs.shape[1]), lambda b, g, l: (b, g, 0, 0))],
        out_shape=[jax.ShapeDtypeStruct((db, N_KV_HEADS, GROUP * tn, HEAD_DIM), f32),
                   jax.ShapeDtypeStruct((db, N_KV_HEADS, tn, mcs.shape[1]), f32)],
        name="nsa_sample_cmp",
    )(larr, z, cmp_a, cmp_b, cmp_a, cmp_b, mcs)


def _nsa_s_sel_kernel(*refs, tn, pps):
    q_ref, sel_ref, e_ref = refs[2:5]
    page_refs = refs[5:5 + pps]
    acc_ref, m_ref, l_ref = refs[5 + pps:]
    n = pl.program_id(1)

    @pl.when(n == 0)
    def _():
        acc_ref[...] = jnp.zeros_like(acc_ref)
        m_ref[...] = jnp.full_like(m_ref, NEG)
        l_ref[...] = jnp.zeros_like(l_ref)

    kvw = N_KV_HEADS * HEAD_DIM
    emat = e_ref[...]
    for g in range(N_KV_HEADS):
        q = _rows_rt(q_ref, g * GROUP, tn).astype(bf16)
        s = jnp.concatenate(
            [lax.dot_general(q, pr[:, g * HEAD_DIM:(g + 1) * HEAD_DIM].astype(bf16), _NT,
                             preferred_element_type=f32) for pr in page_refs], axis=1) * SCALE
        selg = sel_ref[g]
        selg = jnp.concatenate([selg, jnp.zeros_like(selg)], axis=0).astype(bf16)
        selk = jnp.dot(selg, emat, preferred_element_type=f32)[0:tn]
        mask = jnp.concatenate([selk] * GROUP, axis=0) > 0.5
        s = jnp.where(mask, s, NEG)
        m_old = m_ref[g]
        m_new = jnp.maximum(m_old, jnp.max(s, axis=-1, keepdims=True))
        alpha = jnp.exp(m_old - m_new)
        e = jnp.where(mask, jnp.exp(s - m_new[:, 0:1]), 0.0)
        l_ref[g] = alpha * l_ref[g] + jnp.sum(e, axis=-1, keepdims=True)
        pv = None
        for k, pr in enumerate(page_refs):
            page = pr.shape[0]
            ek = e[:, k * page:(k + 1) * page].astype(bf16)
            vk = pr[:, kvw + g * HEAD_DIM:kvw + (g + 1) * HEAD_DIM].astype(bf16)
            d = jnp.dot(ek, vk, preferred_element_type=f32)
            pv = d if pv is None else pv + d
        acc_ref[g] = alpha * acc_ref[g] + pv
        m_ref[g] = m_new


def _nsa_s_sel(larr, page_table, z, sel, emat_pages, cache, n_pool, tn):
    db, n_pages = page_table.shape
    page = cache.shape[1]
    pps = PAGES_PER_STEP
    steps = n_pages // pps
    kvw = N_KV_HEADS * HEAD_DIM
    rows = GROUP * tn
    kern = functools.partial(_nsa_s_sel_kernel, tn=tn, pps=pps)

    def page_spec(k):
        return pl.BlockSpec((None, page, 2 * kvw), lambda b, n, l, pt: (l[0] * n_pool + pt[b, n * pps + k], 0, 1))

    st = jax.ShapeDtypeStruct((db, N_KV_HEADS, rows, HEAD_DIM), f32)
    st_spec = pl.BlockSpec((None, N_KV_HEADS, rows, HEAD_DIM), lambda b, n, l, pt: (b, 0, 0, 0))
    return _call(
        kern, grid=(db, steps), nsp=2,
        in_specs=[pl.BlockSpec((tn, N_HEADS * HEAD_DIM), lambda b, n, l, pt: (b, C_AQ // (N_HEADS * HEAD_DIM))),
                  pl.BlockSpec((None, None, N_KV_HEADS, tn, LANE), lambda b, n, l, pt: (b, n, 0, 0, 0)),
                  pl.BlockSpec(emat_pages.shape, lambda b, n, l, pt: (0, 0))]
        + [page_spec(k) for k in range(pps)],
        out_specs=[st_spec, st_spec, st_spec],
        out_shape=[st, st, st], name="nsa_sample_sel",
    )(larr, page_table, z, sel, emat_pages, *([cache] * pps))


def _nsa_s_fin_kernel(l_ref, q_ref, ksn_ref, vsn_ref, kwn_ref, vwn_ref, ag_ref, kbuf_ref, vbuf_ref,
                      acc_ref, m_ref, lsum_ref, oc_ref, sel_ref, o_ref, *, tn, past, n_sel):
    g = pl.program_id(1)
    rows = GROUP * tn
    wc = kbuf_ref.shape[0]
    q = _rows_rt(q_ref, 0, tn).astype(bf16)
    tpos = past + (lax.broadcasted_iota(jnp.int32, (rows, 1), 0) % tn)

    def pad_keys(a):
        return jnp.concatenate([a, jnp.zeros((LANE - tn, a.shape[1]), a.dtype)], axis=0)

    ksn = pad_keys(ksn_ref[...]).astype(bf16)
    vsn = pad_keys(vsn_ref[...]).astype(bf16)
    s = lax.dot_general(q, ksn, _NT, preferred_element_type=f32) * SCALE
    kidx = lax.broadcasted_iota(jnp.int32, (1, LANE), 1)
    last_sel = sel_ref[:, n_sel - 1:n_sel]
    last_sel = jnp.concatenate([last_sel] * GROUP, axis=0)
    mask = jnp.where(kidx < tn, jnp.where(past + kidx <= tpos, last_sel, 0.0), 0.0) > 0.5
    s = jnp.where(mask, s, NEG)
    m_old = m_ref[...]
    m_new = jnp.maximum(m_old, jnp.max(s, axis=-1, keepdims=True))
    alpha = jnp.exp(m_old - m_new)
    e = jnp.where(mask, jnp.exp(s - m_new[:, 0:1]), 0.0)
    den = alpha * lsum_ref[...] + jnp.sum(e, axis=-1, keepdims=True)
    o_s = (alpha * acc_ref[...] + jnp.dot(e.astype(bf16), vsn, preferred_element_type=f32)) / jnp.maximum(den, 1e-30)

    kw = jnp.concatenate([kbuf_ref[...], pad_keys(kwn_ref[...])], axis=0).astype(bf16)
    vw = jnp.concatenate([vbuf_ref[...], pad_keys(vwn_ref[...])], axis=0).astype(bf16)
    widx = lax.broadcasted_iota(jnp.int32, (1, wc + LANE), 1)
    d = tpos - (past - wc + widx)
    mask_w = jnp.where(widx < wc + tn, jnp.where(d >= 0, d, WINDOW), WINDOW) < WINDOW
    s = lax.dot_general(q, kw, _NT, preferred_element_type=f32) * SCALE
    e, den = _masked_exp(s, mask_w)
    o_w = jnp.dot(e.astype(bf16), vw, preferred_element_type=f32) / jnp.maximum(den, 1e-30)

    o_c = oc_ref[...]
    sg = _sigmoid(ag_ref[...])
    for r in range(GROUP):
        base = (g * GROUP + r) * 3
        sl = slice(r * tn, (r + 1) * tn)
        out = _gate(sg, base) * o_c[sl] + _gate(sg, base + 1) * o_s[sl] + _gate(sg, base + 2) * o_w[sl]
        o_ref[:, r * HEAD_DIM:(r + 1) * HEAD_DIM] = out


def _nsa_s_fin(larr, z, win_buf, acc, mx, lsum, o_c, sel, db, tn, past, n_sel):
    gw = GROUP * HEAD_DIM
    rows = GROUP * tn
    wc = win_buf.shape[1]
    kern = functools.partial(_nsa_s_fin_kernel, tn=tn, past=past, n_sel=n_sel)

    def new_cols(c0):
        return pl.BlockSpec((tn, HEAD_DIM), lambda b, g, l: (b, c0 // HEAD_DIM + g))

    def buf_spec(off):
        return pl.BlockSpec((None, wc, HEAD_DIM), lambda b, g, l: (l[0] * db + b, 0, off + g))

    st_spec = pl.BlockSpec((None, None, rows, HEAD_DIM), lambda b, g, l: (b, g, 0, 0))
    return _call(
        kern, grid=(db, N_KV_HEADS),
        in_specs=[pl.BlockSpec((tn, gw), lambda b, g, l: (b, C_AQ // gw + g)),
                  new_cols(C_KS), new_cols(C_VS), new_cols(C_KW), new_cols(C_VW),
                  pl.BlockSpec((tn, LANE), lambda b, g, l: (b, C_AG // LANE)),
                  buf_spec(0), buf_spec(N_KV_HEADS),
                  st_spec, st_spec, st_spec, st_spec,
                  pl.BlockSpec((None, None, tn, sel.shape[-1]), lambda b, g, l: (b, g, 0, 0))],
        out_specs=pl.BlockSpec((tn, gw), lambda b, g, l: (b, g)),
        out_shape=jax.ShapeDtypeStruct((db * tn, N_HEADS * HEAD_DIM), f32),
        name="nsa_sample_fin",
    )(larr, z, z, z, z, z, z, win_buf, win_buf, acc, mx, lsum, o_c, sel)


def _merge_kernel(l_ref, x_ref, or_ref, oa_ref, ga_ref, gb_ref, wpa_ref, wpb_ref, wo_ref, o_ref, acc_ref):
    j = pl.program_id(1)

    @pl.when(j == 0)
    def _():
        acc_ref[...] = jnp.zeros_like(acc_ref)

    pa = jnp.dot(or_ref[...], wpa_ref[...], preferred_element_type=f32)
    pb = jnp.dot(oa_ref[...], wpb_ref[...], preferred_element_type=f32)
    mix = _sigmoid(ga_ref[...]) * pa + _sigmoid(gb_ref[...]) * pb
    acc_ref[...] += jnp.dot(mix.astype(bf16), wo_ref[...], preferred_element_type=f32)

    @pl.when(j == pl.num_programs(1) - 1)
    def _():
        o_ref[...] = x_ref[...] + acc_ref[...]


def _merge(larr, x, o_r, o_a, z, w_pa, w_pb, w_out, tm):
    m = x.shape[0]
    r_v = RET_HEADS * RET_DV
    a_q = N_HEADS * HEAD_DIM
    return _call(
        _merge_kernel, grid=(m // tm, D_MODEL // TN),
        in_specs=[
            pl.BlockSpec((tm, D_MODEL), lambda i, j, l: (i, 0)),
            pl.BlockSpec((tm, r_v), lambda i, j, l: (i, 0)),
            pl.BlockSpec((tm, a_q), lambda i, j, l: (i, 0)),
            pl.BlockSpec((tm, TN), lambda i, j, l: (i, C_GA // TN + j)),
            pl.BlockSpec((tm, TN), lambda i, j, l: (i, C_GB // TN + j)),
            pl.BlockSpec((None, r_v, TN), lambda i, j, l: (l[0], 0, j)),
            pl.BlockSpec((None, a_q, TN), lambda i, j, l: (l[0], 0, j)),
            pl.BlockSpec((None, TN, D_MODEL), lambda i, j, l: (l[0], j, 0)),
        ],
        out_specs=pl.BlockSpec((tm, D_MODEL), lambda i, j, l: (i, 0)),
        out_shape=jax.ShapeDtypeStruct((m, D_MODEL), f32),
        scratch=[pltpu.VMEM((tm, D_MODEL), f32)],
        name="merge",
    )(larr, x, o_r, o_a, z, z, w_pa, w_pb, w_out)


def _rope_tables(pos):
    half = HEAD_DIM // 2
    inv = 1.0 / (ROPE_THETA ** (jnp.arange(half, dtype=f32) / half))
    ang = pos.astype(f32)[:, None] * inv[None, :]
    cos, sin = jnp.cos(ang), jnp.sin(ang)
    return jnp.concatenate([cos, cos], axis=-1), jnp.concatenate([-sin, sin], axis=-1)


def _cmp_to_sel_table(n_c_valid, n_sel, rows, cols):
    cs = np.arange(n_c_valid) * CMP_STRIDE
    ce = cs + CMP_BLOCK - 1
    js = np.arange(n_sel) * SEL_BLOCK
    je = js + SEL_BLOCK - 1
    tab = np.zeros((rows, cols), np.float32)
    tab[:n_c_valid, :n_sel] = (cs[:, None] <= je[None, :]) & (ce[:, None] >= js[None, :])
    return jnp.asarray(tab, dtype=bf16)


def _expand_table(rows, n_keys, first_block=0):
    tab = np.zeros((rows, n_keys), np.float32)
    s = np.arange(n_keys)
    tab[first_block + s // SEL_BLOCK, s] = 1.0
    return tab


def _round_up(a, b):
    return -(-a // b) * b


def kernel(x_prompt, x_sample, cache_kv, state_win, state_ret, page_table, norm_gain, ffn_gate,
           ffn_up, ffn_down, w_in, qk_norm, cmp_w, ret_gn, w_pa, w_pb, w_out):
    nb, t_len, _ = x_prompt.shape
    db, tn, _ = x_sample.shape
    depth, n_pool, page = cache_kv.shape[:3]
    n_pages = page_table.shape[1]
    past = n_pages * page
    wc = state_win.shape[2]
    kvw = N_KV_HEADS * HEAD_DIM
    assert t_len % 512 == 0 and t_len >= WINDOW and wc == WINDOW and tn == 8
    assert n_pages % PAGES_PER_STEP == 0 and past % SEL_BLOCK == 0 and tn <= CMP_STRIDE

    fpad = F_PAD - D_FF
    wg = jnp.pad(ffn_gate.astype(bf16), ((0, 0), (0, 0), (0, 0), (0, fpad)))
    wu = jnp.pad(ffn_up.astype(bf16), ((0, 0), (0, 0), (0, 0), (0, fpad)))
    wd = jnp.pad(ffn_down.astype(bf16), ((0, 0), (0, 0), (0, fpad), (0, 0)))
    w_in_b = w_in.astype(bf16)
    w_in_p = jnp.concatenate(
        [w_in_b[..., :W_IN_SPLIT], jnp.zeros((depth, D_MODEL, C_GA - W_IN_SPLIT), bf16), w_in_b[..., W_IN_SPLIT:]],
        axis=-1)
    w_pa_b, w_pb_b, w_out_b = w_pa.astype(bf16), w_pb.astype(bf16), w_out.astype(bf16)

    gains = norm_gain.reshape(depth, 3, 1, D_MODEL)
    ones = jnp.ones((depth, HEAD_DIM), f32)
    tile_rows = []
    for j in range(N_ZT):
        c0 = j * TN
        if c0 == C_KC:
            tile_rows.append(qk_norm[:, 1])
        elif c0 == C_KS:
            tile_rows.append(qk_norm[:, 2])
        elif c0 == C_KW:
            tile_rows.append(qk_norm[:, 3])
        elif C_AQ <= c0 < C_KC:
            tile_rows.append(qk_norm[:, 0])
        elif C_RK <= c0 < C_RV:
            tile_rows.append(ones * (RET_DK ** -0.5))
        else:
            tile_rows.append(ones)
    tile_gain = jnp.stack(tile_rows, axis=1).reshape(depth, N_ZT, 1, HEAD_DIM)
    modes = jnp.asarray(_TILE_MODE, jnp.int32)

    w_cmp = jnp.repeat(cmp_w, HEAD_DIM, axis=-1)
    w_cmp = w_cmp.transpose(0, 2, 1, 3).reshape(depth, CMP_BLOCK, 2 * kvw)
    gn = ret_gn.reshape(depth, RET_HEADS, 1, RET_DV)

    cos_p, sin_p = _rope_tables(jnp.arange(t_len, dtype=jnp.int32))
    cos_s, sin_s = _rope_tables(jnp.tile(past + jnp.arange(tn, dtype=jnp.int32), db))

    n_c_p = t_len // CMP_STRIDE
    n_sel_p = t_len // SEL_BLOCK
    mcs_p = _cmp_to_sel_table((t_len - CMP_BLOCK) // CMP_STRIDE + 1, n_sel_p, n_c_p, LANE)
    emat_p = jnp.asarray(_expand_table(LANE, t_len), dtype=bf16)
    l_full = past + tn
    n_sel_s = -(-l_full // SEL_BLOCK)
    n_c_s = past // CMP_STRIDE
    sel_w = _round_up(n_sel_s, LANE)
    mcs_s = _cmp_to_sel_table((l_full - CMP_BLOCK) // CMP_STRIDE + 1, n_sel_s, n_c_s, sel_w)
    steps = n_pages // PAGES_PER_STEP
    keys_per_step = PAGES_PER_STEP * page
    blocks_per_step = keys_per_step // SEL_BLOCK
    emat_s = jnp.asarray(_expand_table(LANE, keys_per_step), dtype=bf16)

    cache2 = cache_kv.reshape(depth * n_pool, page, 4 * kvw)
    win2 = state_win.reshape(depth * db, wc, 2 * kvw)
    sret2 = state_ret.reshape(depth * db, RET_HEADS, RET_DK, RET_DV)
    zero_state = jnp.zeros((nb, RET_HEADS, RET_DK, RET_DV), f32)

    mp = nb * t_len
    ms = db * tn
    tm_p = 512
    tq = 256
    c_p = math.gcd(t_len, RET_CHUNK)

    def layer(carry, l):
        xp, xs = carry
        larr = jnp.reshape(l, (1,)).astype(jnp.int32)

        xp = _ffn(larr, xp, gains, 0, wg, wu, wd, 0, tm_p)
        xs = _ffn(larr, xs, gains, 0, wg, wu, wd, 0, ms)

        zp = _inproj(larr, modes, xp, gains, w_in_p, tile_gain, cos_p, sin_p, tm_p)
        o_r, s_fin = _retention(larr, zp, zero_state, 0, gn, nb, t_len // c_p, c_p, bf16)
        ca, cb = _compress_prompt(larr, zp, w_cmp, 512)
        o_a = _nsa_prompt(larr, zp, ca, cb, mcs_p, emat_p, nb, t_len, tq)
        xp = _merge(larr, xp, o_r, o_a, zp, w_pa_b, w_pb_b, w_out_b, tm_p)
        kv_p = zp[:, C_KC:C_KW].reshape(nb, t_len, 4, N_KV_HEADS, HEAD_DIM)
        win_p = zp[:, C_KW:C_AG].reshape(nb, t_len, 2, N_KV_HEADS, HEAD_DIM)[:, t_len - min(WINDOW, t_len):]

        zs = _inproj(larr, modes, xs, gains, w_in_p, tile_gain, cos_s, sin_s, ms)
        o_rs, s_new = _retention(larr, zs, sret2, db, gn, db, 1, tn, f32)
        sa, sb = _compress_pages(larr, page_table, cache2, w_cmp, n_pool)
        o_c, sel = _nsa_s_cmp(larr, zs, sa, sb, mcs_s, db, tn, past, n_sel_s)
        sel_steps = sel[..., :steps * blocks_per_step].reshape(db, N_KV_HEADS, tn, steps, blocks_per_step)
        sel_steps = jnp.pad(sel_steps.transpose(0, 3, 1, 2, 4), ((0, 0),) * 4 + ((0, LANE - blocks_per_step),))
        acc, mx, lsum = _nsa_s_sel(larr, page_table, zs, sel_steps, emat_s, cache2, n_pool, tn)
        o_as = _nsa_s_fin(larr, zs, win2, acc, mx, lsum, o_c, sel, db, tn, past, n_sel_s)
        xs = _merge(larr, xs, o_rs.astype(bf16), o_as.astype(bf16), zs, w_pa_b, w_pb_b, w_out_b, ms)
        kv_s = zs[:, C_KC:C_KW].reshape(db, tn, 4, N_KV_HEADS, HEAD_DIM)
        win_new = zs[:, C_KW:C_AG].reshape(db, tn, 2, N_KV_HEADS, HEAD_DIM)
        win_old = lax.dynamic_index_in_dim(state_win, l, 0, keepdims=False)
        win_s = jnp.concatenate([win_old[:, tn:], win_new], axis=1)

        xp = _ffn(larr, xp, gains, 2, wg, wu, wd, 1, tm_p)
        xs = _ffn(larr, xs, gains, 2, wg, wu, wd, 1, ms)
        return (xp, xs), (kv_p, kv_s, win_p, win_s, s_fin, s_new)

    (xp, xs), outs = lax.scan(layer, (x_prompt.reshape(mp, D_MODEL), x_sample.reshape(ms, D_MODEL)),
                              jnp.arange(depth, dtype=jnp.int32))
    kv_p, kv_s, win_p, win_s, ret_p, ret_s = outs
    return (xp.reshape(nb, t_len, D_MODEL), xs.reshape(db, tn, D_MODEL), kv_p, kv_s, win_p, win_s, ret_p, ret_s)
```

```python
import functools
import math

import jax
import jax.numpy as jnp
import numpy as np
from jax import lax
from jax.experimental import pallas as pl
from jax.experimental.pallas import tpu as pltpu

D_MODEL = 2048
D_FF = 5504
RET_HEADS = 8
RET_DK = 128
RET_DV = 256
RET_CHUNK = 128
N_HEADS = 16
N_KV_HEADS = 4
HEAD_DIM = 128
GROUP = N_HEADS // N_KV_HEADS
CMP_BLOCK = 32
CMP_STRIDE = 16
SEL_BLOCK = 64
N_SEL = 16
WINDOW = 512
ROPE_THETA = 10000.0
EPS = 1e-6
NEG = -1e30
BIG = 1e30
SCALE = HEAD_DIM ** -0.5
LOG2E = math.log2(math.e)

LANE = 128
VMEM_LIMIT = 56 * 1024 * 1024

TN = 512
C_RQ, C_RK, C_RV, C_RG = 0, 1024, 2048, 4096
C_AQ = 6144
C_KC, C_VC, C_KS, C_VS, C_KW, C_VW = 8192, 8704, 9216, 9728, 10240, 10752
C_AG = 11264
C_GA, C_GB = 11776, 13824
DZ = 15872
N_ZT = DZ // TN
W_IN_SPLIT = 11312
_TILE_MODE = [1, 1, 1, 1] + [0] * 8 + [2, 2, 2, 2] + [2, 0, 2, 0, 2, 0] + [0] * 9

F_PAD = 5632
TF = 512
PAGES_PER_STEP = 8

_RET_LOG_G = [float(np.log(np.float32(1.0) - np.float32(2.0) ** np.float32(-5.0 - h))) for h in range(RET_HEADS)]

_NT = (((1,), (1,)), ((), ()))
_TN = (((0,), (0,)), ((), ()))

bf16 = jnp.bfloat16
f32 = jnp.float32


def _sigmoid(x):
    return 1.0 / (1.0 + jnp.exp(-x))


def _call(kernel, *, grid, in_specs, out_specs, out_shape, scratch=(), nsp=1, sem=None, name=None):
    return pl.pallas_call(
        kernel,
        grid_spec=pltpu.PrefetchScalarGridSpec(num_scalar_prefetch=nsp, grid=grid, in_specs=in_specs,
                                               out_specs=out_specs, scratch_shapes=list(scratch)),
        out_shape=out_shape,
        compiler_params=pltpu.CompilerParams(dimension_semantics=sem or ("arbitrary",) * len(grid),
                                             vmem_limit_bytes=VMEM_LIMIT),
        name=name,
    )


def _ffn_kernel(l_ref, x_ref, g_ref, wg_ref, wu_ref, wd_ref, o_ref, h_ref, acc_ref):
    f = pl.program_id(1)

    @pl.when(f == 0)
    def _():
        x = x_ref[...]
        ms = jnp.mean(x * x, axis=-1, keepdims=True)
        h_ref[...] = (x * lax.rsqrt(ms + EPS) * g_ref[...]).astype(bf16)
        acc_ref[...] = jnp.zeros_like(acc_ref)

    h = h_ref[...]
    a = jnp.dot(h, wg_ref[...], preferred_element_type=f32)
    b = jnp.dot(h, wu_ref[...], preferred_element_type=f32)
    s = (a * _sigmoid(a)) * b
    acc_ref[...] += jnp.dot(s.astype(bf16), wd_ref[...], preferred_element_type=f32)

    @pl.when(f == pl.num_programs(1) - 1)
    def _():
        o_ref[...] = x_ref[...] + 0.5 * acc_ref[...]


def _ffn(larr, x, gains, which_gain, wg, wu, wd, which_w, tm):
    m = x.shape[0]
    grid = (m // tm, F_PAD // TF)
    return _call(
        _ffn_kernel, grid=grid,
        in_specs=[
            pl.BlockSpec((tm, D_MODEL), lambda i, f, l: (i, 0)),
            pl.BlockSpec((None, None, 1, D_MODEL), lambda i, f, l: (l[0], which_gain, 0, 0)),
            pl.BlockSpec((None, None, D_MODEL, TF), lambda i, f, l: (l[0], which_w, 0, f)),
            pl.BlockSpec((None, None, D_MODEL, TF), lambda i, f, l: (l[0], which_w, 0, f)),
            pl.BlockSpec((None, None, TF, D_MODEL), lambda i, f, l: (l[0], which_w, f, 0)),
        ],
        out_specs=pl.BlockSpec((tm, D_MODEL), lambda i, f, l: (i, 0)),
        out_shape=jax.ShapeDtypeStruct((m, D_MODEL), f32),
        scratch=[pltpu.VMEM((tm, D_MODEL), bf16), pltpu.VMEM((tm, D_MODEL), f32)],
        name="ffn",
    )(larr, x, gains, wg, wu, wd)


def _inproj_kernel(l_ref, mode_ref, x_ref, g_ref, w_ref, gain_ref, cos_ref, sin_ref, o_ref, h_ref):
    j = pl.program_id(1)

    @pl.when(j == 0)
    def _():
        x = x_ref[...]
        ms = jnp.mean(x * x, axis=-1, keepdims=True)
        h_ref[...] = (x * lax.rsqrt(ms + EPS) * g_ref[...]).astype(bf16)

    acc = jnp.dot(h_ref[...], w_ref[...], preferred_element_type=f32)
    mode = mode_ref[j]

    @pl.when(mode == 0)
    def _():
        o_ref[...] = acc

    @pl.when(mode != 0)
    def _():
        cos = cos_ref[...]
        sin = sin_ref[...]
        gain = gain_ref[...]
        for hd in range(TN // HEAD_DIM):
            y = acc[:, hd * HEAD_DIM:(hd + 1) * HEAD_DIM]
            ms = jnp.mean(y * y, axis=-1, keepdims=True)
            inv = jnp.where(mode == 2, lax.rsqrt(ms + EPS), 1.0)
            y = y * inv * gain
            o_ref[:, hd * HEAD_DIM:(hd + 1) * HEAD_DIM] = y * cos + pltpu.roll(y, HEAD_DIM // 2, 1) * sin


def _inproj(larr, modes, x, gains, w_in, tile_gain, cos2, sin2, tm):
    m = x.shape[0]
    nt = cos2.shape[0] // tm
    grid = (m // tm, N_ZT)
    return _call(
        _inproj_kernel, grid=grid, nsp=2,
        in_specs=[
            pl.BlockSpec((tm, D_MODEL), lambda i, j, l, md: (i, 0)),
            pl.BlockSpec((None, None, 1, D_MODEL), lambda i, j, l, md: (l[0], 1, 0, 0)),
            pl.BlockSpec((None, D_MODEL, TN), lambda i, j, l, md: (l[0], 0, j)),
            pl.BlockSpec((None, None, 1, HEAD_DIM), lambda i, j, l, md: (l[0], j, 0, 0)),
            pl.BlockSpec((tm, HEAD_DIM), lambda i, j, l, md: (i % nt, 0)),
            pl.BlockSpec((tm, HEAD_DIM), lambda i, j, l, md: (i % nt, 0)),
        ],
        out_specs=pl.BlockSpec((tm, TN), lambda i, j, l, md: (i, j)),
        out_shape=jax.ShapeDtypeStruct((m, DZ), f32),
        scratch=[pltpu.VMEM((tm, D_MODEL), bf16)],
        name="inproj",
    )(larr, modes, x, gains, w_in, tile_gain, cos2, sin2)


def _ret_kernel(l_ref, q_ref, k_ref, v_ref, rg_ref, s0_ref, gn_ref, o_ref, sout_ref, s_scr, *, c_true, c_pad):
    c = pl.program_id(1)

    @pl.when(c == 0)
    def _():
        s_scr[...] = s0_ref[...]

    ri = lax.broadcasted_iota(jnp.int32, (c_pad, c_pad), 0)
    ci = lax.broadcasted_iota(jnp.int32, (c_pad, c_pad), 1)
    diff = ri - ci
    row = lax.broadcasted_iota(jnp.int32, (c_pad, 1), 0)

    def padded(a):
        if c_pad == c_true:
            return a
        return jnp.concatenate([a, jnp.zeros((c_pad - c_true, a.shape[1]), a.dtype)], axis=0)

    for h in range(RET_HEADS):
        lg = _RET_LOG_G[h]
        dmat = jnp.where(diff >= 0, jnp.exp(jnp.maximum(diff, 0).astype(f32) * lg), 0.0)
        q = padded(q_ref[:, h * RET_DK:(h + 1) * RET_DK])
        k = padded(k_ref[:, h * RET_DK:(h + 1) * RET_DK])
        v = padded(v_ref[:, h * RET_DV:(h + 1) * RET_DV])
        qb, kb, vb = q.astype(bf16), k.astype(bf16), v.astype(bf16)
        inner = lax.dot_general(qb, kb, _NT, preferred_element_type=f32) * dmat
        xi = jnp.exp((row + 1).astype(f32) * lg)
        state = s_scr[h]
        o = (jnp.dot(inner.astype(bf16), vb, preferred_element_type=f32)
             + jnp.dot(qb, state.astype(bf16), preferred_element_type=f32) * xi)
        wk = jnp.exp((c_true - 1 - row).astype(f32) * lg)
        kw = (k * wk).astype(bf16)
        s_scr[h] = state * float(np.exp(np.float32(c_true * lg))) + lax.dot_general(
            kw, vb, _TN, preferred_element_type=f32)
        o = o[:c_true]
        mu = jnp.mean(o, axis=-1, keepdims=True)
        cen = o - mu
        var = jnp.mean(cen * cen, axis=-1, keepdims=True)
        y = cen * lax.rsqrt(var + EPS) * gn_ref[h]
        rg = rg_ref[:, h * RET_DV:(h + 1) * RET_DV]
        o_ref[:, h * RET_DV:(h + 1) * RET_DV] = (y * (rg * _sigmoid(rg))).astype(o_ref.dtype)

    @pl.when(c == pl.num_programs(1) - 1)
    def _():
        sout_ref[...] = s_scr[...]


def _retention(larr, z, s0, s0_per_layer, ret_gn, nb, nc, c_true, out_dtype):
    c_pad = max(c_true, RET_CHUNK)
    m = z.shape[0]
    r_qk = RET_HEADS * RET_DK
    r_v = RET_HEADS * RET_DV
    kern = functools.partial(_ret_kernel, c_true=c_true, c_pad=c_pad)
    return _call(
        kern, grid=(nb, nc),
        in_specs=[
            pl.BlockSpec((c_true, r_qk), lambda b, c, l: (b * nc + c, C_RQ // r_qk)),
            pl.BlockSpec((c_true, r_qk), lambda b, c, l: (b * nc + c, C_RK // r_qk)),
            pl.BlockSpec((c_true, r_v), lambda b, c, l: (b * nc + c, C_RV // r_v)),
            pl.BlockSpec((c_true, r_v), lambda b, c, l: (b * nc + c, C_RG // r_v)),
            pl.BlockSpec((None, RET_HEADS, RET_DK, RET_DV), lambda b, c, l: (l[0] * s0_per_layer + b, 0, 0, 0)),
            pl.BlockSpec((None, RET_HEADS, 1, RET_DV), lambda b, c, l: (l[0], 0, 0, 0)),
        ],
        out_specs=[
            pl.BlockSpec((c_true, r_v), lambda b, c, l: (b * nc + c, 0)),
            pl.BlockSpec((None, RET_HEADS, RET_DK, RET_DV), lambda b, c, l: (b, 0, 0, 0)),
        ],
        out_shape=[jax.ShapeDtypeStruct((m, r_v), out_dtype),
                   jax.ShapeDtypeStruct((nb, RET_HEADS, RET_DK, RET_DV), f32)],
        scratch=[pltpu.VMEM((RET_HEADS, RET_DK, RET_DV), f32)],
        name="retention",
    )(larr, z, z, z, z, s0, ret_gn)


def _cmp_kernel(*refs, n_in, n_scalar):
    x_refs = refs[n_scalar:n_scalar + n_in]
    w_ref, a_ref, b_ref = refs[n_scalar + n_in:]
    wa = w_ref[0:CMP_STRIDE]
    wb = w_ref[CMP_STRIDE:CMP_BLOCK]
    for k in range(n_in):
        x = x_refs[k][...]
        r = x.shape[0] // CMP_STRIDE
        x3 = x.reshape((r, CMP_STRIDE) + x.shape[1:])
        a_ref[k * r:(k + 1) * r] = jnp.sum(x3 * wa[None], axis=1)
        b_ref[k * r:(k + 1) * r] = jnp.sum(x3 * wb[None], axis=1)


def _compress_prompt(larr, z, w_cmp, rows):
    m = z.shape[0]
    wcols = 2 * N_KV_HEADS * HEAD_DIM
    kern = functools.partial(_cmp_kernel, n_in=1, n_scalar=1)
    shp = jax.ShapeDtypeStruct((m // CMP_STRIDE, wcols), f32)
    return _call(
        kern, grid=(m // rows,),
        in_specs=[pl.BlockSpec((rows, wcols), lambda i, l: (i, C_KC // wcols)),
                  pl.BlockSpec((None, CMP_BLOCK, wcols), lambda i, l: (l[0], 0, 0))],
        out_specs=[pl.BlockSpec((rows // CMP_STRIDE, wcols), lambda i, l: (i, 0))] * 2,
        out_shape=[shp, shp], name="compress_prompt",
    )(larr, z, w_cmp)


def _compress_pages(larr, page_table, cache, w_cmp, n_pool):
    db, n_pages = page_table.shape
    page, _, sg, d = cache.shape[1:]
    pps = PAGES_PER_STEP
    steps = n_pages // pps
    sub = page // CMP_STRIDE
    kern = functools.partial(_cmp_kernel, n_in=pps, n_scalar=2)

    def page_spec(k):
        return pl.BlockSpec((None, page, None, sg, d),
                            lambda b, n, l, pt: (l[0] * n_pool + pt[b, n * pps + k], 0, 0, 0, 0))

    shp = jax.ShapeDtypeStruct((db * n_pages * sub, sg, d), f32)
    return _call(
        kern, grid=(db, steps), nsp=2,
        in_specs=[page_spec(k) for k in range(pps)]
        + [pl.BlockSpec((None, CMP_BLOCK, sg, d), lambda b, n, l, pt: (l[0], 0, 0, 0))],
        out_specs=[pl.BlockSpec((pps * sub, sg, d), lambda b, n, l, pt: (b * steps + n, 0, 0))] * 2,
        out_shape=[shp, shp], name="compress_pages",
    )(larr, page_table, *([cache] * pps), w_cmp)


def _combine_cmp(a, b):
    n = a.shape[0]
    row = lax.broadcasted_iota(jnp.int32, (n, 1), 0)
    return a + jnp.where(row == n - 1, 0.0, pltpu.roll(b, n - 1, 0))


def _masked_exp(s, mask, exp=jnp.exp):
    s = jnp.where(mask, s, NEG)
    m = jnp.max(s, axis=-1, keepdims=True)
    e = jnp.where(mask, exp(s - m), 0.0)
    return e, jnp.sum(e, axis=-1, keepdims=True)


def _dot_hilo(p, m01):
    hi = p.astype(bf16)
    lo = (p - hi.astype(f32)).astype(bf16)
    return jnp.dot(hi, m01, preferred_element_type=f32) + jnp.dot(lo, m01, preferred_element_type=f32)


def _topk_mask(imp, k):
    lane = lax.broadcasted_iota(jnp.int32, imp.shape, 1).astype(f32)
    sel = jnp.zeros(imp.shape, f32)
    for _ in range(k):
        m = jnp.max(imp, axis=-1, keepdims=True)
        idx = jnp.min(jnp.where(imp == m, lane, 1e9), axis=-1, keepdims=True)
        hit = lane == idx
        sel = jnp.where(hit, 1.0, sel)
        imp = jnp.where(hit, -jnp.inf, imp)
    return sel


def _block_importance(p_sum, mcs, tpos, n_sel):
    imp = _dot_hilo(p_sum, mcs)
    jj = lax.broadcasted_iota(jnp.int32, (1, imp.shape[1]), 1)
    forced = jnp.where(jj == 0, 1, jnp.where(jj == (tpos >> 6), 1, 0))
    imp = jnp.where(jj * SEL_BLOCK <= tpos, imp, NEG)
    imp = jnp.where(forced == 1, BIG, imp)
    return jnp.where(jj < n_sel, imp, -jnp.inf)


def _tile_row(ref, j):
    n, s, d = ref.shape
    return ref.reshape(n * s, d)[pl.ds(j, n, stride=s), :]


def _gate(sg, idx):
    lane = lax.broadcasted_iota(jnp.int32, (1, sg.shape[1]), 1)
    return jnp.sum(jnp.where(lane == idx, sg, 0.0), axis=-1, keepdims=True)


def _softmax_pv(q, k, v, mask):
    s = jnp.where(mask, lax.dot_general(q, k, _NT, preferred_element_type=f32), NEG)
    e = jnp.exp2(s - jnp.max(s, axis=-1, keepdims=True))
    den = jnp.sum(e, axis=-1, keepdims=True)
    return jnp.dot(e.astype(bf16), v, preferred_element_type=f32) / den


def _nsa_prompt_kernel(l_ref, q_ref, ka_ref, kb_ref, va_ref, vb_ref, ks_ref, vs_ref, kw_ref, vw_ref, ag_ref,
                       mcs_ref, e_ref, o_ref, os_scr, *, tq, t_len, wq, wlen):
    g = pl.program_id(1)
    i = pl.program_id(2)
    t0 = i * tq
    tpos = t0 + lax.broadcasted_iota(jnp.int32, (tq, 1), 0)
    n_c = t_len // CMP_STRIDE
    n_sel = t_len // SEL_BLOCK

    kc = _combine_cmp(ka_ref[...], kb_ref[...]).astype(bf16)
    vc = _combine_cmp(va_ref[...], vb_ref[...]).astype(bf16)
    cend = lax.broadcasted_iota(jnp.int32, (1, n_c), 1) * CMP_STRIDE + (CMP_BLOCK - 1)
    mask_c = cend <= tpos

    qs = [(q_ref[:, r * HEAD_DIM:(r + 1) * HEAD_DIM] * (SCALE * LOG2E)).astype(bf16) for r in range(GROUP)]

    p_sum = jnp.zeros((tq, n_c), f32)
    o_c = []
    for r in range(GROUP):
        s = lax.dot_general(qs[r], kc, _NT, preferred_element_type=f32)
        e, den = _masked_exp(s, mask_c, jnp.exp2)
        p = e / jnp.maximum(den, 1e-30)
        p_sum = p_sum + p
        o_c.append(jnp.dot(p.astype(bf16), vc, preferred_element_type=f32))

    imp = _block_importance(p_sum, mcs_ref[...], tpos, n_sel)
    selb = _topk_mask(imp, min(N_SEL, n_sel)).astype(bf16)

    for br in range(t_len // tq):
        @pl.when(i == br)
        def _(br=br):
            klen = (br + 1) * tq
            selk = jnp.dot(selb, e_ref[:, 0:klen], preferred_element_type=f32)
            kpos = lax.broadcasted_iota(jnp.int32, (1, klen), 1)
            mask_s = jnp.where(kpos <= tpos, selk, 0.0) > 0.5
            ks = ks_ref[0:klen, :].astype(bf16)
            vs = vs_ref[0:klen, :].astype(bf16)
            for r in range(GROUP):
                os_scr[:, r * HEAD_DIM:(r + 1) * HEAD_DIM] = _softmax_pv(qs[r], ks, vs, mask_s)

    sg = _sigmoid(ag_ref[...])
    gates = [[_gate(sg, (g * GROUP + r) * 3 + k) for k in range(3)] for r in range(GROUP)]
    for h in range(tq // wq):
        rs = slice(h * wq, (h + 1) * wq)
        start = pl.multiple_of(jnp.maximum(t0 + (h + 1) * wq - wlen, 0), wq)
        kw = kw_ref[pl.ds(start, wlen), :].astype(bf16)
        vw = vw_ref[pl.ds(start, wlen), :].astype(bf16)
        d = tpos[rs] - (start + lax.broadcasted_iota(jnp.int32, (1, wlen), 1))
        mask_w = jnp.where(d >= 0, d, WINDOW) < WINDOW
        for r in range(GROUP):
            cs = slice(r * HEAD_DIM, (r + 1) * HEAD_DIM)
            o_w = _softmax_pv(qs[r][rs], kw, vw, mask_w)
            out = gates[r][0][rs] * o_c[r][rs] + gates[r][1][rs] * os_scr[rs, cs] + gates[r][2][rs] * o_w
            o_ref[rs, cs] = out.astype(o_ref.dtype)


def _nsa_prompt(larr, z, cmp_a, cmp_b, mcs, emat, nb, t_len, tq):
    m = z.shape[0]
    nq = t_len // tq
    n_c = t_len // CMP_STRIDE
    wq = min(tq, 256)
    wlen = min(WINDOW + wq, t_len)
    gw = GROUP * HEAD_DIM
    kern = functools.partial(_nsa_prompt_kernel, tq=tq, t_len=t_len, wq=wq, wlen=wlen)

    def head_cols(c0):
        return pl.BlockSpec((t_len, HEAD_DIM), lambda b, g, i, l: (b, c0 // HEAD_DIM + g))

    def cmp_spec(off):
        return pl.BlockSpec((n_c, HEAD_DIM), lambda b, g, i, l: (b, off + g))

    return _call(
        kern, grid=(nb, N_KV_HEADS, nq),
        in_specs=[
            pl.BlockSpec((tq, gw), lambda b, g, i, l: (b * nq + i, C_AQ // gw + g)),
            cmp_spec(0), cmp_spec(0), cmp_spec(N_KV_HEADS), cmp_spec(N_KV_HEADS),
            head_cols(C_KS), head_cols(C_VS), head_cols(C_KW), head_cols(C_VW),
            pl.BlockSpec((tq, LANE), lambda b, g, i, l: (b * nq + i, C_AG // LANE)),
            pl.BlockSpec(mcs.shape, lambda b, g, i, l: (0, 0)),
            pl.BlockSpec(emat.shape, lambda b, g, i, l: (0, 0)),
        ],
        out_specs=pl.BlockSpec((tq, gw), lambda b, g, i, l: (b * nq + i, g)),
        out_shape=jax.ShapeDtypeStruct((m, N_HEADS * HEAD_DIM), bf16),
        scratch=[pltpu.VMEM((tq, gw), f32)],
        name="nsa_prompt",
    )(larr, z, cmp_a, cmp_b, cmp_a, cmp_b, z, z, z, z, z, mcs, emat)


def _rows_rt(ref, g_off, tn):
    return jnp.concatenate(
        [ref[:, (g_off + r) * HEAD_DIM:(g_off + r + 1) * HEAD_DIM] for r in range(GROUP)], axis=0)


def _nsa_s_cmp_kernel(l_ref, q_ref, a_ref, b_ref, mcs_ref, oc_ref, sel_ref, *, tn, past, n_sel):
    n_c = a_ref.shape[0]
    rows = GROUP * tn
    tpos_r = past + (lax.broadcasted_iota(jnp.int32, (rows, 1), 0) % tn)
    tpos = past + (lax.broadcasted_iota(jnp.int32, (2 * tn, 1), 0) % tn)
    cend = lax.broadcasted_iota(jnp.int32, (1, n_c), 1) * CMP_STRIDE + (CMP_BLOCK - 1)
    mask_c = cend <= tpos_r
    mcs = mcs_ref[...]
    for g in range(N_KV_HEADS):
        q = (_rows_rt(q_ref, g * GROUP, tn) * SCALE).astype(bf16)
        kc = _combine_cmp(_tile_row(a_ref, g), _tile_row(b_ref, g)).astype(bf16)
        vc = _combine_cmp(_tile_row(a_ref, N_KV_HEADS + g), _tile_row(b_ref, N_KV_HEADS + g)).astype(bf16)
        s = lax.dot_general(q, kc, _NT, preferred_element_type=f32)
        e, den = _masked_exp(s, mask_c)
        p = e / jnp.maximum(den, 1e-30)
        oc_ref[g] = jnp.dot(p.astype(bf16), vc, preferred_element_type=f32)
        p_sum = p[0:tn]
        for r in range(1, GROUP):
            p_sum = p_sum + p[r * tn:(r + 1) * tn]
        p_sum = jnp.concatenate([p_sum, jnp.zeros_like(p_sum)], axis=0)
        imp = _block_importance(p_sum, mcs, tpos, n_sel)
        sel_ref[g] = _topk_mask(imp, min(N_SEL, n_sel))[0:tn]


def _nsa_s_cmp(larr, z, cmp_a, cmp_b, mcs, db, tn, past, n_sel):
    n_c = cmp_a.shape[0] // db
    aq = N_HEADS * HEAD_DIM
    kern = functools.partial(_nsa_s_cmp_kernel, tn=tn, past=past, n_sel=n_sel)
    cmp_spec = pl.BlockSpec((n_c,) + cmp_a.shape[1:], lambda b, l: (b, 0, 0))
    return _call(
        kern, grid=(db,),
        in_specs=[pl.BlockSpec((tn, aq), lambda b, l: (b, C_AQ // aq)), cmp_spec, cmp_spec,
                  pl.BlockSpec(mcs.shape, lambda b, l: (0, 0))],
        out_specs=[pl.BlockSpec((None, N_KV_HEADS, GROUP * tn, HEAD_DIM), lambda b, l: (b, 0, 0, 0)),
                   pl.BlockSpec((None, N_KV_HEADS, tn, mcs.shape[1]), lambda b, l: (b, 0, 0, 0))],
        out_shape=[jax.ShapeDtypeStruct((db, N_KV_HEADS, GROUP * tn, HEAD_DIM), f32),
                   jax.ShapeDtypeStruct((db, N_KV_HEADS, tn, mcs.shape[1]), f32)],
        name="nsa_sample_cmp",
    )(larr, z, cmp_a, cmp_b, mcs)


def _nsa_s_sel_kernel(*refs, tn, pps):
    q_ref, sel_ref, e_ref = refs[2:5]
    page_refs = refs[5:5 + pps]
    acc_ref, m_ref, l_ref = refs[5 + pps:]
    n = pl.program_id(1)

    @pl.when(n == 0)
    def _():
        acc_ref[...] = jnp.zeros_like(acc_ref)
        m_ref[...] = jnp.full_like(m_ref, NEG)
        l_ref[...] = jnp.zeros_like(l_ref)

    emat = e_ref[...]
    for g in range(N_KV_HEADS):
        q = (_rows_rt(q_ref, g * GROUP, tn) * SCALE).astype(bf16)
        s = jnp.concatenate(
            [lax.dot_general(q, _tile_row(pr, g).astype(bf16), _NT, preferred_element_type=f32)
             for pr in page_refs], axis=1)
        selg = sel_ref[g]
        selg = jnp.concatenate([selg, jnp.zeros_like(selg)], axis=0).astype(bf16)
        selk = jnp.dot(selg, emat, preferred_element_type=f32)[0:tn]
        mask = jnp.concatenate([selk] * GROUP, axis=0) > 0.5
        s = jnp.where(mask, s, NEG)
        m_old = m_ref[g]
        m_new = jnp.maximum(m_old, jnp.max(s, axis=-1, keepdims=True))
        alpha = jnp.exp(m_old - m_new)
        e = jnp.where(mask, jnp.exp(s - m_new[:, 0:1]), 0.0)
        l_ref[g] = alpha * l_ref[g] + jnp.sum(e, axis=-1, keepdims=True)
        pv = None
        for k, pr in enumerate(page_refs):
            page = pr.shape[0]
            ek = e[:, k * page:(k + 1) * page].astype(bf16)
            vk = _tile_row(pr, N_KV_HEADS + g).astype(bf16)
            d = jnp.dot(ek, vk, preferred_element_type=f32)
            pv = d if pv is None else pv + d
        acc_ref[g] = alpha * acc_ref[g] + pv
        m_ref[g] = m_new


def _nsa_s_sel(larr, page_table, z, sel, emat_pages, cache, n_pool, tn):
    db, n_pages = page_table.shape
    page, _, sg, d = cache.shape[1:]
    pps = PAGES_PER_STEP
    steps = n_pages // pps
    rows = GROUP * tn
    kern = functools.partial(_nsa_s_sel_kernel, tn=tn, pps=pps)

    def page_spec(k):
        return pl.BlockSpec((None, page, None, sg, d),
                            lambda b, n, l, pt: (l[0] * n_pool + pt[b, n * pps + k], 0, 1, 0, 0))

    st = jax.ShapeDtypeStruct((db, N_KV_HEADS, rows, HEAD_DIM), f32)
    st_spec = pl.BlockSpec((None, N_KV_HEADS, rows, HEAD_DIM), lambda b, n, l, pt: (b, 0, 0, 0))
    return _call(
        kern, grid=(db, steps), nsp=2,
        in_specs=[pl.BlockSpec((tn, N_HEADS * HEAD_DIM), lambda b, n, l, pt: (b, C_AQ // (N_HEADS * HEAD_DIM))),
                  pl.BlockSpec((None, None, N_KV_HEADS, tn, LANE), lambda b, n, l, pt: (b, n, 0, 0, 0)),
                  pl.BlockSpec(emat_pages.shape, lambda b, n, l, pt: (0, 0))]
        + [page_spec(k) for k in range(pps)],
        out_specs=[st_spec, st_spec, st_spec],
        out_shape=[st, st, st], name="nsa_sample_sel",
    )(larr, page_table, z, sel, emat_pages, *([cache] * pps))


def _nsa_s_fin_kernel(l_ref, q_ref, ksn_ref, vsn_ref, kwn_ref, vwn_ref, ag_ref, buf_ref,
                      acc_ref, m_ref, lsum_ref, oc_ref, sel_ref, o_ref, *, tn, past, n_sel):
    rows = GROUP * tn
    wc = buf_ref.shape[0]
    tpos = past + (lax.broadcasted_iota(jnp.int32, (rows, 1), 0) % tn)
    kidx = lax.broadcasted_iota(jnp.int32, (1, LANE), 1)
    widx = lax.broadcasted_iota(jnp.int32, (1, wc + LANE), 1)
    d = tpos - (past - wc + widx)
    mask_w = jnp.where(widx < wc + tn, jnp.where(d >= 0, d, WINDOW), WINDOW) < WINDOW
    sg = _sigmoid(ag_ref[...])

    def new_rows(ref, g):
        a = ref[:, g * HEAD_DIM:(g + 1) * HEAD_DIM]
        return jnp.concatenate([a, jnp.zeros((LANE - tn, HEAD_DIM), a.dtype)], axis=0)

    for g in range(N_KV_HEADS):
        q = (_rows_rt(q_ref, g * GROUP, tn) * SCALE).astype(bf16)

        ksn = new_rows(ksn_ref, g).astype(bf16)
        vsn = new_rows(vsn_ref, g).astype(bf16)
        s = lax.dot_general(q, ksn, _NT, preferred_element_type=f32)
        last_sel = sel_ref[g][:, n_sel - 1:n_sel]
        last_sel = jnp.concatenate([last_sel] * GROUP, axis=0)
        mask = jnp.where(kidx < tn, jnp.where(past + kidx <= tpos, last_sel, 0.0), 0.0) > 0.5
        s = jnp.where(mask, s, NEG)
        m_old = m_ref[g]
        m_new = jnp.maximum(m_old, jnp.max(s, axis=-1, keepdims=True))
        alpha = jnp.exp(m_old - m_new)
        e = jnp.where(mask, jnp.exp(s - m_new[:, 0:1]), 0.0)
        den = alpha * lsum_ref[g] + jnp.sum(e, axis=-1, keepdims=True)
        o_s = (alpha * acc_ref[g] + jnp.dot(e.astype(bf16), vsn, preferred_element_type=f32)) / jnp.maximum(den, 1e-30)

        kw = jnp.concatenate([_tile_row(buf_ref, g), new_rows(kwn_ref, g)], axis=0).astype(bf16)
        vw = jnp.concatenate([_tile_row(buf_ref, N_KV_HEADS + g), new_rows(vwn_ref, g)], axis=0).astype(bf16)
        s = lax.dot_general(q, kw, _NT, preferred_element_type=f32)
        e, den = _masked_exp(s, mask_w)
        o_w = jnp.dot(e.astype(bf16), vw, preferred_element_type=f32) / jnp.maximum(den, 1e-30)

        o_c = oc_ref[g]
        for r in range(GROUP):
            base = (g * GROUP + r) * 3
            sl = slice(r * tn, (r + 1) * tn)
            out = _gate(sg, base) * o_c[sl] + _gate(sg, base + 1) * o_s[sl] + _gate(sg, base + 2) * o_w[sl]
            o_ref[:, (g * GROUP + r) * HEAD_DIM:(g * GROUP + r + 1) * HEAD_DIM] = out


def _nsa_s_fin(larr, z, win_buf, acc, mx, lsum, o_c, sel, db, tn, past, n_sel):
    aq = N_HEADS * HEAD_DIM
    kvw = N_KV_HEADS * HEAD_DIM
    rows = GROUP * tn
    kern = functools.partial(_nsa_s_fin_kernel, tn=tn, past=past, n_sel=n_sel)

    def new_cols(c0):
        return pl.BlockSpec((tn, kvw), lambda b, l: (b, c0 // kvw))

    st_spec = pl.BlockSpec((None, N_KV_HEADS, rows, HEAD_DIM), lambda b, l: (b, 0, 0, 0))
    return _call(
        kern, grid=(db,),
        in_specs=[pl.BlockSpec((tn, aq), lambda b, l: (b, C_AQ // aq)),
                  new_cols(C_KS), new_cols(C_VS), new_cols(C_KW), new_cols(C_VW),
                  pl.BlockSpec((tn, LANE), lambda b, l: (b, C_AG // LANE)),
                  pl.BlockSpec((None,) + win_buf.shape[1:], lambda b, l: (l[0] * db + b, 0, 0, 0)),
                  st_spec, st_spec, st_spec, st_spec,
                  pl.BlockSpec((None, N_KV_HEADS, tn, sel.shape[-1]), lambda b, l: (b, 0, 0, 0))],
        out_specs=pl.BlockSpec((tn, aq), lambda b, l: (b, 0)),
        out_shape=jax.ShapeDtypeStruct((db * tn, aq), f32),
        name="nsa_sample_fin",
    )(larr, z, z, z, z, z, z, win_buf, acc, mx, lsum, o_c, sel)


def _merge_kernel(l_ref, x_ref, or_ref, oa_ref, ga_ref, gb_ref, wpa_ref, wpb_ref, wo_ref, o_ref, acc_ref):
    j = pl.program_id(1)

    @pl.when(j == 0)
    def _():
        acc_ref[...] = jnp.zeros_like(acc_ref)

    pa = jnp.dot(or_ref[...], wpa_ref[...], preferred_element_type=f32)
    pb = jnp.dot(oa_ref[...], wpb_ref[...], preferred_element_type=f32)
    mix = _sigmoid(ga_ref[...]) * pa + _sigmoid(gb_ref[...]) * pb
    acc_ref[...] += jnp.dot(mix.astype(bf16), wo_ref[...], preferred_element_type=f32)

    @pl.when(j == pl.num_programs(1) - 1)
    def _():
        o_ref[...] = x_ref[...] + acc_ref[...]


def _merge(larr, x, o_r, o_a, z, w_pa, w_pb, w_out, tm):
    m = x.shape[0]
    r_v = RET_HEADS * RET_DV
    a_q = N_HEADS * HEAD_DIM
    return _call(
        _merge_kernel, grid=(m // tm, D_MODEL // TN),
        in_specs=[
            pl.BlockSpec((tm, D_MODEL), lambda i, j, l: (i, 0)),
            pl.BlockSpec((tm, r_v), lambda i, j, l: (i, 0)),
            pl.BlockSpec((tm, a_q), lambda i, j, l: (i, 0)),
            pl.BlockSpec((tm, TN), lambda i, j, l: (i, C_GA // TN + j)),
            pl.BlockSpec((tm, TN), lambda i, j, l: (i, C_GB // TN + j)),
            pl.BlockSpec((None, r_v, TN), lambda i, j, l: (l[0], 0, j)),
            pl.BlockSpec((None, a_q, TN), lambda i, j, l: (l[0], 0, j)),
            pl.BlockSpec((None, TN, D_MODEL), lambda i, j, l: (l[0], j, 0)),
        ],
        out_specs=pl.BlockSpec((tm, D_MODEL), lambda i, j, l: (i, 0)),
        out_shape=jax.ShapeDtypeStruct((m, D_MODEL), f32),
        scratch=[pltpu.VMEM((tm, D_MODEL), f32)],
        name="merge",
    )(larr, x, o_r, o_a, z, z, w_pa, w_pb, w_out)


def _rope_tables(pos):
    half = HEAD_DIM // 2
    inv = 1.0 / (ROPE_THETA ** (jnp.arange(half, dtype=f32) / half))
    ang = pos.astype(f32)[:, None] * inv[None, :]
    cos, sin = jnp.cos(ang), jnp.sin(ang)
    return jnp.concatenate([cos, cos], axis=-1), jnp.concatenate([-sin, sin], axis=-1)


def _cmp_to_sel_table(n_c_valid, n_sel, rows, cols):
    cs = np.arange(n_c_valid) * CMP_STRIDE
    ce = cs + CMP_BLOCK - 1
    js = np.arange(n_sel) * SEL_BLOCK
    je = js + SEL_BLOCK - 1
    tab = np.zeros((rows, cols), np.float32)
    tab[:n_c_valid, :n_sel] = (cs[:, None] <= je[None, :]) & (ce[:, None] >= js[None, :])
    return jnp.asarray(tab, dtype=bf16)


def _expand_table(rows, n_keys, first_block=0):
    tab = np.zeros((rows, n_keys), np.float32)
    s = np.arange(n_keys)
    tab[first_block + s // SEL_BLOCK, s] = 1.0
    return tab


def _round_up(a, b):
    return -(-a // b) * b


def kernel(x_prompt, x_sample, cache_kv, state_win, state_ret, page_table, norm_gain, ffn_gate,
           ffn_up, ffn_down, w_in, qk_norm, cmp_w, ret_gn, w_pa, w_pb, w_out):
    nb, t_len, _ = x_prompt.shape
    db, tn, _ = x_sample.shape
    depth, n_pool, page = cache_kv.shape[:3]
    n_pages = page_table.shape[1]
    past = n_pages * page
    wc = state_win.shape[2]
    kvw = N_KV_HEADS * HEAD_DIM
    assert t_len % 512 == 0 and t_len >= WINDOW and wc == WINDOW and tn == 8
    assert n_pages % PAGES_PER_STEP == 0 and past % SEL_BLOCK == 0 and tn <= CMP_STRIDE

    fpad = F_PAD - D_FF
    wg = jnp.pad(ffn_gate.astype(bf16), ((0, 0), (0, 0), (0, 0), (0, fpad)))
    wu = jnp.pad(ffn_up.astype(bf16), ((0, 0), (0, 0), (0, 0), (0, fpad)))
    wd = jnp.pad(ffn_down.astype(bf16), ((0, 0), (0, 0), (0, fpad), (0, 0)))
    w_in_b = w_in.astype(bf16)
    w_in_p = jnp.concatenate(
        [w_in_b[..., :W_IN_SPLIT], jnp.zeros((depth, D_MODEL, C_GA - W_IN_SPLIT), bf16), w_in_b[..., W_IN_SPLIT:]],
        axis=-1)
    w_pa_b, w_pb_b, w_out_b = w_pa.astype(bf16), w_pb.astype(bf16), w_out.astype(bf16)

    gains = norm_gain.reshape(depth, 3, 1, D_MODEL)
    ones = jnp.ones((depth, HEAD_DIM), f32)
    tile_rows = []
    for j in range(N_ZT):
        c0 = j * TN
        if c0 == C_KC:
            tile_rows.append(qk_norm[:, 1])
        elif c0 == C_KS:
            tile_rows.append(qk_norm[:, 2])
        elif c0 == C_KW:
            tile_rows.append(qk_norm[:, 3])
        elif C_AQ <= c0 < C_KC:
            tile_rows.append(qk_norm[:, 0])
        elif C_RK <= c0 < C_RV:
            tile_rows.append(ones * (RET_DK ** -0.5))
        else:
            tile_rows.append(ones)
    tile_gain = jnp.stack(tile_rows, axis=1).reshape(depth, N_ZT, 1, HEAD_DIM)
    modes = jnp.asarray(_TILE_MODE, jnp.int32)

    w_cmp = jnp.repeat(cmp_w, HEAD_DIM, axis=-1)
    w_cmp = w_cmp.transpose(0, 2, 1, 3).reshape(depth, CMP_BLOCK, 2 * kvw)
    w_cmp_rows = w_cmp.reshape(depth, CMP_BLOCK, 2 * N_KV_HEADS, HEAD_DIM)
    gn = ret_gn.reshape(depth, RET_HEADS, 1, RET_DV)

    cos_p, sin_p = _rope_tables(jnp.arange(t_len, dtype=jnp.int32))
    cos_s, sin_s = _rope_tables(jnp.tile(past + jnp.arange(tn, dtype=jnp.int32), db))

    n_c_p = t_len // CMP_STRIDE
    n_sel_p = t_len // SEL_BLOCK
    mcs_p = _cmp_to_sel_table((t_len - CMP_BLOCK) // CMP_STRIDE + 1, n_sel_p, n_c_p, LANE)
    emat_p = jnp.asarray(_expand_table(LANE, t_len), dtype=bf16)
    l_full = past + tn
    n_sel_s = -(-l_full // SEL_BLOCK)
    n_c_s = past // CMP_STRIDE
    sel_w = _round_up(n_sel_s, LANE)
    mcs_s = _cmp_to_sel_table((l_full - CMP_BLOCK) // CMP_STRIDE + 1, n_sel_s, n_c_s, sel_w)
    steps = n_pages // PAGES_PER_STEP
    keys_per_step = PAGES_PER_STEP * page
    blocks_per_step = keys_per_step // SEL_BLOCK
    emat_s = jnp.asarray(_expand_table(LANE, keys_per_step), dtype=bf16)

    cache2 = cache_kv.reshape(depth * n_pool, page, 2, 2 * N_KV_HEADS, HEAD_DIM)
    win2 = state_win.reshape(depth * db, wc, 2 * N_KV_HEADS, HEAD_DIM)
    sret2 = state_ret.reshape(depth * db, RET_HEADS, RET_DK, RET_DV)
    zero_state = jnp.zeros((nb, RET_HEADS, RET_DK, RET_DV), f32)

    mp = nb * t_len
    ms = db * tn
    tm_p = 512
    tm_in = 1024
    tq = 512
    c_p = math.gcd(t_len, RET_CHUNK)

    def layer(carry, l):
        xp, xs = carry
        larr = jnp.reshape(l, (1,)).astype(jnp.int32)

        xp = _ffn(larr, xp, gains, 0, wg, wu, wd, 0, tm_p)
        xs = _ffn(larr, xs, gains, 0, wg, wu, wd, 0, ms)

        zp = _inproj(larr, modes, xp, gains, w_in_p, tile_gain, cos_p, sin_p, min(tm_in, t_len))
        o_r, s_fin = _retention(larr, zp, zero_state, 0, gn, nb, t_len // c_p, c_p, bf16)
        ca, cb = _compress_prompt(larr, zp, w_cmp, 512)
        o_a = _nsa_prompt(larr, zp, ca, cb, mcs_p, emat_p, nb, t_len, tq)
        xp = _merge(larr, xp, o_r, o_a, zp, w_pa_b, w_pb_b, w_out_b, tm_p)
        kv_p = zp[:, C_KC:C_KW].reshape(nb, t_len, 4, N_KV_HEADS, HEAD_DIM)
        win_p = zp[:, C_KW:C_AG].reshape(nb, t_len, 2, N_KV_HEADS, HEAD_DIM)[:, t_len - min(WINDOW, t_len):]

        zs = _inproj(larr, modes, xs, gains, w_in_p, tile_gain, cos_s, sin_s, ms)
        o_rs, s_new = _retention(larr, zs, sret2, db, gn, db, 1, tn, f32)
        sa, sb = _compress_pages(larr, page_table, cache2, w_cmp_rows, n_pool)
        o_c, sel = _nsa_s_cmp(larr, zs, sa, sb, mcs_s, db, tn, past, n_sel_s)
        sel_steps = sel[..., :steps * blocks_per_step].reshape(db, N_KV_HEADS, tn, steps, blocks_per_step)
        sel_steps = jnp.pad(sel_steps.transpose(0, 3, 1, 2, 4), ((0, 0),) * 4 + ((0, LANE - blocks_per_step),))
        acc, mx, lsum = _nsa_s_sel(larr, page_table, zs, sel_steps, emat_s, cache2, n_pool, tn)
        o_as = _nsa_s_fin(larr, zs, win2, acc, mx, lsum, o_c, sel, db, tn, past, n_sel_s)
        xs = _merge(larr, xs, o_rs.astype(bf16), o_as.astype(bf16), zs, w_pa_b, w_pb_b, w_out_b, ms)
        kv_s = zs[:, C_KC:C_KW].reshape(db, tn, 4, N_KV_HEADS, HEAD_DIM)
        win_new = zs[:, C_KW:C_AG].reshape(db, tn, 2, N_KV_HEADS, HEAD_DIM)
        win_old = lax.dynamic_index_in_dim(state_win, l, 0, keepdims=False)
        win_s = jnp.concatenate([win_old[:, tn:], win_new], axis=1)

        xp = _ffn(larr, xp, gains, 2, wg, wu, wd, 1, tm_p)
        xs = _ffn(larr, xs, gains, 2, wg, wu, wd, 1, ms)
        return (xp, xs), (kv_p, kv_s, win_p, win_s, s_fin, s_new)

    (xp, xs), outs = lax.scan(layer, (x_prompt.reshape(mp, D_MODEL), x_sample.reshape(ms, D_MODEL)),
                              jnp.arange(depth, dtype=jnp.int32))
    kv_p, kv_s, win_p, win_s, ret_p, ret_s = outs
    return (xp.reshape(nb, t_len, D_MODEL), xs.reshape(db, tn, D_MODEL), kv_p, kv_s, win_p, win_s, ret_p, ret_s)
```

```python
import functools
import math

import jax
import jax.numpy as jnp
import numpy as np
from jax import lax
from jax.experimental import pallas as pl
from jax.experimental.pallas import tpu as pltpu

D_MODEL = 2048
D_FF = 5504
RET_HEADS = 8
RET_DK = 128
RET_DV = 256
RET_CHUNK = 128
N_HEADS = 16
N_KV_HEADS = 4
HEAD_DIM = 128
GROUP = N_HEADS // N_KV_HEADS
CMP_BLOCK = 32
CMP_STRIDE = 16
SEL_BLOCK = 64
N_SEL = 16
WINDOW = 512
ROPE_THETA = 10000.0
EPS = 1e-6
NEG = -1e30
BIG = 1e30
SCALE = HEAD_DIM ** -0.5
LOG2E = math.log2(math.e)

LANE = 128
VMEM_LIMIT = 56 * 1024 * 1024

TN = 512
C_RQ, C_RK, C_RV, C_RG = 0, 1024, 2048, 4096
C_AQ = 6144
C_KC, C_VC, C_KS, C_VS, C_KW, C_VW = 8192, 8704, 9216, 9728, 10240, 10752
C_AG = 11264
C_GA, C_GB = 11776, 13824
DZ = 15872
N_ZT = DZ // TN
W_IN_SPLIT = 11312
_TILE_MODE = [1, 1, 1, 1] + [0] * 8 + [2, 2, 2, 2] + [2, 0, 2, 0, 2, 0] + [0] * 9

TF = 512
F_TILES = -(-D_FF // TF)
PAGES_PER_STEP = 8

_RET_LOG_G = [float(np.log(np.float32(1.0) - np.float32(2.0) ** np.float32(-5.0 - h))) for h in range(RET_HEADS)]

_NT = (((1,), (1,)), ((), ()))
_TN = (((0,), (0,)), ((), ()))

bf16 = jnp.bfloat16
f32 = jnp.float32


def _sigmoid(x):
    return 1.0 / (1.0 + jnp.exp(-x))


def _call(kernel, *, grid, in_specs, out_specs, out_shape, scratch=(), nsp=1, sem=None, name=None, aliases=None):
    return pl.pallas_call(
        kernel,
        grid_spec=pltpu.PrefetchScalarGridSpec(num_scalar_prefetch=nsp, grid=grid, in_specs=in_specs,
                                               out_specs=out_specs, scratch_shapes=list(scratch)),
        out_shape=out_shape,
        compiler_params=pltpu.CompilerParams(dimension_semantics=sem or ("arbitrary",) * len(grid),
                                             vmem_limit_bytes=VMEM_LIMIT),
        input_output_aliases=aliases or {},
        name=name,
    )


def _ffn_kernel(l_ref, x_ref, g_ref, wg_ref, wu_ref, wd_ref, o_ref, h_ref, acc_ref):
    f = pl.program_id(1)

    @pl.when(f == 0)
    def _():
        x = x_ref[...]
        ms = jnp.mean(x * x, axis=-1, keepdims=True)
        h_ref[...] = (x * lax.rsqrt(ms + EPS) * g_ref[...]).astype(bf16)
        acc_ref[...] = jnp.zeros_like(acc_ref)

    def accumulate(width):
        h = h_ref[...]
        a = jnp.dot(h, wg_ref[:, 0:width], preferred_element_type=f32)
        b = jnp.dot(h, wu_ref[:, 0:width], preferred_element_type=f32)
        s = (a * _sigmoid(a)) * b
        acc_ref[...] += jnp.dot(s.astype(bf16), wd_ref[0:width, :], preferred_element_type=f32)

    last = pl.num_programs(1) - 1

    @pl.when(f < last)
    def _():
        accumulate(TF)

    @pl.when(f == last)
    def _():
        accumulate(D_FF - (F_TILES - 1) * TF)
        o_ref[...] = x_ref[...] + 0.5 * acc_ref[...]


def _ffn(larr, x, gains, which_gain, wg, wu, wd, which_w, tm):
    m = x.shape[0]
    grid = (m // tm, F_TILES)
    return _call(
        _ffn_kernel, grid=grid,
        in_specs=[
            pl.BlockSpec((tm, D_MODEL), lambda i, f, l: (i, 0)),
            pl.BlockSpec((None, None, 1, D_MODEL), lambda i, f, l: (l[0], which_gain, 0, 0)),
            pl.BlockSpec((None, None, D_MODEL, TF), lambda i, f, l: (l[0], which_w, 0, f)),
            pl.BlockSpec((None, None, D_MODEL, TF), lambda i, f, l: (l[0], which_w, 0, f)),
            pl.BlockSpec((None, None, TF, D_MODEL), lambda i, f, l: (l[0], which_w, f, 0)),
        ],
        out_specs=pl.BlockSpec((tm, D_MODEL), lambda i, f, l: (i, 0)),
        out_shape=jax.ShapeDtypeStruct((m, D_MODEL), f32),
        scratch=[pltpu.VMEM((tm, D_MODEL), bf16), pltpu.VMEM((tm, D_MODEL), f32)],
        name="ffn",
    )(larr, x, gains, wg, wu, wd)


def _inproj_kernel(l_ref, mode_ref, x_ref, g_ref, w_ref, gain_ref, cos_ref, sin_ref, o_ref, h_ref):
    j = pl.program_id(1)

    @pl.when(j == 0)
    def _():
        x = x_ref[...]
        ms = jnp.mean(x * x, axis=-1, keepdims=True)
        h_ref[...] = (x * lax.rsqrt(ms + EPS) * g_ref[...]).astype(bf16)

    acc = jnp.dot(h_ref[...], w_ref[...], preferred_element_type=f32)
    mode = mode_ref[j]

    @pl.when(mode == 0)
    def _():
        o_ref[...] = acc

    @pl.when(mode != 0)
    def _():
        cos = cos_ref[...]
        sin = sin_ref[...]
        gain = gain_ref[...]
        for hd in range(TN // HEAD_DIM):
            y = acc[:, hd * HEAD_DIM:(hd + 1) * HEAD_DIM]
            ms = jnp.mean(y * y, axis=-1, keepdims=True)
            inv = jnp.where(mode == 2, lax.rsqrt(ms + EPS), 1.0)
            y = y * inv * gain
            o_ref[:, hd * HEAD_DIM:(hd + 1) * HEAD_DIM] = y * cos + pltpu.roll(y, HEAD_DIM // 2, 1) * sin


def _inproj(larr, modes, x, gains, w_in, tile_gain, cos2, sin2, tm):
    m = x.shape[0]
    nt = cos2.shape[0] // tm
    grid = (m // tm, N_ZT)
    return _call(
        _inproj_kernel, grid=grid, nsp=2,
        in_specs=[
            pl.BlockSpec((tm, D_MODEL), lambda i, j, l, md: (i, 0)),
            pl.BlockSpec((None, None, 1, D_MODEL), lambda i, j, l, md: (l[0], 1, 0, 0)),
            pl.BlockSpec((None, D_MODEL, TN), lambda i, j, l, md: (l[0], 0, j)),
            pl.BlockSpec((None, None, 1, HEAD_DIM), lambda i, j, l, md: (l[0], j, 0, 0)),
            pl.BlockSpec((tm, HEAD_DIM), lambda i, j, l, md: (i % nt, 0)),
            pl.BlockSpec((tm, HEAD_DIM), lambda i, j, l, md: (i % nt, 0)),
        ],
        out_specs=pl.BlockSpec((tm, TN), lambda i, j, l, md: (i, j)),
        out_shape=jax.ShapeDtypeStruct((m, DZ), f32),
        scratch=[pltpu.VMEM((tm, D_MODEL), bf16)],
        name="inproj",
    )(larr, modes, x, gains, w_in, tile_gain, cos2, sin2)


def _kv_rows_kernel(l_ref, z_ref, buf_ref, o_ref):
    del buf_ref
    rows = z_ref.shape[0]
    n = z_ref.shape[1] // LANE
    for c in range(n):
        o_ref[pl.ds(c, rows, stride=n), :] = z_ref[:, c * LANE:(c + 1) * LANE]


def _kv_rows(larr, z, buf, rows):
    m = z.shape[0]
    w = 4 * N_KV_HEADS * HEAD_DIM
    n = w // LANE
    return _call(
        _kv_rows_kernel, grid=(m // rows,),
        in_specs=[pl.BlockSpec((rows, w), lambda i, l: (i, C_KC // w)),
                  pl.BlockSpec(memory_space=pl.ANY)],
        out_specs=pl.BlockSpec((None, rows * n, LANE), lambda i, l: (l[0], i, 0)),
        out_shape=jax.ShapeDtypeStruct(buf.shape, buf.dtype),
        aliases={2: 0}, name="kv_rows",
    )(larr, z, buf)


def _ret_kernel(l_ref, q_ref, k_ref, v_ref, rg_ref, s0_ref, gn_ref, o_ref, sout_ref, s_scr, *, c_true, c_pad):
    c = pl.program_id(1)

    @pl.when(c == 0)
    def _():
        s_scr[...] = s0_ref[...]

    ri = lax.broadcasted_iota(jnp.int32, (c_pad, c_pad), 0)
    ci = lax.broadcasted_iota(jnp.int32, (c_pad, c_pad), 1)
    diff = ri - ci
    row = lax.broadcasted_iota(jnp.int32, (c_pad, 1), 0)

    def padded(a):
        if c_pad == c_true:
            return a
        return jnp.concatenate([a, jnp.zeros((c_pad - c_true, a.shape[1]), a.dtype)], axis=0)

    for h in range(RET_HEADS):
        lg = _RET_LOG_G[h]
        dmat = jnp.where(diff >= 0, jnp.exp(jnp.maximum(diff, 0).astype(f32) * lg), 0.0)
        q = padded(q_ref[:, h * RET_DK:(h + 1) * RET_DK])
        k = padded(k_ref[:, h * RET_DK:(h + 1) * RET_DK])
        v = padded(v_ref[:, h * RET_DV:(h + 1) * RET_DV])
        qb, kb, vb = q.astype(bf16), k.astype(bf16), v.astype(bf16)
        inner = lax.dot_general(qb, kb, _NT, preferred_element_type=f32) * dmat
        xi = jnp.exp((row + 1).astype(f32) * lg)
        state = s_scr[h]
        o = (jnp.dot(inner.astype(bf16), vb, preferred_element_type=f32)
             + jnp.dot(qb, state.astype(bf16), preferred_element_type=f32) * xi)
        wk = jnp.exp((c_true - 1 - row).astype(f32) * lg)
        kw = (k * wk).astype(bf16)
        s_scr[h] = state * float(np.exp(np.float32(c_true * lg))) + lax.dot_general(
            kw, vb, _TN, preferred_element_type=f32)
        o = o[:c_true]
        mu = jnp.mean(o, axis=-1, keepdims=True)
        cen = o - mu
        var = jnp.mean(cen * cen, axis=-1, keepdims=True)
        y = cen * lax.rsqrt(var + EPS) * gn_ref[h]
        rg = rg_ref[:, h * RET_DV:(h + 1) * RET_DV]
        o_ref[:, h * RET_DV:(h + 1) * RET_DV] = (y * (rg * _sigmoid(rg))).astype(o_ref.dtype)

    @pl.when(c == pl.num_programs(1) - 1)
    def _():
        sout_ref[...] = s_scr[...]


def _retention(larr, z, s0, s0_per_layer, ret_gn, nb, nc, c_true, out_dtype):
    c_pad = max(c_true, RET_CHUNK)
    m = z.shape[0]
    r_qk = RET_HEADS * RET_DK
    r_v = RET_HEADS * RET_DV
    kern = functools.partial(_ret_kernel, c_true=c_true, c_pad=c_pad)
    return _call(
        kern, grid=(nb, nc),
        in_specs=[
            pl.BlockSpec((c_true, r_qk), lambda b, c, l: (b * nc + c, C_RQ // r_qk)),
            pl.BlockSpec((c_true, r_qk), lambda b, c, l: (b * nc + c, C_RK // r_qk)),
            pl.BlockSpec((c_true, r_v), lambda b, c, l: (b * nc + c, C_RV // r_v)),
            pl.BlockSpec((c_true, r_v), lambda b, c, l: (b * nc + c, C_RG // r_v)),
            pl.BlockSpec((None, RET_HEADS, RET_DK, RET_DV), lambda b, c, l: (l[0] * s0_per_layer + b, 0, 0, 0)),
            pl.BlockSpec((None, RET_HEADS, 1, RET_DV), lambda b, c, l: (l[0], 0, 0, 0)),
        ],
        out_specs=[
            pl.BlockSpec((c_true, r_v), lambda b, c, l: (b * nc + c, 0)),
            pl.BlockSpec((None, RET_HEADS, RET_DK, RET_DV), lambda b, c, l: (b, 0, 0, 0)),
        ],
        out_shape=[jax.ShapeDtypeStruct((m, r_v), out_dtype),
                   jax.ShapeDtypeStruct((nb, RET_HEADS, RET_DK, RET_DV), f32)],
        scratch=[pltpu.VMEM((RET_HEADS, RET_DK, RET_DV), f32)],
        name="retention",
    )(larr, z, z, z, z, s0, ret_gn)


def _cmp_kernel(*refs, n_in, n_scalar):
    x_refs = refs[n_scalar:n_scalar + n_in]
    w_ref, a_ref, b_ref = refs[n_scalar + n_in:]
    wa = w_ref[0:CMP_STRIDE]
    wb = w_ref[CMP_STRIDE:CMP_BLOCK]
    for k in range(n_in):
        x = x_refs[k][...]
        r = x.shape[0] // CMP_STRIDE
        x3 = x.reshape((r, CMP_STRIDE) + x.shape[1:])
        a_ref[k * r:(k + 1) * r] = jnp.sum(x3 * wa[None], axis=1)
        b_ref[k * r:(k + 1) * r] = jnp.sum(x3 * wb[None], axis=1)


def _compress_prompt(larr, z, w_cmp, rows):
    m = z.shape[0]
    wcols = 2 * N_KV_HEADS * HEAD_DIM
    kern = functools.partial(_cmp_kernel, n_in=1, n_scalar=1)
    shp = jax.ShapeDtypeStruct((m // CMP_STRIDE, wcols), f32)
    return _call(
        kern, grid=(m // rows,),
        in_specs=[pl.BlockSpec((rows, wcols), lambda i, l: (i, C_KC // wcols)),
                  pl.BlockSpec((None, CMP_BLOCK, wcols), lambda i, l: (l[0], 0, 0))],
        out_specs=[pl.BlockSpec((rows // CMP_STRIDE, wcols), lambda i, l: (i, 0))] * 2,
        out_shape=[shp, shp], name="compress_prompt",
    )(larr, z, w_cmp)


def _compress_pages(larr, page_table, cache, w_cmp, n_pool):
    db, n_pages = page_table.shape
    page, _, sg, d = cache.shape[1:]
    pps = PAGES_PER_STEP
    steps = n_pages // pps
    sub = page // CMP_STRIDE
    kern = functools.partial(_cmp_kernel, n_in=pps, n_scalar=2)

    def page_spec(k):
        return pl.BlockSpec((None, page, None, sg, d),
                            lambda b, n, l, pt: (l[0] * n_pool + pt[b, n * pps + k], 0, 0, 0, 0))

    shp = jax.ShapeDtypeStruct((db * n_pages * sub, sg, d), f32)
    return _call(
        kern, grid=(db, steps), nsp=2,
        in_specs=[page_spec(k) for k in range(pps)]
        + [pl.BlockSpec((None, CMP_BLOCK, sg, d), lambda b, n, l, pt: (l[0], 0, 0, 0))],
        out_specs=[pl.BlockSpec((pps * sub, sg, d), lambda b, n, l, pt: (b * steps + n, 0, 0))] * 2,
        out_shape=[shp, shp], name="compress_pages",
    )(larr, page_table, *([cache] * pps), w_cmp)


def _combine_cmp(a, b):
    n = a.shape[0]
    row = lax.broadcasted_iota(jnp.int32, (n, 1), 0)
    return a + jnp.where(row == n - 1, 0.0, pltpu.roll(b, n - 1, 0))


def _masked_exp(s, mask, exp=jnp.exp):
    s = jnp.where(mask, s, NEG)
    m = jnp.max(s, axis=-1, keepdims=True)
    e = jnp.where(mask, exp(s - m), 0.0)
    return e, jnp.sum(e, axis=-1, keepdims=True)


def _dot_hilo(p, m01):
    hi = p.astype(bf16)
    lo = (p - hi.astype(f32)).astype(bf16)
    return jnp.dot(hi, m01, preferred_element_type=f32) + jnp.dot(lo, m01, preferred_element_type=f32)


def _topk_mask(imp, k):
    lane = lax.broadcasted_iota(jnp.int32, imp.shape, 1).astype(f32)
    sel = jnp.zeros(imp.shape, f32)
    for _ in range(k):
        m = jnp.max(imp, axis=-1, keepdims=True)
        idx = jnp.min(jnp.where(imp == m, lane, 1e9), axis=-1, keepdims=True)
        hit = lane == idx
        sel = jnp.where(hit, 1.0, sel)
        imp = jnp.where(hit, -jnp.inf, imp)
    return sel


def _block_importance(p_sum, mcs, tpos, n_sel):
    imp = _dot_hilo(p_sum, mcs)
    jj = lax.broadcasted_iota(jnp.int32, (1, imp.shape[1]), 1)
    forced = jnp.where(jj == 0, 1, jnp.where(jj == (tpos >> 6), 1, 0))
    imp = jnp.where(jj * SEL_BLOCK <= tpos, imp, NEG)
    imp = jnp.where(forced == 1, BIG, imp)
    return jnp.where(jj < n_sel, imp, -jnp.inf)


def _tile_row(ref, j):
    n, s, d = ref.shape
    return ref.reshape(n * s, d)[pl.ds(j, n, stride=s), :]


def _gate(sg, idx):
    lane = lax.broadcasted_iota(jnp.int32, (1, sg.shape[1]), 1)
    return jnp.sum(jnp.where(lane == idx, sg, 0.0), axis=-1, keepdims=True)


def _softmax_pv(q, k, v, mask):
    s = jnp.where(mask, lax.dot_general(q, k, _NT, preferred_element_type=f32), NEG)
    e = jnp.exp2(s - jnp.max(s, axis=-1, keepdims=True))
    den = jnp.sum(e, axis=-1, keepdims=True)
    return jnp.dot(e.astype(bf16), v, preferred_element_type=f32) / den


def _nsa_prompt_kernel(l_ref, q_ref, ka_ref, kb_ref, va_ref, vb_ref, ks_ref, vs_ref, kw_ref, vw_ref, ag_ref,
                       mcs_ref, e_ref, o_ref, os_scr, *, tq, t_len, wq, wlen):
    g = pl.program_id(1)
    i = pl.program_id(2)
    t0 = i * tq
    tpos = t0 + lax.broadcasted_iota(jnp.int32, (tq, 1), 0)
    n_c = t_len // CMP_STRIDE
    n_sel = t_len // SEL_BLOCK

    kc = _combine_cmp(ka_ref[...], kb_ref[...]).astype(bf16)
    vc = _combine_cmp(va_ref[...], vb_ref[...]).astype(bf16)
    cend = lax.broadcasted_iota(jnp.int32, (1, n_c), 1) * CMP_STRIDE + (CMP_BLOCK - 1)
    mask_c = cend <= tpos

    qs = [(q_ref[:, r * HEAD_DIM:(r + 1) * HEAD_DIM] * (SCALE * LOG2E)).astype(bf16) for r in range(GROUP)]

    p_sum = jnp.zeros((tq, n_c), f32)
    o_c = []
    for r in range(GROUP):
        s = lax.dot_general(qs[r], kc, _NT, preferred_element_type=f32)
        e, den = _masked_exp(s, mask_c, jnp.exp2)
        p = e / jnp.maximum(den, 1e-30)
        p_sum = p_sum + p
        o_c.append(jnp.dot(p.astype(bf16), vc, preferred_element_type=f32))

    imp = _block_importance(p_sum, mcs_ref[...], tpos, n_sel)
    selb = _topk_mask(imp, min(N_SEL, n_sel)).astype(bf16)

    for br in range(t_len // tq):
        @pl.when(i == br)
        def _(br=br):
            klen = (br + 1) * tq
            selk = jnp.dot(selb, e_ref[:, 0:klen], preferred_element_type=f32)
            kpos = lax.broadcasted_iota(jnp.int32, (1, klen), 1)
            mask_s = jnp.where(kpos <= tpos, selk, 0.0) > 0.5
            ks = ks_ref[0:klen, :].astype(bf16)
            vs = vs_ref[0:klen, :].astype(bf16)
            for r in range(GROUP):
                os_scr[:, r * HEAD_DIM:(r + 1) * HEAD_DIM] = _softmax_pv(qs[r], ks, vs, mask_s)

    sg = _sigmoid(ag_ref[...])
    gates = [[_gate(sg, (g * GROUP + r) * 3 + k) for k in range(3)] for r in range(GROUP)]
    for h in range(tq // wq):
        rs = slice(h * wq, (h + 1) * wq)
        start = pl.multiple_of(jnp.maximum(t0 + (h + 1) * wq - wlen, 0), wq)
        kw = kw_ref[pl.ds(start, wlen), :].astype(bf16)
        vw = vw_ref[pl.ds(start, wlen), :].astype(bf16)
        d = tpos[rs] - (start + lax.broadcasted_iota(jnp.int32, (1, wlen), 1))
        mask_w = jnp.where(d >= 0, d, WINDOW) < WINDOW
        for r in range(GROUP):
            cs = slice(r * HEAD_DIM, (r + 1) * HEAD_DIM)
            o_w = _softmax_pv(qs[r][rs], kw, vw, mask_w)
            out = gates[r][0][rs] * o_c[r][rs] + gates[r][1][rs] * os_scr[rs, cs] + gates[r][2][rs] * o_w
            o_ref[rs, cs] = out.astype(o_ref.dtype)


def _nsa_prompt(larr, z, cmp_a, cmp_b, mcs, emat, nb, t_len, tq):
    m = z.shape[0]
    nq = t_len // tq
    n_c = t_len // CMP_STRIDE
    wq = min(tq, 256)
    wlen = min(WINDOW + wq, t_len)
    gw = GROUP * HEAD_DIM
    kern = functools.partial(_nsa_prompt_kernel, tq=tq, t_len=t_len, wq=wq, wlen=wlen)

    def head_cols(c0):
        return pl.BlockSpec((t_len, HEAD_DIM), lambda b, g, i, l: (b, c0 // HEAD_DIM + g))

    def cmp_spec(off):
        return pl.BlockSpec((n_c, HEAD_DIM), lambda b, g, i, l: (b, off + g))

    return _call(
        kern, grid=(nb, N_KV_HEADS, nq),
        in_specs=[
            pl.BlockSpec((tq, gw), lambda b, g, i, l: (b * nq + i, C_AQ // gw + g)),
            cmp_spec(0), cmp_spec(0), cmp_spec(N_KV_HEADS), cmp_spec(N_KV_HEADS),
            head_cols(C_KS), head_cols(C_VS), head_cols(C_KW), head_cols(C_VW),
            pl.BlockSpec((tq, LANE), lambda b, g, i, l: (b * nq + i, C_AG // LANE)),
            pl.BlockSpec(mcs.shape, lambda b, g, i, l: (0, 0)),
            pl.BlockSpec(emat.shape, lambda b, g, i, l: (0, 0)),
        ],
        out_specs=pl.BlockSpec((tq, gw), lambda b, g, i, l: (b * nq + i, g)),
        out_shape=jax.ShapeDtypeStruct((m, N_HEADS * HEAD_DIM), bf16),
        scratch=[pltpu.VMEM((tq, gw), f32)],
        name="nsa_prompt",
    )(larr, z, cmp_a, cmp_b, cmp_a, cmp_b, z, z, z, z, z, mcs, emat)


def _rows_rt(ref, g_off, tn):
    return jnp.concatenate(
        [ref[:, (g_off + r) * HEAD_DIM:(g_off + r + 1) * HEAD_DIM] for r in range(GROUP)], axis=0)


def _nsa_s_cmp_kernel(l_ref, q_ref, a_ref, b_ref, mcs_ref, oc_ref, sel_ref, *, tn, past, n_sel):
    n_c = a_ref.shape[0]
    rows = GROUP * tn
    tpos_r = past + (lax.broadcasted_iota(jnp.int32, (rows, 1), 0) % tn)
    tpos = past + (lax.broadcasted_iota(jnp.int32, (2 * tn, 1), 0) % tn)
    cend = lax.broadcasted_iota(jnp.int32, (1, n_c), 1) * CMP_STRIDE + (CMP_BLOCK - 1)
    mask_c = cend <= tpos_r
    mcs = mcs_ref[...]
    for g in range(N_KV_HEADS):
        q = (_rows_rt(q_ref, g * GROUP, tn) * SCALE).astype(bf16)
        kc = _combine_cmp(_tile_row(a_ref, g), _tile_row(b_ref, g)).astype(bf16)
        vc = _combine_cmp(_tile_row(a_ref, N_KV_HEADS + g), _tile_row(b_ref, N_KV_HEADS + g)).astype(bf16)
        s = lax.dot_general(q, kc, _NT, preferred_element_type=f32)
        e, den = _masked_exp(s, mask_c)
        p = e / jnp.maximum(den, 1e-30)
        oc_ref[g] = jnp.dot(p.astype(bf16), vc, preferred_element_type=f32)
        p_sum = p[0:tn]
        for r in range(1, GROUP):
            p_sum = p_sum + p[r * tn:(r + 1) * tn]
        p_sum = jnp.concatenate([p_sum, jnp.zeros_like(p_sum)], axis=0)
        imp = _block_importance(p_sum, mcs, tpos, n_sel)
        sel_ref[g] = _topk_mask(imp, min(N_SEL, n_sel))[0:tn]


def _nsa_s_cmp(larr, z, cmp_a, cmp_b, mcs, db, tn, past, n_sel):
    n_c = cmp_a.shape[0] // db
    aq = N_HEADS * HEAD_DIM
    kern = functools.partial(_nsa_s_cmp_kernel, tn=tn, past=past, n_sel=n_sel)
    cmp_spec = pl.BlockSpec((n_c,) + cmp_a.shape[1:], lambda b, l: (b, 0, 0))
    return _call(
        kern, grid=(db,),
        in_specs=[pl.BlockSpec((tn, aq), lambda b, l: (b, C_AQ // aq)), cmp_spec, cmp_spec,
                  pl.BlockSpec(mcs.shape, lambda b, l: (0, 0))],
        out_specs=[pl.BlockSpec((None, N_KV_HEADS, GROUP * tn, HEAD_DIM), lambda b, l: (b, 0, 0, 0)),
                   pl.BlockSpec((None, N_KV_HEADS, tn, mcs.shape[1]), lambda b, l: (b, 0, 0, 0))],
        out_shape=[jax.ShapeDtypeStruct((db, N_KV_HEADS, GROUP * tn, HEAD_DIM), f32),
                   jax.ShapeDtypeStruct((db, N_KV_HEADS, tn, mcs.shape[1]), f32)],
        name="nsa_sample_cmp",
    )(larr, z, cmp_a, cmp_b, mcs)


def _nsa_s_sel_kernel(*refs, tn, pps):
    q_ref, sel_ref, e_ref = refs[2:5]
    page_refs = refs[5:5 + pps]
    acc_ref, m_ref, l_ref = refs[5 + pps:]
    n = pl.program_id(1)

    @pl.when(n == 0)
    def _():
        acc_ref[...] = jnp.zeros_like(acc_ref)
        m_ref[...] = jnp.full_like(m_ref, NEG)
        l_ref[...] = jnp.zeros_like(l_ref)

    emat = e_ref[...]
    for g in range(N_KV_HEADS):
        q = (_rows_rt(q_ref, g * GROUP, tn) * SCALE).astype(bf16)
        s = jnp.concatenate(
            [lax.dot_general(q, _tile_row(pr, g).astype(bf16), _NT, preferred_element_type=f32)
             for pr in page_refs], axis=1)
        selg = sel_ref[g]
        selg = jnp.concatenate([selg, jnp.zeros_like(selg)], axis=0).astype(bf16)
        selk = jnp.dot(selg, emat, preferred_element_type=f32)[0:tn]
        mask = jnp.concatenate([selk] * GROUP, axis=0) > 0.5
        s = jnp.where(mask, s, NEG)
        m_old = m_ref[g]
        m_new = jnp.maximum(m_old, jnp.max(s, axis=-1, keepdims=True))
        alpha = jnp.exp(m_old - m_new)
        e = jnp.where(mask, jnp.exp(s - m_new[:, 0:1]), 0.0)
        l_ref[g] = alpha * l_ref[g] + jnp.sum(e, axis=-1, keepdims=True)
        pv = None
        for k, pr in enumerate(page_refs):
            page = pr.shape[0]
            ek = e[:, k * page:(k + 1) * page].astype(bf16)
            vk = _tile_row(pr, N_KV_HEADS + g).astype(bf16)
            d = jnp.dot(ek, vk, preferred_element_type=f32)
            pv = d if pv is None else pv + d
        acc_ref[g] = alpha * acc_ref[g] + pv
        m_ref[g] = m_new


def _nsa_s_sel(larr, page_table, z, sel, emat_pages, cache, n_pool, tn):
    db, n_pages = page_table.shape
    page, _, sg, d = cache.shape[1:]
    pps = PAGES_PER_STEP
    steps = n_pages // pps
    rows = GROUP * tn
    kern = functools.partial(_nsa_s_sel_kernel, tn=tn, pps=pps)

    def page_spec(k):
        return pl.BlockSpec((None, page, None, sg, d),
                            lambda b, n, l, pt: (l[0] * n_pool + pt[b, n * pps + k], 0, 1, 0, 0))

    st = jax.ShapeDtypeStruct((db, N_KV_HEADS, rows, HEAD_DIM), f32)
    st_spec = pl.BlockSpec((None, N_KV_HEADS, rows, HEAD_DIM), lambda b, n, l, pt: (b, 0, 0, 0))
    return _call(
        kern, grid=(db, steps), nsp=2,
        in_specs=[pl.BlockSpec((tn, N_HEADS * HEAD_DIM), lambda b, n, l, pt: (b, C_AQ // (N_HEADS * HEAD_DIM))),
                  pl.BlockSpec((None, None, N_KV_HEADS, tn, LANE), lambda b, n, l, pt: (b, n, 0, 0, 0)),
                  pl.BlockSpec(emat_pages.shape, lambda b, n, l, pt: (0, 0))]
        + [page_spec(k) for k in range(pps)],
        out_specs=[st_spec, st_spec, st_spec],
        out_shape=[st, st, st], name="nsa_sample_sel",
    )(larr, page_table, z, sel, emat_pages, *([cache] * pps))


def _nsa_s_fin_kernel(l_ref, q_ref, ksn_ref, vsn_ref, kwn_ref, vwn_ref, ag_ref, buf_ref,
                      acc_ref, m_ref, lsum_ref, oc_ref, sel_ref, o_ref, *, tn, past, n_sel):
    rows = GROUP * tn
    wc = buf_ref.shape[0]
    tpos = past + (lax.broadcasted_iota(jnp.int32, (rows, 1), 0) % tn)
    kidx = lax.broadcasted_iota(jnp.int32, (1, LANE), 1)
    widx = lax.broadcasted_iota(jnp.int32, (1, wc + LANE), 1)
    d = tpos - (past - wc + widx)
    mask_w = jnp.where(widx < wc + tn, jnp.where(d >= 0, d, WINDOW), WINDOW) < WINDOW
    sg = _sigmoid(ag_ref[...])

    def new_rows(ref, g):
        a = ref[:, g * HEAD_DIM:(g + 1) * HEAD_DIM]
        return jnp.concatenate([a, jnp.zeros((LANE - tn, HEAD_DIM), a.dtype)], axis=0)

    for g in range(N_KV_HEADS):
        q = (_rows_rt(q_ref, g * GROUP, tn) * SCALE).astype(bf16)

        ksn = new_rows(ksn_ref, g).astype(bf16)
        vsn = new_rows(vsn_ref, g).astype(bf16)
        s = lax.dot_general(q, ksn, _NT, preferred_element_type=f32)
        last_sel = sel_ref[g][:, n_sel - 1:n_sel]
        last_sel = jnp.concatenate([last_sel] * GROUP, axis=0)
        mask = jnp.where(kidx < tn, jnp.where(past + kidx <= tpos, last_sel, 0.0), 0.0) > 0.5
        s = jnp.where(mask, s, NEG)
        m_old = m_ref[g]
        m_new = jnp.maximum(m_old, jnp.max(s, axis=-1, keepdims=True))
        alpha = jnp.exp(m_old - m_new)
        e = jnp.where(mask, jnp.exp(s - m_new[:, 0:1]), 0.0)
        den = alpha * lsum_ref[g] + jnp.sum(e, axis=-1, keepdims=True)
        o_s = (alpha * acc_ref[g] + jnp.dot(e.astype(bf16), vsn, preferred_element_type=f32)) / jnp.maximum(den, 1e-30)

        kw = jnp.concatenate([_tile_row(buf_ref, g), new_rows(kwn_ref, g)], axis=0).astype(bf16)
        vw = jnp.concatenate([_tile_row(buf_ref, N_KV_HEADS + g), new_rows(vwn_ref, g)], axis=0).astype(bf16)
        s = lax.dot_general(q, kw, _NT, preferred_element_type=f32)
        e, den = _masked_exp(s, mask_w)
        o_w = jnp.dot(e.astype(bf16), vw, preferred_element_type=f32) / jnp.maximum(den, 1e-30)

        o_c = oc_ref[g]
        for r in range(GROUP):
            base = (g * GROUP + r) * 3
            sl = slice(r * tn, (r + 1) * tn)
            out = _gate(sg, base) * o_c[sl] + _gate(sg, base + 1) * o_s[sl] + _gate(sg, base + 2) * o_w[sl]
            o_ref[:, (g * GROUP + r) * HEAD_DIM:(g * GROUP + r + 1) * HEAD_DIM] = out


def _nsa_s_fin(larr, z, win_buf, acc, mx, lsum, o_c, sel, db, tn, past, n_sel):
    aq = N_HEADS * HEAD_DIM
    kvw = N_KV_HEADS * HEAD_DIM
    rows = GROUP * tn
    kern = functools.partial(_nsa_s_fin_kernel, tn=tn, past=past, n_sel=n_sel)

    def new_cols(c0):
        return pl.BlockSpec((tn, kvw), lambda b, l: (b, c0 // kvw))

    st_spec = pl.BlockSpec((None, N_KV_HEADS, rows, HEAD_DIM), lambda b, l: (b, 0, 0, 0))
    return _call(
        kern, grid=(db,),
        in_specs=[pl.BlockSpec((tn, aq), lambda b, l: (b, C_AQ // aq)),
                  new_cols(C_KS), new_cols(C_VS), new_cols(C_KW), new_cols(C_VW),
                  pl.BlockSpec((tn, LANE), lambda b, l: (b, C_AG // LANE)),
                  pl.BlockSpec((None,) + win_buf.shape[1:], lambda b, l: (l[0] * db + b, 0, 0, 0)),
                  st_spec, st_spec, st_spec, st_spec,
                  pl.BlockSpec((None, N_KV_HEADS, tn, sel.shape[-1]), lambda b, l: (b, 0, 0, 0))],
        out_specs=pl.BlockSpec((tn, aq), lambda b, l: (b, 0)),
        out_shape=jax.ShapeDtypeStruct((db * tn, aq), f32),
        name="nsa_sample_fin",
    )(larr, z, z, z, z, z, z, win_buf, acc, mx, lsum, o_c, sel)


def _merge_kernel(l_ref, x_ref, or_ref, oa_ref, ga_ref, gb_ref, wpa_ref, wpb_ref, wo_ref, o_ref, acc_ref):
    j = pl.program_id(1)

    @pl.when(j == 0)
    def _():
        acc_ref[...] = jnp.zeros_like(acc_ref)

    pa = jnp.dot(or_ref[...], wpa_ref[...], preferred_element_type=f32)
    pb = jnp.dot(oa_ref[...], wpb_ref[...], preferred_element_type=f32)
    mix = _sigmoid(ga_ref[...]) * pa + _sigmoid(gb_ref[...]) * pb
    acc_ref[...] += jnp.dot(mix.astype(bf16), wo_ref[...], preferred_element_type=f32)

    @pl.when(j == pl.num_programs(1) - 1)
    def _():
        o_ref[...] = x_ref[...] + acc_ref[...]


def _merge(larr, x, o_r, o_a, z, w_pa, w_pb, w_out, tm):
    m = x.shape[0]
    r_v = RET_HEADS * RET_DV
    a_q = N_HEADS * HEAD_DIM
    return _call(
        _merge_kernel, grid=(m // tm, D_MODEL // TN),
        in_specs=[
            pl.BlockSpec((tm, D_MODEL), lambda i, j, l: (i, 0)),
            pl.BlockSpec((tm, r_v), lambda i, j, l: (i, 0)),
            pl.BlockSpec((tm, a_q), lambda i, j, l: (i, 0)),
            pl.BlockSpec((tm, TN), lambda i, j, l: (i, C_GA // TN + j)),
            pl.BlockSpec((tm, TN), lambda i, j, l: (i, C_GB // TN + j)),
            pl.BlockSpec((None, r_v, TN), lambda i, j, l: (l[0], 0, j)),
            pl.BlockSpec((None, a_q, TN), lambda i, j, l: (l[0], 0, j)),
            pl.BlockSpec((None, TN, D_MODEL), lambda i, j, l: (l[0], j, 0)),
        ],
        out_specs=pl.BlockSpec((tm, D_MODEL), lambda i, j, l: (i, 0)),
        out_shape=jax.ShapeDtypeStruct((m, D_MODEL), f32),
        scratch=[pltpu.VMEM((tm, D_MODEL), f32)],
        name="merge",
    )(larr, x, o_r, o_a, z, z, w_pa, w_pb, w_out)


def _rope_tables(pos):
    half = HEAD_DIM // 2
    inv = 1.0 / (ROPE_THETA ** (jnp.arange(half, dtype=f32) / half))
    ang = pos.astype(f32)[:, None] * inv[None, :]
    cos, sin = jnp.cos(ang), jnp.sin(ang)
    return jnp.concatenate([cos, cos], axis=-1), jnp.concatenate([-sin, sin], axis=-1)


def _cmp_to_sel_table(n_c_valid, n_sel, rows, cols):
    cs = np.arange(n_c_valid) * CMP_STRIDE
    ce = cs + CMP_BLOCK - 1
    js = np.arange(n_sel) * SEL_BLOCK
    je = js + SEL_BLOCK - 1
    tab = np.zeros((rows, cols), np.float32)
    tab[:n_c_valid, :n_sel] = (cs[:, None] <= je[None, :]) & (ce[:, None] >= js[None, :])
    return jnp.asarray(tab, dtype=bf16)


def _expand_table(rows, n_keys, first_block=0):
    tab = np.zeros((rows, n_keys), np.float32)
    s = np.arange(n_keys)
    tab[first_block + s // SEL_BLOCK, s] = 1.0
    return tab


def _round_up(a, b):
    return -(-a // b) * b


def kernel(x_prompt, x_sample, cache_kv, state_win, state_ret, page_table, norm_gain, ffn_gate,
           ffn_up, ffn_down, w_in, qk_norm, cmp_w, ret_gn, w_pa, w_pb, w_out):
    nb, t_len, _ = x_prompt.shape
    db, tn, _ = x_sample.shape
    depth, n_pool, page = cache_kv.shape[:3]
    n_pages = page_table.shape[1]
    past = n_pages * page
    wc = state_win.shape[2]
    kvw = N_KV_HEADS * HEAD_DIM
    assert t_len % 512 == 0 and t_len >= WINDOW and wc == WINDOW and tn == 8
    assert n_pages % PAGES_PER_STEP == 0 and past % SEL_BLOCK == 0 and tn <= CMP_STRIDE

    wg, wu, wd = ffn_gate.astype(bf16), ffn_up.astype(bf16), ffn_down.astype(bf16)
    w_in_b = w_in.astype(bf16)
    w_in_p = jnp.concatenate(
        [w_in_b[..., :W_IN_SPLIT], jnp.zeros((depth, D_MODEL, C_GA - W_IN_SPLIT), bf16), w_in_b[..., W_IN_SPLIT:]],
        axis=-1)
    w_pa_b, w_pb_b, w_out_b = w_pa.astype(bf16), w_pb.astype(bf16), w_out.astype(bf16)

    gains = norm_gain.reshape(depth, 3, 1, D_MODEL)
    ones = jnp.ones((depth, HEAD_DIM), f32)
    tile_rows = []
    for j in range(N_ZT):
        c0 = j * TN
        if c0 == C_KC:
            tile_rows.append(qk_norm[:, 1])
        elif c0 == C_KS:
            tile_rows.append(qk_norm[:, 2])
        elif c0 == C_KW:
            tile_rows.append(qk_norm[:, 3])
        elif C_AQ <= c0 < C_KC:
            tile_rows.append(qk_norm[:, 0])
        elif C_RK <= c0 < C_RV:
            tile_rows.append(ones * (RET_DK ** -0.5))
        else:
            tile_rows.append(ones)
    tile_gain = jnp.stack(tile_rows, axis=1).reshape(depth, N_ZT, 1, HEAD_DIM)
    modes = jnp.asarray(_TILE_MODE, jnp.int32)

    w_cmp = jnp.repeat(cmp_w, HEAD_DIM, axis=-1)
    w_cmp = w_cmp.transpose(0, 2, 1, 3).reshape(depth, CMP_BLOCK, 2 * kvw)
    w_cmp_rows = w_cmp.reshape(depth, CMP_BLOCK, 2 * N_KV_HEADS, HEAD_DIM)
    gn = ret_gn.reshape(depth, RET_HEADS, 1, RET_DV)

    cos_p, sin_p = _rope_tables(jnp.arange(t_len, dtype=jnp.int32))
    cos_s, sin_s = _rope_tables(jnp.tile(past + jnp.arange(tn, dtype=jnp.int32), db))

    n_c_p = t_len // CMP_STRIDE
    n_sel_p = t_len // SEL_BLOCK
    mcs_p = _cmp_to_sel_table((t_len - CMP_BLOCK) // CMP_STRIDE + 1, n_sel_p, n_c_p, LANE)
    emat_p = jnp.asarray(_expand_table(LANE, t_len), dtype=bf16)
    l_full = past + tn
    n_sel_s = -(-l_full // SEL_BLOCK)
    n_c_s = past // CMP_STRIDE
    sel_w = _round_up(n_sel_s, LANE)
    mcs_s = _cmp_to_sel_table((l_full - CMP_BLOCK) // CMP_STRIDE + 1, n_sel_s, n_c_s, sel_w)
    steps = n_pages // PAGES_PER_STEP
    keys_per_step = PAGES_PER_STEP * page
    blocks_per_step = keys_per_step // SEL_BLOCK
    emat_s = jnp.asarray(_expand_table(LANE, keys_per_step), dtype=bf16)

    cache2 = cache_kv.reshape(depth * n_pool, page, 2, 2 * N_KV_HEADS, HEAD_DIM)
    win2 = state_win.reshape(depth * db, wc, 2 * N_KV_HEADS, HEAD_DIM)
    sret2 = state_ret.reshape(depth * db, RET_HEADS, RET_DK, RET_DV)
    zero_state = jnp.zeros((nb, RET_HEADS, RET_DK, RET_DV), f32)

    mp = nb * t_len
    ms = db * tn
    tm_p = 512
    tm_in = 1024
    tq = 512
    c_p = math.gcd(t_len, RET_CHUNK)

    def layer(carry, l):
        xp, xs, kv_buf = carry
        larr = jnp.reshape(l, (1,)).astype(jnp.int32)

        xp = _ffn(larr, xp, gains, 0, wg, wu, wd, 0, tm_p)
        xs = _ffn(larr, xs, gains, 0, wg, wu, wd, 0, ms)

        zp = _inproj(larr, modes, xp, gains, w_in_p, tile_gain, cos_p, sin_p, min(tm_in, t_len))
        o_r, s_fin = _retention(larr, zp, zero_state, 0, gn, nb, t_len // c_p, c_p, bf16)
        ca, cb = _compress_prompt(larr, zp, w_cmp, 512)
        o_a = _nsa_prompt(larr, zp, ca, cb, mcs_p, emat_p, nb, t_len, tq)
        xp = _merge(larr, xp, o_r, o_a, zp, w_pa_b, w_pb_b, w_out_b, tm_p)
        kv_buf = _kv_rows(larr, zp, kv_buf, 512)
        win_p = zp[:, C_KW:C_AG].reshape(nb, t_len, 2, N_KV_HEADS, HEAD_DIM)[:, t_len - min(WINDOW, t_len):]

        zs = _inproj(larr, modes, xs, gains, w_in_p, tile_gain, cos_s, sin_s, ms)
        o_rs, s_new = _retention(larr, zs, sret2, db, gn, db, 1, tn, f32)
        sa, sb = _compress_pages(larr, page_table, cache2, w_cmp_rows, n_pool)
        o_c, sel = _nsa_s_cmp(larr, zs, sa, sb, mcs_s, db, tn, past, n_sel_s)
        sel_steps = sel[..., :steps * blocks_per_step].reshape(db, N_KV_HEADS, tn, steps, blocks_per_step)
        sel_steps = jnp.pad(sel_steps.transpose(0, 3, 1, 2, 4), ((0, 0),) * 4 + ((0, LANE - blocks_per_step),))
        acc, mx, lsum = _nsa_s_sel(larr, page_table, zs, sel_steps, emat_s, cache2, n_pool, tn)
        o_as = _nsa_s_fin(larr, zs, win2, acc, mx, lsum, o_c, sel, db, tn, past, n_sel_s)
        xs = _merge(larr, xs, o_rs.astype(bf16), o_as.astype(bf16), zs, w_pa_b, w_pb_b, w_out_b, ms)
        kv_s = zs[:, C_KC:C_KW].reshape(db, tn, 4, N_KV_HEADS, HEAD_DIM)
        win_new = zs[:, C_KW:C_AG].reshape(db, tn, 2, N_KV_HEADS, HEAD_DIM)
        win_old = lax.dynamic_index_in_dim(state_win, l, 0, keepdims=False)
        win_s = jnp.concatenate([win_old[:, tn:], win_new], axis=1)

        xp = _ffn(larr, xp, gains, 2, wg, wu, wd, 1, tm_p)
        xs = _ffn(larr, xs, gains, 2, wg, wu, wd, 1, ms)
        return (xp, xs, kv_buf), (kv_s, win_p, win_s, s_fin, s_new)

    kv_buf0 = jnp.zeros((depth, mp * 4 * kvw // LANE, LANE), f32)
    (xp, xs, kv_buf), outs = lax.scan(
        layer, (x_prompt.reshape(mp, D_MODEL), x_sample.reshape(ms, D_MODEL), kv_buf0),
        jnp.arange(depth, dtype=jnp.int32))
    kv_s, win_p, win_s, ret_p, ret_s = outs
    kv_p = kv_buf.reshape(depth, nb, t_len, 4, N_KV_HEADS, HEAD_DIM)
    return (xp.reshape(nb, t_len, D_MODEL), xs.reshape(db, tn, D_MODEL), kv_p, kv_s, win_p, win_s, ret_p, ret_s)
```

```python
import functools
import math

import jax
import jax.numpy as jnp
import numpy as np
from jax import lax
from jax.experimental import pallas as pl
from jax.experimental.pallas import tpu as pltpu

D_MODEL = 2048
D_FF = 5504
RET_HEADS = 8
RET_DK = 128
RET_DV = 256
RET_CHUNK = 128
N_HEADS = 16
N_KV_HEADS = 4
HEAD_DIM = 128
GROUP = N_HEADS // N_KV_HEADS
CMP_BLOCK = 32
CMP_STRIDE = 16
SEL_BLOCK = 64
N_SEL = 16
WINDOW = 512
ROPE_THETA = 10000.0
EPS = 1e-6
NEG = -1e30
BIG = 1e30
SCALE = HEAD_DIM ** -0.5
LOG2E = math.log2(math.e)

LANE = 128
VMEM_LIMIT = 56 * 1024 * 1024

TN = 512
C_RQ, C_RK, C_RV, C_RG = 0, 1024, 2048, 4096
C_AQ = 6144
C_KC, C_VC, C_KS, C_VS, C_KW, C_VW = 8192, 8704, 9216, 9728, 10240, 10752
C_AG = 11264
C_GA, C_GB = 11776, 13824
DZ = 15872
N_ZT = DZ // TN
W_IN_SPLIT = 11312
_TILE_MODE = [1, 1, 1, 1] + [0] * 8 + [2, 2, 2, 2] + [2, 0, 2, 0, 2, 0] + [0] * 9

TF = 512
F_TILES = -(-D_FF // TF)
PAGES_PER_STEP = 8
CMP_PAGES_PER_STEP = 16

_RET_LOG_G = [float(np.log(np.float32(1.0) - np.float32(2.0) ** np.float32(-5.0 - h))) for h in range(RET_HEADS)]

_NT = (((1,), (1,)), ((), ()))
_TN = (((0,), (0,)), ((), ()))

bf16 = jnp.bfloat16
f32 = jnp.float32


def _sigmoid(x):
    return 1.0 / (1.0 + jnp.exp(-x))


def _call(kernel, *, grid, in_specs, out_specs, out_shape, scratch=(), nsp=1, sem=None, name=None, aliases=None):
    return pl.pallas_call(
        kernel,
        grid_spec=pltpu.PrefetchScalarGridSpec(num_scalar_prefetch=nsp, grid=grid, in_specs=in_specs,
                                               out_specs=out_specs, scratch_shapes=list(scratch)),
        out_shape=out_shape,
        compiler_params=pltpu.CompilerParams(dimension_semantics=sem or ("arbitrary",) * len(grid),
                                             vmem_limit_bytes=VMEM_LIMIT),
        input_output_aliases=aliases or {},
        name=name,
    )


def _ffn_kernel(l_ref, x_ref, g_ref, wg_ref, wu_ref, wd_ref, o_ref, h_ref, acc_ref):
    f = pl.program_id(1)

    @pl.when(f == 0)
    def _():
        x = x_ref[...]
        ms = jnp.mean(x * x, axis=-1, keepdims=True)
        h_ref[...] = (x * lax.rsqrt(ms + EPS) * g_ref[...]).astype(bf16)
        acc_ref[...] = jnp.zeros_like(acc_ref)

    def accumulate(width):
        h = h_ref[...]
        a = jnp.dot(h, wg_ref[:, 0:width], preferred_element_type=f32)
        b = jnp.dot(h, wu_ref[:, 0:width], preferred_element_type=f32)
        s = (a * _sigmoid(a)) * b
        acc_ref[...] += jnp.dot(s.astype(bf16), wd_ref[0:width, :], preferred_element_type=f32)

    last = pl.num_programs(1) - 1

    @pl.when(f < last)
    def _():
        accumulate(TF)

    @pl.when(f == last)
    def _():
        accumulate(D_FF - (F_TILES - 1) * TF)
        o_ref[...] = x_ref[...] + 0.5 * acc_ref[...]


def _ffn(larr, x, gains, which_gain, wg, wu, wd, which_w, tm):
    m = x.shape[0]
    grid = (m // tm, F_TILES)
    return _call(
        _ffn_kernel, grid=grid,
        in_specs=[
            pl.BlockSpec((tm, D_MODEL), lambda i, f, l: (i, 0)),
            pl.BlockSpec((None, None, 1, D_MODEL), lambda i, f, l: (l[0], which_gain, 0, 0)),
            pl.BlockSpec((None, None, D_MODEL, TF), lambda i, f, l: (l[0], which_w, 0, f)),
            pl.BlockSpec((None, None, D_MODEL, TF), lambda i, f, l: (l[0], which_w, 0, f)),
            pl.BlockSpec((None, None, TF, D_MODEL), lambda i, f, l: (l[0], which_w, f, 0)),
        ],
        out_specs=pl.BlockSpec((tm, D_MODEL), lambda i, f, l: (i, 0)),
        out_shape=jax.ShapeDtypeStruct((m, D_MODEL), f32),
        scratch=[pltpu.VMEM((tm, D_MODEL), bf16), pltpu.VMEM((tm, D_MODEL), f32)],
        name="ffn",
    )(larr, x, gains, wg, wu, wd)


def _inproj_kernel(l_ref, mode_ref, x_ref, g_ref, w_ref, gain_ref, cos_ref, sin_ref, o_ref, h_ref):
    j = pl.program_id(1)

    @pl.when(j == 0)
    def _():
        x = x_ref[...]
        ms = jnp.mean(x * x, axis=-1, keepdims=True)
        h_ref[...] = (x * lax.rsqrt(ms + EPS) * g_ref[...]).astype(bf16)

    acc = jnp.dot(h_ref[...], w_ref[...], preferred_element_type=f32)
    mode = mode_ref[j]

    @pl.when(mode == 0)
    def _():
        o_ref[...] = acc

    @pl.when(mode != 0)
    def _():
        cos = cos_ref[...]
        sin = sin_ref[...]
        gain = gain_ref[...]
        for hd in range(TN // HEAD_DIM):
            y = acc[:, hd * HEAD_DIM:(hd + 1) * HEAD_DIM]
            ms = jnp.mean(y * y, axis=-1, keepdims=True)
            inv = jnp.where(mode == 2, lax.rsqrt(ms + EPS), 1.0)
            y = y * inv * gain
            o_ref[:, hd * HEAD_DIM:(hd + 1) * HEAD_DIM] = y * cos + pltpu.roll(y, HEAD_DIM // 2, 1) * sin


def _inproj(larr, modes, x, gains, w_in, tile_gain, cos2, sin2, tm):
    m = x.shape[0]
    nt = cos2.shape[0] // tm
    grid = (m // tm, N_ZT)
    return _call(
        _inproj_kernel, grid=grid, nsp=2,
        in_specs=[
            pl.BlockSpec((tm, D_MODEL), lambda i, j, l, md: (i, 0)),
            pl.BlockSpec((None, None, 1, D_MODEL), lambda i, j, l, md: (l[0], 1, 0, 0)),
            pl.BlockSpec((None, D_MODEL, TN), lambda i, j, l, md: (l[0], 0, j)),
            pl.BlockSpec((None, None, 1, HEAD_DIM), lambda i, j, l, md: (l[0], j, 0, 0)),
            pl.BlockSpec((tm, HEAD_DIM), lambda i, j, l, md: (i % nt, 0)),
            pl.BlockSpec((tm, HEAD_DIM), lambda i, j, l, md: (i % nt, 0)),
        ],
        out_specs=pl.BlockSpec((tm, TN), lambda i, j, l, md: (i, j)),
        out_shape=jax.ShapeDtypeStruct((m, DZ), f32),
        scratch=[pltpu.VMEM((tm, D_MODEL), bf16)],
        name="inproj",
    )(larr, modes, x, gains, w_in, tile_gain, cos2, sin2)


def _kv_rows_kernel(l_ref, z_ref, buf_ref, o_ref):
    del buf_ref
    rows = z_ref.shape[0]
    n = z_ref.shape[1] // LANE
    for c in range(n):
        o_ref[pl.ds(c, rows, stride=n), :] = z_ref[:, c * LANE:(c + 1) * LANE]


def _kv_rows(larr, z, buf, rows, col0, width, n_blocks, in_block):
    n = width // LANE
    return _call(
        _kv_rows_kernel, grid=(n_blocks,),
        in_specs=[pl.BlockSpec((rows, width), lambda i, l: (in_block(i), col0 // width)),
                  pl.BlockSpec(memory_space=pl.ANY)],
        out_specs=pl.BlockSpec((None, rows * n, LANE), lambda i, l: (l[0], i, 0)),
        out_shape=jax.ShapeDtypeStruct(buf.shape, buf.dtype),
        aliases={2: 0}, name="kv_rows",
    )(larr, z, buf)


def _ret_kernel(l_ref, q_ref, k_ref, v_ref, rg_ref, s0_ref, gn_ref, o_ref, sout_ref, s_scr, *, c_true, c_pad):
    c = pl.program_id(1)

    @pl.when(c == 0)
    def _():
        s_scr[...] = s0_ref[...]

    ri = lax.broadcasted_iota(jnp.int32, (c_pad, c_pad), 0)
    ci = lax.broadcasted_iota(jnp.int32, (c_pad, c_pad), 1)
    diff = ri - ci
    row = lax.broadcasted_iota(jnp.int32, (c_pad, 1), 0)

    def padded(a):
        if c_pad == c_true:
            return a
        return jnp.concatenate([a, jnp.zeros((c_pad - c_true, a.shape[1]), a.dtype)], axis=0)

    for h in range(RET_HEADS):
        lg = _RET_LOG_G[h]
        dmat = jnp.where(diff >= 0, jnp.exp(jnp.maximum(diff, 0).astype(f32) * lg), 0.0)
        q = padded(q_ref[:, h * RET_DK:(h + 1) * RET_DK])
        k = padded(k_ref[:, h * RET_DK:(h + 1) * RET_DK])
        v = padded(v_ref[:, h * RET_DV:(h + 1) * RET_DV])
        qb, kb, vb = q.astype(bf16), k.astype(bf16), v.astype(bf16)
        inner = lax.dot_general(qb, kb, _NT, preferred_element_type=f32) * dmat
        xi = jnp.exp((row + 1).astype(f32) * lg)
        state = s_scr[h]
        o = (jnp.dot(inner.astype(bf16), vb, preferred_element_type=f32)
             + jnp.dot(qb, state.astype(bf16), preferred_element_type=f32) * xi)
        wk = jnp.exp((c_true - 1 - row).astype(f32) * lg)
        kw = (k * wk).astype(bf16)
        s_scr[h] = state * float(np.exp(np.float32(c_true * lg))) + lax.dot_general(
            kw, vb, _TN, preferred_element_type=f32)
        o = o[:c_true]
        mu = jnp.mean(o, axis=-1, keepdims=True)
        cen = o - mu
        var = jnp.mean(cen * cen, axis=-1, keepdims=True)
        y = cen * lax.rsqrt(var + EPS) * gn_ref[h]
        rg = rg_ref[:, h * RET_DV:(h + 1) * RET_DV]
        o_ref[:, h * RET_DV:(h + 1) * RET_DV] = (y * (rg * _sigmoid(rg))).astype(o_ref.dtype)

    @pl.when(c == pl.num_programs(1) - 1)
    def _():
        sout_ref[...] = s_scr[...]


def _retention(larr, z, s0, s0_per_layer, ret_gn, nb, nc, c_true, out_dtype):
    c_pad = max(c_true, RET_CHUNK)
    m = z.shape[0]
    r_qk = RET_HEADS * RET_DK
    r_v = RET_HEADS * RET_DV
    kern = functools.partial(_ret_kernel, c_true=c_true, c_pad=c_pad)
    return _call(
        kern, grid=(nb, nc),
        in_specs=[
            pl.BlockSpec((c_true, r_qk), lambda b, c, l: (b * nc + c, C_RQ // r_qk)),
            pl.BlockSpec((c_true, r_qk), lambda b, c, l: (b * nc + c, C_RK // r_qk)),
            pl.BlockSpec((c_true, r_v), lambda b, c, l: (b * nc + c, C_RV // r_v)),
            pl.BlockSpec((c_true, r_v), lambda b, c, l: (b * nc + c, C_RG // r_v)),
            pl.BlockSpec((None, RET_HEADS, RET_DK, RET_DV), lambda b, c, l: (l[0] * s0_per_layer + b, 0, 0, 0)),
            pl.BlockSpec((None, RET_HEADS, 1, RET_DV), lambda b, c, l: (l[0], 0, 0, 0)),
        ],
        out_specs=[
            pl.BlockSpec((c_true, r_v), lambda b, c, l: (b * nc + c, 0)),
            pl.BlockSpec((None, RET_HEADS, RET_DK, RET_DV), lambda b, c, l: (b, 0, 0, 0)),
        ],
        out_shape=[jax.ShapeDtypeStruct((m, r_v), out_dtype),
                   jax.ShapeDtypeStruct((nb, RET_HEADS, RET_DK, RET_DV), f32)],
        scratch=[pltpu.VMEM((RET_HEADS, RET_DK, RET_DV), f32)],
        name="retention",
    )(larr, z, z, z, z, s0, ret_gn)


def _cmp_kernel(*refs, n_in, n_scalar):
    x_refs = refs[n_scalar:n_scalar + n_in]
    w_ref, a_ref, b_ref = refs[n_scalar + n_in:]
    wa = w_ref[0:CMP_STRIDE]
    wb = w_ref[CMP_STRIDE:CMP_BLOCK]
    for k in range(n_in):
        x = x_refs[k][...]
        r = x.shape[0] // CMP_STRIDE
        x3 = x.reshape((r, CMP_STRIDE) + x.shape[1:])
        a_ref[k * r:(k + 1) * r] = jnp.sum(x3 * wa[None], axis=1)
        b_ref[k * r:(k + 1) * r] = jnp.sum(x3 * wb[None], axis=1)


def _compress_prompt(larr, z, w_cmp, rows):
    m = z.shape[0]
    wcols = 2 * N_KV_HEADS * HEAD_DIM
    kern = functools.partial(_cmp_kernel, n_in=1, n_scalar=1)
    shp = jax.ShapeDtypeStruct((m // CMP_STRIDE, wcols), f32)
    return _call(
        kern, grid=(m // rows,),
        in_specs=[pl.BlockSpec((rows, wcols), lambda i, l: (i, C_KC // wcols)),
                  pl.BlockSpec((None, CMP_BLOCK, wcols), lambda i, l: (l[0], 0, 0))],
        out_specs=[pl.BlockSpec((rows // CMP_STRIDE, wcols), lambda i, l: (i, 0))] * 2,
        out_shape=[shp, shp], name="compress_prompt",
    )(larr, z, w_cmp)


def _compress_pages(larr, page_table, cache, w_cmp, n_pool):
    db, n_pages = page_table.shape
    page, _, sg, d = cache.shape[1:]
    pps = CMP_PAGES_PER_STEP
    steps = n_pages // pps
    sub = page // CMP_STRIDE
    kern = functools.partial(_cmp_kernel, n_in=pps, n_scalar=2)

    def page_spec(k):
        return pl.BlockSpec((None, page, None, sg, d),
                            lambda b, n, l, pt: (l[0] * n_pool + pt[b, n * pps + k], 0, 0, 0, 0))

    shp = jax.ShapeDtypeStruct((db * n_pages * sub, sg, d), f32)
    return _call(
        kern, grid=(db, steps), nsp=2,
        in_specs=[page_spec(k) for k in range(pps)]
        + [pl.BlockSpec((None, CMP_BLOCK, sg, d), lambda b, n, l, pt: (l[0], 0, 0, 0))],
        out_specs=[pl.BlockSpec((pps * sub, sg, d), lambda b, n, l, pt: (b * steps + n, 0, 0))] * 2,
        out_shape=[shp, shp], name="compress_pages",
    )(larr, page_table, *([cache] * pps), w_cmp)


def _combine_cmp(a, b):
    n = a.shape[0]
    row = lax.broadcasted_iota(jnp.int32, (n, 1), 0)
    return a + jnp.where(row == n - 1, 0.0, pltpu.roll(b, n - 1, 0))


def _masked_exp(s, mask, exp=jnp.exp):
    s = jnp.where(mask, s, NEG)
    m = jnp.max(s, axis=-1, keepdims=True)
    e = jnp.where(mask, exp(s - m), 0.0)
    return e, jnp.sum(e, axis=-1, keepdims=True)


def _dot_hilo(p, m01):
    hi = p.astype(bf16)
    lo = (p - hi.astype(f32)).astype(bf16)
    return jnp.dot(hi, m01, preferred_element_type=f32) + jnp.dot(lo, m01, preferred_element_type=f32)


def _topk_mask(imp, k, axis=1):
    pos = lax.broadcasted_iota(jnp.int32, imp.shape, axis).astype(f32)
    sel = jnp.zeros(imp.shape, f32)
    for _ in range(k):
        m = jnp.max(imp, axis=axis, keepdims=True)
        idx = jnp.min(jnp.where(imp == m, pos, 1e9), axis=axis, keepdims=True)
        hit = pos == idx
        sel = jnp.where(hit, 1.0, sel)
        imp = jnp.where(hit, -jnp.inf, imp)
    return sel


def _block_importance_t(p_sum, mcs_t, tpos_row, n_sel):
    hi = p_sum.astype(bf16)
    lo = (p_sum - hi.astype(f32)).astype(bf16)
    imp = (lax.dot_general(mcs_t, hi, _NT, preferred_element_type=f32)
           + lax.dot_general(mcs_t, lo, _NT, preferred_element_type=f32))
    jj = lax.broadcasted_iota(jnp.int32, (imp.shape[0], 1), 0)
    forced = jnp.where(jj == 0, 1, jnp.where(jj == (tpos_row >> 6), 1, 0))
    imp = jnp.where(jj * SEL_BLOCK <= tpos_row, imp, NEG)
    imp = jnp.where(forced == 1, BIG, imp)
    return jnp.where(jj < n_sel, imp, -jnp.inf)


def _block_importance(p_sum, mcs, tpos, n_sel):
    imp = _dot_hilo(p_sum, mcs)
    jj = lax.broadcasted_iota(jnp.int32, (1, imp.shape[1]), 1)
    forced = jnp.where(jj == 0, 1, jnp.where(jj == (tpos >> 6), 1, 0))
    imp = jnp.where(jj * SEL_BLOCK <= tpos, imp, NEG)
    imp = jnp.where(forced == 1, BIG, imp)
    return jnp.where(jj < n_sel, imp, -jnp.inf)


def _tile_row(ref, j):
    n, s, d = ref.shape
    return ref.reshape(n * s, d)[pl.ds(j, n, stride=s), :]


def _gate(sg, idx):
    lane = lax.broadcasted_iota(jnp.int32, (1, sg.shape[1]), 1)
    return jnp.sum(jnp.where(lane == idx, sg, 0.0), axis=-1, keepdims=True)


def _softmax_pv(q, k, v, mask):
    s = jnp.where(mask, lax.dot_general(q, k, _NT, preferred_element_type=f32), NEG)
    e = jnp.exp2(s - jnp.max(s, axis=-1, keepdims=True))
    den = jnp.sum(e, axis=-1, keepdims=True)
    return jnp.dot(e.astype(bf16), v, preferred_element_type=f32) / den


def _nsa_prompt_kernel(l_ref, q_ref, ka_ref, kb_ref, va_ref, vb_ref, ks_ref, vs_ref, kw_ref, vw_ref, ag_ref,
                       mcs_ref, e_ref, o_ref, os_scr, *, tq, t_len, wq, wlen):
    g = pl.program_id(1)
    i = pl.program_id(2)
    t0 = i * tq
    tpos = t0 + lax.broadcasted_iota(jnp.int32, (tq, 1), 0)
    n_c = t_len // CMP_STRIDE
    n_sel = t_len // SEL_BLOCK

    kc = _combine_cmp(ka_ref[...], kb_ref[...]).astype(bf16)
    vc = _combine_cmp(va_ref[...], vb_ref[...]).astype(bf16)
    cend = lax.broadcasted_iota(jnp.int32, (1, n_c), 1) * CMP_STRIDE + (CMP_BLOCK - 1)
    mask_c = cend <= tpos

    qs = [(q_ref[:, r * HEAD_DIM:(r + 1) * HEAD_DIM] * (SCALE * LOG2E)).astype(bf16) for r in range(GROUP)]

    p_sum = jnp.zeros((tq, n_c), f32)
    o_c = []
    for r in range(GROUP):
        s = lax.dot_general(qs[r], kc, _NT, preferred_element_type=f32)
        e, den = _masked_exp(s, mask_c, jnp.exp2)
        p = e / jnp.maximum(den, 1e-30)
        p_sum = p_sum + p
        o_c.append(jnp.dot(p.astype(bf16), vc, preferred_element_type=f32))

    tpos_row = t0 + lax.broadcasted_iota(jnp.int32, (1, tq), 1)
    imp_t = _block_importance_t(p_sum, mcs_ref[...], tpos_row, n_sel)
    selb_t = _topk_mask(imp_t, min(N_SEL, n_sel), axis=0).astype(bf16)

    for br in range(t_len // tq):
        @pl.when(i == br)
        def _(br=br):
            klen = (br + 1) * tq
            selk = lax.dot_general(selb_t, e_ref[:, 0:klen], _TN, preferred_element_type=f32)
            kpos = lax.broadcasted_iota(jnp.int32, (1, klen), 1)
            mask_s = jnp.where(kpos <= tpos, selk, 0.0) > 0.5
            ks = ks_ref[0:klen, :].astype(bf16)
            vs = vs_ref[0:klen, :].astype(bf16)
            for r in range(GROUP):
                os_scr[:, r * HEAD_DIM:(r + 1) * HEAD_DIM] = _softmax_pv(qs[r], ks, vs, mask_s)

    sg = _sigmoid(ag_ref[...])
    gates = [[_gate(sg, (g * GROUP + r) * 3 + k) for k in range(3)] for r in range(GROUP)]
    for h in range(tq // wq):
        rs = slice(h * wq, (h + 1) * wq)
        start = pl.multiple_of(jnp.maximum(t0 + (h + 1) * wq - wlen, 0), wq)
        kw = kw_ref[pl.ds(start, wlen), :].astype(bf16)
        vw = vw_ref[pl.ds(start, wlen), :].astype(bf16)
        d = tpos[rs] - (start + lax.broadcasted_iota(jnp.int32, (1, wlen), 1))
        mask_w = jnp.where(d >= 0, d, WINDOW) < WINDOW
        for r in range(GROUP):
            cs = slice(r * HEAD_DIM, (r + 1) * HEAD_DIM)
            o_w = _softmax_pv(qs[r][rs], kw, vw, mask_w)
            out = gates[r][0][rs] * o_c[r][rs] + gates[r][1][rs] * os_scr[rs, cs] + gates[r][2][rs] * o_w
            o_ref[rs, cs] = out.astype(o_ref.dtype)


def _nsa_prompt(larr, z, cmp_a, cmp_b, mcs, emat, nb, t_len, tq):
    m = z.shape[0]
    nq = t_len // tq
    n_c = t_len // CMP_STRIDE
    wq = min(tq, 256)
    wlen = min(WINDOW + wq, t_len)
    gw = GROUP * HEAD_DIM
    kern = functools.partial(_nsa_prompt_kernel, tq=tq, t_len=t_len, wq=wq, wlen=wlen)

    def head_cols(c0):
        return pl.BlockSpec((t_len, HEAD_DIM), lambda b, g, i, l: (b, c0 // HEAD_DIM + g))

    def cmp_spec(off):
        return pl.BlockSpec((n_c, HEAD_DIM), lambda b, g, i, l: (b, off + g))

    return _call(
        kern, grid=(nb, N_KV_HEADS, nq),
        in_specs=[
            pl.BlockSpec((tq, gw), lambda b, g, i, l: (b * nq + i, C_AQ // gw + g)),
            cmp_spec(0), cmp_spec(0), cmp_spec(N_KV_HEADS), cmp_spec(N_KV_HEADS),
            head_cols(C_KS), head_cols(C_VS), head_cols(C_KW), head_cols(C_VW),
            pl.BlockSpec((tq, LANE), lambda b, g, i, l: (b * nq + i, C_AG // LANE)),
            pl.BlockSpec(mcs.shape, lambda b, g, i, l: (0, 0)),
            pl.BlockSpec(emat.shape, lambda b, g, i, l: (0, 0)),
        ],
        out_specs=pl.BlockSpec((tq, gw), lambda b, g, i, l: (b * nq + i, g)),
        out_shape=jax.ShapeDtypeStruct((m, N_HEADS * HEAD_DIM), bf16),
        scratch=[pltpu.VMEM((tq, gw), f32)],
        name="nsa_prompt",
    )(larr, z, cmp_a, cmp_b, cmp_a, cmp_b, z, z, z, z, z, mcs, emat)


def _rows_rt(ref, g_off, tn):
    return jnp.concatenate(
        [ref[:, (g_off + r) * HEAD_DIM:(g_off + r + 1) * HEAD_DIM] for r in range(GROUP)], axis=0)


def _nsa_s_cmp_kernel(l_ref, q_ref, a_ref, b_ref, mcs_ref, oc_ref, sel_ref, *, tn, past, n_sel):
    n_c = a_ref.shape[0]
    rows = GROUP * tn
    tpos_r = past + (lax.broadcasted_iota(jnp.int32, (rows, 1), 0) % tn)
    tpos = past + (lax.broadcasted_iota(jnp.int32, (2 * tn, 1), 0) % tn)
    cend = lax.broadcasted_iota(jnp.int32, (1, n_c), 1) * CMP_STRIDE + (CMP_BLOCK - 1)
    mask_c = cend <= tpos_r
    mcs = mcs_ref[...]
    for g in range(N_KV_HEADS):
        q = (_rows_rt(q_ref, g * GROUP, tn) * SCALE).astype(bf16)
        kc = _combine_cmp(_tile_row(a_ref, g), _tile_row(b_ref, g)).astype(bf16)
        vc = _combine_cmp(_tile_row(a_ref, N_KV_HEADS + g), _tile_row(b_ref, N_KV_HEADS + g)).astype(bf16)
        s = lax.dot_general(q, kc, _NT, preferred_element_type=f32)
        e, den = _masked_exp(s, mask_c)
        p = e / jnp.maximum(den, 1e-30)
        oc_ref[g] = jnp.dot(p.astype(bf16), vc, preferred_element_type=f32)
        p_sum = p[0:tn]
        for r in range(1, GROUP):
            p_sum = p_sum + p[r * tn:(r + 1) * tn]
        p_sum = jnp.concatenate([p_sum, jnp.zeros_like(p_sum)], axis=0)
        imp = _block_importance(p_sum, mcs, tpos, n_sel)
        sel_ref[g] = _topk_mask(imp, min(N_SEL, n_sel))[0:tn]


def _nsa_s_cmp(larr, z, cmp_a, cmp_b, mcs, db, tn, past, n_sel):
    n_c = cmp_a.shape[0] // db
    aq = N_HEADS * HEAD_DIM
    kern = functools.partial(_nsa_s_cmp_kernel, tn=tn, past=past, n_sel=n_sel)
    cmp_spec = pl.BlockSpec((n_c,) + cmp_a.shape[1:], lambda b, l: (b, 0, 0))
    return _call(
        kern, grid=(db,),
        in_specs=[pl.BlockSpec((tn, aq), lambda b, l: (b, C_AQ // aq)), cmp_spec, cmp_spec,
                  pl.BlockSpec(mcs.shape, lambda b, l: (0, 0))],
        out_specs=[pl.BlockSpec((None, N_KV_HEADS, GROUP * tn, HEAD_DIM), lambda b, l: (b, 0, 0, 0)),
                   pl.BlockSpec((None, N_KV_HEADS, tn, mcs.shape[1]), lambda b, l: (b, 0, 0, 0))],
        out_shape=[jax.ShapeDtypeStruct((db, N_KV_HEADS, GROUP * tn, HEAD_DIM), f32),
                   jax.ShapeDtypeStruct((db, N_KV_HEADS, tn, mcs.shape[1]), f32)],
        name="nsa_sample_cmp",
    )(larr, z, cmp_a, cmp_b, mcs)


def _nsa_s_sel_kernel(*refs, tn, pps):
    q_ref, sel_ref, e_ref = refs[2:5]
    page_refs = refs[5:5 + pps]
    acc_ref, m_ref, l_ref = refs[5 + pps:]
    n = pl.program_id(1)

    @pl.when(n == 0)
    def _():
        acc_ref[...] = jnp.zeros_like(acc_ref)
        m_ref[...] = jnp.full_like(m_ref, NEG)
        l_ref[...] = jnp.zeros_like(l_ref)

    emat = e_ref[...]
    for g in range(N_KV_HEADS):
        q = (_rows_rt(q_ref, g * GROUP, tn) * SCALE).astype(bf16)
        s = jnp.concatenate(
            [lax.dot_general(q, _tile_row(pr, g).astype(bf16), _NT, preferred_element_type=f32)
             for pr in page_refs], axis=1)
        selg = sel_ref[g]
        selg = jnp.concatenate([selg, jnp.zeros_like(selg)], axis=0).astype(bf16)
        selk = jnp.dot(selg, emat, preferred_element_type=f32)[0:tn]
        mask = jnp.concatenate([selk] * GROUP, axis=0) > 0.5
        s = jnp.where(mask, s, NEG)
        m_old = m_ref[g]
        m_new = jnp.maximum(m_old, jnp.max(s, axis=-1, keepdims=True))
        alpha = jnp.exp(m_old - m_new)
        e = jnp.where(mask, jnp.exp(s - m_new[:, 0:1]), 0.0)
        l_ref[g] = alpha * l_ref[g] + jnp.sum(e, axis=-1, keepdims=True)
        pv = None
        for k, pr in enumerate(page_refs):
            page = pr.shape[0]
            ek = e[:, k * page:(k + 1) * page].astype(bf16)
            vk = _tile_row(pr, N_KV_HEADS + g).astype(bf16)
            d = jnp.dot(ek, vk, preferred_element_type=f32)
            pv = d if pv is None else pv + d
        acc_ref[g] = alpha * acc_ref[g] + pv
        m_ref[g] = m_new


def _nsa_s_sel(larr, page_table, z, sel, emat_pages, cache, n_pool, tn):
    db, n_pages = page_table.shape
    page, _, sg, d = cache.shape[1:]
    pps = PAGES_PER_STEP
    steps = n_pages // pps
    rows = GROUP * tn
    kern = functools.partial(_nsa_s_sel_kernel, tn=tn, pps=pps)

    def page_spec(k):
        return pl.BlockSpec((None, page, None, sg, d),
                            lambda b, n, l, pt: (l[0] * n_pool + pt[b, n * pps + k], 0, 1, 0, 0))

    st = jax.ShapeDtypeStruct((db, N_KV_HEADS, rows, HEAD_DIM), f32)
    st_spec = pl.BlockSpec((None, N_KV_HEADS, rows, HEAD_DIM), lambda b, n, l, pt: (b, 0, 0, 0))
    return _call(
        kern, grid=(db, steps), nsp=2,
        in_specs=[pl.BlockSpec((tn, N_HEADS * HEAD_DIM), lambda b, n, l, pt: (b, C_AQ // (N_HEADS * HEAD_DIM))),
                  pl.BlockSpec((None, None, N_KV_HEADS, tn, LANE), lambda b, n, l, pt: (b, n, 0, 0, 0)),
                  pl.BlockSpec(emat_pages.shape, lambda b, n, l, pt: (0, 0))]
        + [page_spec(k) for k in range(pps)],
        out_specs=[st_spec, st_spec, st_spec],
        out_shape=[st, st, st], name="nsa_sample_sel",
    )(larr, page_table, z, sel, emat_pages, *([cache] * pps))


def _nsa_s_fin_kernel(l_ref, q_ref, ksn_ref, vsn_ref, kwn_ref, vwn_ref, ag_ref, buf_ref,
                      acc_ref, m_ref, lsum_ref, oc_ref, sel_ref, o_ref, *, tn, past, n_sel):
    rows = GROUP * tn
    wc = buf_ref.shape[0]
    tpos = past + (lax.broadcasted_iota(jnp.int32, (rows, 1), 0) % tn)
    kidx = lax.broadcasted_iota(jnp.int32, (1, LANE), 1)
    widx = lax.broadcasted_iota(jnp.int32, (1, wc + LANE), 1)
    d = tpos - (past - wc + widx)
    mask_w = jnp.where(widx < wc + tn, jnp.where(d >= 0, d, WINDOW), WINDOW) < WINDOW
    sg = _sigmoid(ag_ref[...])

    def new_rows(ref, g):
        a = ref[:, g * HEAD_DIM:(g + 1) * HEAD_DIM]
        return jnp.concatenate([a, jnp.zeros((LANE - tn, HEAD_DIM), a.dtype)], axis=0)

    for g in range(N_KV_HEADS):
        q = (_rows_rt(q_ref, g * GROUP, tn) * SCALE).astype(bf16)

        ksn = new_rows(ksn_ref, g).astype(bf16)
        vsn = new_rows(vsn_ref, g).astype(bf16)
        s = lax.dot_general(q, ksn, _NT, preferred_element_type=f32)
        last_sel = sel_ref[g][:, n_sel - 1:n_sel]
        last_sel = jnp.concatenate([last_sel] * GROUP, axis=0)
        mask = jnp.where(kidx < tn, jnp.where(past + kidx <= tpos, last_sel, 0.0), 0.0) > 0.5
        s = jnp.where(mask, s, NEG)
        m_old = m_ref[g]
        m_new = jnp.maximum(m_old, jnp.max(s, axis=-1, keepdims=True))
        alpha = jnp.exp(m_old - m_new)
        e = jnp.where(mask, jnp.exp(s - m_new[:, 0:1]), 0.0)
        den = alpha * lsum_ref[g] + jnp.sum(e, axis=-1, keepdims=True)
        o_s = (alpha * acc_ref[g] + jnp.dot(e.astype(bf16), vsn, preferred_element_type=f32)) / jnp.maximum(den, 1e-30)

        kw = jnp.concatenate([_tile_row(buf_ref, g), new_rows(kwn_ref, g)], axis=0).astype(bf16)
        vw = jnp.concatenate([_tile_row(buf_ref, N_KV_HEADS + g), new_rows(vwn_ref, g)], axis=0).astype(bf16)
        s = lax.dot_general(q, kw, _NT, preferred_element_type=f32)
        e, den = _masked_exp(s, mask_w)
        o_w = jnp.dot(e.astype(bf16), vw, preferred_element_type=f32) / jnp.maximum(den, 1e-30)

        o_c = oc_ref[g]
        for r in range(GROUP):
            base = (g * GROUP + r) * 3
            sl = slice(r * tn, (r + 1) * tn)
            out = _gate(sg, base) * o_c[sl] + _gate(sg, base + 1) * o_s[sl] + _gate(sg, base + 2) * o_w[sl]
            o_ref[:, (g * GROUP + r) * HEAD_DIM:(g * GROUP + r + 1) * HEAD_DIM] = out


def _nsa_s_fin(larr, z, win_buf, acc, mx, lsum, o_c, sel, db, tn, past, n_sel):
    aq = N_HEADS * HEAD_DIM
    kvw = N_KV_HEADS * HEAD_DIM
    rows = GROUP * tn
    kern = functools.partial(_nsa_s_fin_kernel, tn=tn, past=past, n_sel=n_sel)

    def new_cols(c0):
        return pl.BlockSpec((tn, kvw), lambda b, l: (b, c0 // kvw))

    st_spec = pl.BlockSpec((None, N_KV_HEADS, rows, HEAD_DIM), lambda b, l: (b, 0, 0, 0))
    return _call(
        kern, grid=(db,),
        in_specs=[pl.BlockSpec((tn, aq), lambda b, l: (b, C_AQ // aq)),
                  new_cols(C_KS), new_cols(C_VS), new_cols(C_KW), new_cols(C_VW),
                  pl.BlockSpec((tn, LANE), lambda b, l: (b, C_AG // LANE)),
                  pl.BlockSpec((None,) + win_buf.shape[1:], lambda b, l: (l[0] * db + b, 0, 0, 0)),
                  st_spec, st_spec, st_spec, st_spec,
                  pl.BlockSpec((None, N_KV_HEADS, tn, sel.shape[-1]), lambda b, l: (b, 0, 0, 0))],
        out_specs=pl.BlockSpec((tn, aq), lambda b, l: (b, 0)),
        out_shape=jax.ShapeDtypeStruct((db * tn, aq), f32),
        name="nsa_sample_fin",
    )(larr, z, z, z, z, z, z, win_buf, acc, mx, lsum, o_c, sel)


def _merge_kernel(l_ref, x_ref, or_ref, oa_ref, ga_ref, gb_ref, wpa_ref, wpb_ref, wo_ref, o_ref, acc_ref):
    j = pl.program_id(1)

    @pl.when(j == 0)
    def _():
        acc_ref[...] = jnp.zeros_like(acc_ref)

    pa = jnp.dot(or_ref[...], wpa_ref[...], preferred_element_type=f32)
    pb = jnp.dot(oa_ref[...], wpb_ref[...], preferred_element_type=f32)
    mix = _sigmoid(ga_ref[...]) * pa + _sigmoid(gb_ref[...]) * pb
    acc_ref[...] += jnp.dot(mix.astype(bf16), wo_ref[...], preferred_element_type=f32)

    @pl.when(j == pl.num_programs(1) - 1)
    def _():
        o_ref[...] = x_ref[...] + acc_ref[...]


def _merge(larr, x, o_r, o_a, z, w_pa, w_pb, w_out, tm):
    m = x.shape[0]
    r_v = RET_HEADS * RET_DV
    a_q = N_HEADS * HEAD_DIM
    return _call(
        _merge_kernel, grid=(m // tm, D_MODEL // TN),
        in_specs=[
            pl.BlockSpec((tm, D_MODEL), lambda i, j, l: (i, 0)),
            pl.BlockSpec((tm, r_v), lambda i, j, l: (i, 0)),
            pl.BlockSpec((tm, a_q), lambda i, j, l: (i, 0)),
            pl.BlockSpec((tm, TN), lambda i, j, l: (i, C_GA // TN + j)),
            pl.BlockSpec((tm, TN), lambda i, j, l: (i, C_GB // TN + j)),
            pl.BlockSpec((None, r_v, TN), lambda i, j, l: (l[0], 0, j)),
            pl.BlockSpec((None, a_q, TN), lambda i, j, l: (l[0], 0, j)),
            pl.BlockSpec((None, TN, D_MODEL), lambda i, j, l: (l[0], j, 0)),
        ],
        out_specs=pl.BlockSpec((tm, D_MODEL), lambda i, j, l: (i, 0)),
        out_shape=jax.ShapeDtypeStruct((m, D_MODEL), f32),
        scratch=[pltpu.VMEM((tm, D_MODEL), f32)],
        name="merge",
    )(larr, x, o_r, o_a, z, z, w_pa, w_pb, w_out)


def _rope_tables(pos):
    half = HEAD_DIM // 2
    inv = 1.0 / (ROPE_THETA ** (jnp.arange(half, dtype=f32) / half))
    ang = pos.astype(f32)[:, None] * inv[None, :]
    cos, sin = jnp.cos(ang), jnp.sin(ang)
    return jnp.concatenate([cos, cos], axis=-1), jnp.concatenate([-sin, sin], axis=-1)


def _cmp_to_sel_table(n_c_valid, n_sel, rows, cols):
    cs = np.arange(n_c_valid) * CMP_STRIDE
    ce = cs + CMP_BLOCK - 1
    js = np.arange(n_sel) * SEL_BLOCK
    je = js + SEL_BLOCK - 1
    tab = np.zeros((rows, cols), np.float32)
    tab[:n_c_valid, :n_sel] = (cs[:, None] <= je[None, :]) & (ce[:, None] >= js[None, :])
    return jnp.asarray(tab, dtype=bf16)


def _expand_table(rows, n_keys, first_block=0):
    tab = np.zeros((rows, n_keys), np.float32)
    s = np.arange(n_keys)
    tab[first_block + s // SEL_BLOCK, s] = 1.0
    return tab


def _round_up(a, b):
    return -(-a // b) * b


def kernel(x_prompt, x_sample, cache_kv, state_win, state_ret, page_table, norm_gain, ffn_gate,
           ffn_up, ffn_down, w_in, qk_norm, cmp_w, ret_gn, w_pa, w_pb, w_out):
    nb, t_len, _ = x_prompt.shape
    db, tn, _ = x_sample.shape
    depth, n_pool, page = cache_kv.shape[:3]
    n_pages = page_table.shape[1]
    past = n_pages * page
    wc = state_win.shape[2]
    kvw = N_KV_HEADS * HEAD_DIM
    assert t_len % 512 == 0 and t_len >= WINDOW and wc == WINDOW and tn == 8
    assert n_pages % PAGES_PER_STEP == 0 and n_pages % CMP_PAGES_PER_STEP == 0
    assert past % SEL_BLOCK == 0 and tn <= CMP_STRIDE

    wg, wu, wd = ffn_gate.astype(bf16), ffn_up.astype(bf16), ffn_down.astype(bf16)
    w_in_p = jnp.pad(w_in.astype(bf16), ((0, 0), (0, 0), (0, DZ - w_in.shape[-1])))
    w_in_p = lax.dynamic_update_slice(w_in_p, w_in[..., W_IN_SPLIT:].astype(bf16), (0, 0, C_GA))
    w_pa_b, w_pb_b, w_out_b = w_pa.astype(bf16), w_pb.astype(bf16), w_out.astype(bf16)

    gains = norm_gain.reshape(depth, 3, 1, D_MODEL)
    ones = jnp.ones((depth, HEAD_DIM), f32)
    tile_rows = []
    for j in range(N_ZT):
        c0 = j * TN
        if c0 == C_KC:
            tile_rows.append(qk_norm[:, 1])
        elif c0 == C_KS:
            tile_rows.append(qk_norm[:, 2])
        elif c0 == C_KW:
            tile_rows.append(qk_norm[:, 3])
        elif C_AQ <= c0 < C_KC:
            tile_rows.append(qk_norm[:, 0])
        elif C_RK <= c0 < C_RV:
            tile_rows.append(ones * (RET_DK ** -0.5))
        else:
            tile_rows.append(ones)
    tile_gain = jnp.stack(tile_rows, axis=1).reshape(depth, N_ZT, 1, HEAD_DIM)
    modes = jnp.asarray(_TILE_MODE, jnp.int32)

    w_cmp = jnp.repeat(cmp_w, HEAD_DIM, axis=-1)
    w_cmp = w_cmp.transpose(0, 2, 1, 3).reshape(depth, CMP_BLOCK, 2 * kvw)
    w_cmp_rows = w_cmp.reshape(depth, CMP_BLOCK, 2 * N_KV_HEADS, HEAD_DIM)
    gn = ret_gn.reshape(depth, RET_HEADS, 1, RET_DV)

    cos_p, sin_p = _rope_tables(jnp.arange(t_len, dtype=jnp.int32))
    cos_s, sin_s = _rope_tables(jnp.tile(past + jnp.arange(tn, dtype=jnp.int32), db))

    n_c_p = t_len // CMP_STRIDE
    n_sel_p = t_len // SEL_BLOCK
    mcs_p = _cmp_to_sel_table((t_len - CMP_BLOCK) // CMP_STRIDE + 1, n_sel_p, n_c_p, n_sel_p).T
    emat_p = jnp.asarray(_expand_table(n_sel_p, t_len), dtype=bf16)
    l_full = past + tn
    n_sel_s = -(-l_full // SEL_BLOCK)
    n_c_s = past // CMP_STRIDE
    sel_w = _round_up(n_sel_s, LANE)
    mcs_s = _cmp_to_sel_table((l_full - CMP_BLOCK) // CMP_STRIDE + 1, n_sel_s, n_c_s, sel_w)
    steps = n_pages // PAGES_PER_STEP
    keys_per_step = PAGES_PER_STEP * page
    blocks_per_step = keys_per_step // SEL_BLOCK
    emat_s = jnp.asarray(_expand_table(LANE, keys_per_step), dtype=bf16)

    cache2 = cache_kv.reshape(depth * n_pool, page, 2, 2 * N_KV_HEADS, HEAD_DIM)
    win2 = state_win.reshape(depth * db, wc, 2 * N_KV_HEADS, HEAD_DIM)
    sret2 = state_ret.reshape(depth * db, RET_HEADS, RET_DK, RET_DV)
    zero_state = jnp.zeros((nb, RET_HEADS, RET_DK, RET_DV), f32)

    mp = nb * t_len
    ms = db * tn
    tm_p = 512
    rb = 512
    keep = min(WINDOW, t_len)
    tm_in = 1024
    tq = 512
    c_p = math.gcd(t_len, RET_CHUNK)

    def layer(carry, l):
        xp, xs, kv_buf, win_buf = carry
        larr = jnp.reshape(l, (1,)).astype(jnp.int32)

        xp = _ffn(larr, xp, gains, 0, wg, wu, wd, 0, tm_p)
        xs = _ffn(larr, xs, gains, 0, wg, wu, wd, 0, ms)

        zp = _inproj(larr, modes, xp, gains, w_in_p, tile_gain, cos_p, sin_p, min(tm_in, t_len))
        o_r, s_fin = _retention(larr, zp, zero_state, 0, gn, nb, t_len // c_p, c_p, bf16)
        ca, cb = _compress_prompt(larr, zp, w_cmp, 512)
        o_a = _nsa_prompt(larr, zp, ca, cb, mcs_p, emat_p, nb, t_len, tq)
        xp = _merge(larr, xp, o_r, o_a, zp, w_pa_b, w_pb_b, w_out_b, tm_p)
        kv_buf = _kv_rows(larr, zp, kv_buf, rb, C_KC, 4 * kvw, mp // rb, lambda i: i)
        win_buf = _kv_rows(larr, zp, win_buf, rb, C_KW, 2 * kvw, nb * (keep // rb),
                           lambda i: (i // (keep // rb)) * (t_len // rb) + (t_len - keep) // rb + i % (keep // rb))

        zs = _inproj(larr, modes, xs, gains, w_in_p, tile_gain, cos_s, sin_s, ms)
        o_rs, s_new = _retention(larr, zs, sret2, db, gn, db, 1, tn, f32)
        sa, sb = _compress_pages(larr, page_table, cache2, w_cmp_rows, n_pool)
        o_c, sel = _nsa_s_cmp(larr, zs, sa, sb, mcs_s, db, tn, past, n_sel_s)
        sel_steps = sel[..., :steps * blocks_per_step].reshape(db, N_KV_HEADS, tn, steps, blocks_per_step)
        sel_steps = jnp.pad(sel_steps.transpose(0, 3, 1, 2, 4), ((0, 0),) * 4 + ((0, LANE - blocks_per_step),))
        acc, mx, lsum = _nsa_s_sel(larr, page_table, zs, sel_steps, emat_s, cache2, n_pool, tn)
        o_as = _nsa_s_fin(larr, zs, win2, acc, mx, lsum, o_c, sel, db, tn, past, n_sel_s)
        xs = _merge(larr, xs, o_rs.astype(bf16), o_as.astype(bf16), zs, w_pa_b, w_pb_b, w_out_b, ms)
        kv_s = zs[:, C_KC:C_KW].reshape(db, tn, 4, N_KV_HEADS, HEAD_DIM)
        win_new = zs[:, C_KW:C_AG].reshape(db, tn, 2, N_KV_HEADS, HEAD_DIM)
        win_old = lax.dynamic_index_in_dim(state_win, l, 0, keepdims=False)
        win_s = jnp.concatenate([win_old[:, tn:], win_new], axis=1)

        xp = _ffn(larr, xp, gains, 2, wg, wu, wd, 1, tm_p)
        xs = _ffn(larr, xs, gains, 2, wg, wu, wd, 1, ms)
        return (xp, xs, kv_buf, win_buf), (kv_s, win_s, s_fin, s_new)

    kv_buf0 = jnp.zeros((depth, mp * 4 * kvw // LANE, LANE), f32)
    win_buf0 = jnp.zeros((depth, nb * keep * 2 * kvw // LANE, LANE), f32)
    (xp, xs, kv_buf, win_buf), outs = lax.scan(
        layer, (x_prompt.reshape(mp, D_MODEL), x_sample.reshape(ms, D_MODEL), kv_buf0, win_buf0),
        jnp.arange(depth, dtype=jnp.int32))
    kv_s, win_s, ret_p, ret_s = outs
    kv_p = kv_buf.reshape(depth, nb, t_len, 4, N_KV_HEADS, HEAD_DIM)
    win_p = win_buf.reshape(depth, nb, keep, 2, N_KV_HEADS, HEAD_DIM)
    return (xp.reshape(nb, t_len, D_MODEL), xs.reshape(db, tn, D_MODEL), kv_p, kv_s, win_p, win_s, ret_p, ret_s)
```

```python
import functools
import math

import jax
import jax.numpy as jnp
import numpy as np
from jax import lax
from jax.experimental import pallas as pl
from jax.experimental.pallas import tpu as pltpu

D_MODEL = 2048
D_FF = 5504
RET_HEADS = 8
RET_DK = 128
RET_DV = 256
RET_CHUNK = 128
N_HEADS = 16
N_KV_HEADS = 4
HEAD_DIM = 128
GROUP = N_HEADS // N_KV_HEADS
CMP_BLOCK = 32
CMP_STRIDE = 16
SEL_BLOCK = 64
N_SEL = 16
WINDOW = 512
ROPE_THETA = 10000.0
EPS = 1e-6
NEG = -1e30
BIG = 1e30
SCALE = HEAD_DIM ** -0.5
LOG2E = math.log2(math.e)

LANE = 128
VMEM_LIMIT = 56 * 1024 * 1024

TN = 512
C_RQ, C_RK, C_RV, C_RG = 0, 1024, 2048, 4096
C_AQ = 6144
C_KC, C_VC, C_KS, C_VS, C_KW, C_VW = 8192, 8704, 9216, 9728, 10240, 10752
C_AG = 11264
C_GA, C_GB = 11776, 13824
DZ = 15872
N_ZT = DZ // TN
W_IN_SPLIT = 11312
_TILE_MODE = [1, 1, 1, 1] + [0] * 8 + [2, 2, 2, 2] + [2, 0, 2, 0, 2, 0] + [0] * 9

TF = 512
F_TILES = -(-D_FF // TF)
PAGES_PER_STEP = 8
CMP_PAGES_PER_STEP = 16

_RET_LOG_G = [float(np.log(np.float32(1.0) - np.float32(2.0) ** np.float32(-5.0 - h))) for h in range(RET_HEADS)]

_NT = (((1,), (1,)), ((), ()))
_TN = (((0,), (0,)), ((), ()))

bf16 = jnp.bfloat16
f32 = jnp.float32


def _sigmoid(x):
    return 1.0 / (1.0 + jnp.exp(-x))


def _call(kernel, *, grid, in_specs, out_specs, out_shape, scratch=(), nsp=1, sem=None, name=None, aliases=None):
    return pl.pallas_call(
        kernel,
        grid_spec=pltpu.PrefetchScalarGridSpec(num_scalar_prefetch=nsp, grid=grid, in_specs=in_specs,
                                               out_specs=out_specs, scratch_shapes=list(scratch)),
        out_shape=out_shape,
        compiler_params=pltpu.CompilerParams(dimension_semantics=sem or ("arbitrary",) * len(grid),
                                             vmem_limit_bytes=VMEM_LIMIT),
        input_output_aliases=aliases or {},
        name=name,
    )


def _ffn_kernel(l_ref, x_ref, g_ref, wg_ref, wu_ref, wd_ref, o_ref, h_ref, acc_ref):
    f = pl.program_id(1)

    @pl.when(f == 0)
    def _():
        x = x_ref[...]
        ms = jnp.mean(x * x, axis=-1, keepdims=True)
        h_ref[...] = (x * lax.rsqrt(ms + EPS) * g_ref[...]).astype(bf16)
        acc_ref[...] = jnp.zeros_like(acc_ref)

    def accumulate(width):
        h = h_ref[...]
        a = jnp.dot(h, wg_ref[:, 0:width], preferred_element_type=f32)
        b = jnp.dot(h, wu_ref[:, 0:width], preferred_element_type=f32)
        s = (a * _sigmoid(a)) * b
        acc_ref[...] += jnp.dot(s.astype(bf16), wd_ref[0:width, :], preferred_element_type=f32)

    last = pl.num_programs(1) - 1

    @pl.when(f < last)
    def _():
        accumulate(TF)

    @pl.when(f == last)
    def _():
        accumulate(D_FF - (F_TILES - 1) * TF)
        o_ref[...] = x_ref[...] + 0.5 * acc_ref[...]


def _ffn(larr, x, gains, which_gain, wg, wu, wd, which_w, tm):
    m = x.shape[0]
    grid = (m // tm, F_TILES)
    return _call(
        _ffn_kernel, grid=grid,
        in_specs=[
            pl.BlockSpec((tm, D_MODEL), lambda i, f, l: (i, 0)),
            pl.BlockSpec((None, None, 1, D_MODEL), lambda i, f, l: (l[0], which_gain, 0, 0)),
            pl.BlockSpec((None, None, D_MODEL, TF), lambda i, f, l: (l[0], which_w, 0, f)),
            pl.BlockSpec((None, None, D_MODEL, TF), lambda i, f, l: (l[0], which_w, 0, f)),
            pl.BlockSpec((None, None, TF, D_MODEL), lambda i, f, l: (l[0], which_w, f, 0)),
        ],
        out_specs=pl.BlockSpec((tm, D_MODEL), lambda i, f, l: (i, 0)),
        out_shape=jax.ShapeDtypeStruct((m, D_MODEL), f32),
        scratch=[pltpu.VMEM((tm, D_MODEL), bf16), pltpu.VMEM((tm, D_MODEL), f32)],
        name="ffn",
    )(larr, x, gains, wg, wu, wd)


def _inproj_kernel(l_ref, mode_ref, x_ref, g_ref, w_ref, gain_ref, cos_ref, sin_ref, o_ref, h_ref, acc_a, acc_b):
    j = pl.program_id(1)
    n_tiles = pl.num_programs(1) - 1
    mode = mode_ref[jnp.maximum(j - 1, 0)]

    def matmul(acc):
        acc[...] = jnp.dot(h_ref[...], w_ref[...], preferred_element_type=f32)

    def finish_plain(acc):
        o_ref[...] = acc[...]

    def finish_rotary(acc):
        cos = cos_ref[...]
        sin = sin_ref[...]
        gain = gain_ref[...]
        for hd in range(TN // HEAD_DIM):
            y = acc[:, hd * HEAD_DIM:(hd + 1) * HEAD_DIM]
            ms = jnp.mean(y * y, axis=-1, keepdims=True)
            inv = jnp.where(mode == 2, lax.rsqrt(ms + EPS), 1.0)
            y = y * inv * gain
            o_ref[:, hd * HEAD_DIM:(hd + 1) * HEAD_DIM] = y * cos + pltpu.roll(y, HEAD_DIM // 2, 1) * sin

    @pl.when(j == 0)
    def _():
        x = x_ref[...]
        ms = jnp.mean(x * x, axis=-1, keepdims=True)
        h_ref[...] = (x * lax.rsqrt(ms + EPS) * g_ref[...]).astype(bf16)
        matmul(acc_a)

    for parity, (cur, prev) in enumerate(((acc_a, acc_b), (acc_b, acc_a))):
        mid = (j > 0) & (j < n_tiles) & ((j & 1) == parity)

        @pl.when(mid & (mode == 0))
        def _(cur=cur, prev=prev):
            matmul(cur)
            finish_plain(prev)

        @pl.when(mid & (mode != 0))
        def _(cur=cur, prev=prev):
            matmul(cur)
            finish_rotary(prev)

    @pl.when(j == n_tiles)
    def _():
        finish_plain(acc_a if N_ZT % 2 == 1 else acc_b)


def _inproj(larr, modes, x, gains, w_in, tile_gain, cos2, sin2, tm):
    assert _TILE_MODE[-1] == 0
    m = x.shape[0]
    nt = cos2.shape[0] // tm
    grid = (m // tm, N_ZT + 1)
    last = N_ZT - 1
    return _call(
        _inproj_kernel, grid=grid, nsp=2,
        in_specs=[
            pl.BlockSpec((tm, D_MODEL), lambda i, j, l, md: (i, 0)),
            pl.BlockSpec((None, None, 1, D_MODEL), lambda i, j, l, md: (l[0], 1, 0, 0)),
            pl.BlockSpec((None, D_MODEL, TN), lambda i, j, l, md: (l[0], 0, jnp.minimum(j, last))),
            pl.BlockSpec((None, None, 1, HEAD_DIM), lambda i, j, l, md: (l[0], jnp.maximum(j - 1, 0), 0, 0)),
            pl.BlockSpec((tm, HEAD_DIM), lambda i, j, l, md: (i % nt, 0)),
            pl.BlockSpec((tm, HEAD_DIM), lambda i, j, l, md: (i % nt, 0)),
        ],
        out_specs=pl.BlockSpec((tm, TN), lambda i, j, l, md: (i, jnp.maximum(j - 1, 0))),
        out_shape=jax.ShapeDtypeStruct((m, DZ), f32),
        scratch=[pltpu.VMEM((tm, D_MODEL), bf16), pltpu.VMEM((tm, TN), f32), pltpu.VMEM((tm, TN), f32)],
        name="inproj",
    )(larr, modes, x, gains, w_in, tile_gain, cos2, sin2)


def _kv_rows_kernel(l_ref, z_ref, buf_ref, o_ref):
    del buf_ref
    rows = z_ref.shape[0]
    n = z_ref.shape[1] // LANE
    for c in range(n):
        o_ref[pl.ds(c, rows, stride=n), :] = z_ref[:, c * LANE:(c + 1) * LANE]


def _kv_rows(larr, z, buf, rows, col0, width, n_blocks, in_block):
    n = width // LANE
    return _call(
        _kv_rows_kernel, grid=(n_blocks,),
        in_specs=[pl.BlockSpec((rows, width), lambda i, l: (in_block(i), col0 // width)),
                  pl.BlockSpec(memory_space=pl.ANY)],
        out_specs=pl.BlockSpec((None, rows * n, LANE), lambda i, l: (l[0], i, 0)),
        out_shape=jax.ShapeDtypeStruct(buf.shape, buf.dtype),
        aliases={2: 0}, name="kv_rows",
    )(larr, z, buf)


def _ret_kernel(l_ref, q_ref, k_ref, v_ref, rg_ref, s0_ref, gn_ref, o_ref, sout_ref, s_scr, *, c_true, c_pad):
    c = pl.program_id(1)

    @pl.when(c == 0)
    def _():
        s_scr[...] = s0_ref[...]

    ri = lax.broadcasted_iota(jnp.int32, (c_pad, c_pad), 0)
    ci = lax.broadcasted_iota(jnp.int32, (c_pad, c_pad), 1)
    diff = ri - ci
    row = lax.broadcasted_iota(jnp.int32, (c_pad, 1), 0)

    def padded(a):
        if c_pad == c_true:
            return a
        return jnp.concatenate([a, jnp.zeros((c_pad - c_true, a.shape[1]), a.dtype)], axis=0)

    for h in range(RET_HEADS):
        lg = _RET_LOG_G[h]
        dmat = jnp.where(diff >= 0, jnp.exp(jnp.maximum(diff, 0).astype(f32) * lg), 0.0)
        q = padded(q_ref[:, h * RET_DK:(h + 1) * RET_DK])
        k = padded(k_ref[:, h * RET_DK:(h + 1) * RET_DK])
        v = padded(v_ref[:, h * RET_DV:(h + 1) * RET_DV])
        qb, kb, vb = q.astype(bf16), k.astype(bf16), v.astype(bf16)
        inner = lax.dot_general(qb, kb, _NT, preferred_element_type=f32) * dmat
        xi = jnp.exp((row + 1).astype(f32) * lg)
        state = s_scr[h]
        o = (jnp.dot(inner.astype(bf16), vb, preferred_element_type=f32)
             + jnp.dot(qb, state.astype(bf16), preferred_element_type=f32) * xi)
        wk = jnp.exp((c_true - 1 - row).astype(f32) * lg)
        kw = (k * wk).astype(bf16)
        s_scr[h] = state * float(np.exp(np.float32(c_true * lg))) + lax.dot_general(
            kw, vb, _TN, preferred_element_type=f32)
        o = o[:c_true]
        mu = jnp.mean(o, axis=-1, keepdims=True)
        cen = o - mu
        var = jnp.mean(cen * cen, axis=-1, keepdims=True)
        y = cen * lax.rsqrt(var + EPS) * gn_ref[h]
        rg = rg_ref[:, h * RET_DV:(h + 1) * RET_DV]
        o_ref[:, h * RET_DV:(h + 1) * RET_DV] = (y * (rg * _sigmoid(rg))).astype(o_ref.dtype)

    @pl.when(c == pl.num_programs(1) - 1)
    def _():
        sout_ref[...] = s_scr[...]


def _retention(larr, z, s0, s0_per_layer, ret_gn, nb, nc, c_true, out_dtype):
    c_pad = max(c_true, RET_CHUNK)
    m = z.shape[0]
    r_qk = RET_HEADS * RET_DK
    r_v = RET_HEADS * RET_DV
    kern = functools.partial(_ret_kernel, c_true=c_true, c_pad=c_pad)
    return _call(
        kern, grid=(nb, nc),
        in_specs=[
            pl.BlockSpec((c_true, r_qk), lambda b, c, l: (b * nc + c, C_RQ // r_qk)),
            pl.BlockSpec((c_true, r_qk), lambda b, c, l: (b * nc + c, C_RK // r_qk)),
            pl.BlockSpec((c_true, r_v), lambda b, c, l: (b * nc + c, C_RV // r_v)),
            pl.BlockSpec((c_true, r_v), lambda b, c, l: (b * nc + c, C_RG // r_v)),
            pl.BlockSpec((None, RET_HEADS, RET_DK, RET_DV), lambda b, c, l: (l[0] * s0_per_layer + b, 0, 0, 0)),
            pl.BlockSpec((None, RET_HEADS, 1, RET_DV), lambda b, c, l: (l[0], 0, 0, 0)),
        ],
        out_specs=[
            pl.BlockSpec((c_true, r_v), lambda b, c, l: (b * nc + c, 0)),
            pl.BlockSpec((None, RET_HEADS, RET_DK, RET_DV), lambda b, c, l: (b, 0, 0, 0)),
        ],
        out_shape=[jax.ShapeDtypeStruct((m, r_v), out_dtype),
                   jax.ShapeDtypeStruct((nb, RET_HEADS, RET_DK, RET_DV), f32)],
        scratch=[pltpu.VMEM((RET_HEADS, RET_DK, RET_DV), f32)],
        name="retention",
    )(larr, z, z, z, z, s0, ret_gn)


def _cmp_kernel(*refs, n_in, n_scalar):
    x_refs = refs[n_scalar:n_scalar + n_in]
    w_ref, a_ref, b_ref = refs[n_scalar + n_in:]
    wa = w_ref[0:CMP_STRIDE]
    wb = w_ref[CMP_STRIDE:CMP_BLOCK]
    for k in range(n_in):
        x = x_refs[k][...]
        r = x.shape[0] // CMP_STRIDE
        x3 = x.reshape((r, CMP_STRIDE) + x.shape[1:])
        a_ref[k * r:(k + 1) * r] = jnp.sum(x3 * wa[None], axis=1)
        b_ref[k * r:(k + 1) * r] = jnp.sum(x3 * wb[None], axis=1)


def _compress_prompt(larr, z, w_cmp, rows):
    m = z.shape[0]
    wcols = 2 * N_KV_HEADS * HEAD_DIM
    kern = functools.partial(_cmp_kernel, n_in=1, n_scalar=1)
    shp = jax.ShapeDtypeStruct((m // CMP_STRIDE, wcols), f32)
    return _call(
        kern, grid=(m // rows,),
        in_specs=[pl.BlockSpec((rows, wcols), lambda i, l: (i, C_KC // wcols)),
                  pl.BlockSpec((None, CMP_BLOCK, wcols), lambda i, l: (l[0], 0, 0))],
        out_specs=[pl.BlockSpec((rows // CMP_STRIDE, wcols), lambda i, l: (i, 0))] * 2,
        out_shape=[shp, shp], name="compress_prompt",
    )(larr, z, w_cmp)


def _compress_pages(larr, page_table, cache, w_cmp, n_pool):
    db, n_pages = page_table.shape
    page, _, sg, d = cache.shape[1:]
    pps = CMP_PAGES_PER_STEP
    steps = n_pages // pps
    sub = page // CMP_STRIDE
    kern = functools.partial(_cmp_kernel, n_in=pps, n_scalar=2)

    def page_spec(k):
        return pl.BlockSpec((None, page, None, sg, d),
                            lambda b, n, l, pt: (l[0] * n_pool + pt[b, n * pps + k], 0, 0, 0, 0))

    shp = jax.ShapeDtypeStruct((db * n_pages * sub, sg, d), f32)
    return _call(
        kern, grid=(db, steps), nsp=2,
        in_specs=[page_spec(k) for k in range(pps)]
        + [pl.BlockSpec((None, CMP_BLOCK, sg, d), lambda b, n, l, pt: (l[0], 0, 0, 0))],
        out_specs=[pl.BlockSpec((pps * sub, sg, d), lambda b, n, l, pt: (b * steps + n, 0, 0))] * 2,
        out_shape=[shp, shp], name="compress_pages",
    )(larr, page_table, *([cache] * pps), w_cmp)


def _combine_cmp(a, b):
    n = a.shape[0]
    row = lax.broadcasted_iota(jnp.int32, (n, 1), 0)
    return a + jnp.where(row == n - 1, 0.0, pltpu.roll(b, n - 1, 0))


def _masked_exp(s, mask, exp=jnp.exp):
    s = jnp.where(mask, s, NEG)
    m = jnp.max(s, axis=-1, keepdims=True)
    e = jnp.where(mask, exp(s - m), 0.0)
    return e, jnp.sum(e, axis=-1, keepdims=True)


def _dot_hilo(p, m01):
    hi = p.astype(bf16)
    lo = (p - hi.astype(f32)).astype(bf16)
    return jnp.dot(hi, m01, preferred_element_type=f32) + jnp.dot(lo, m01, preferred_element_type=f32)


def _topk_mask(imp, k, axis=1):
    pos = lax.broadcasted_iota(jnp.int32, imp.shape, axis).astype(f32)
    sel = jnp.zeros(imp.shape, f32)
    for _ in range(k):
        m = jnp.max(imp, axis=axis, keepdims=True)
        idx = jnp.min(jnp.where(imp == m, pos, 1e9), axis=axis, keepdims=True)
        hit = pos == idx
        sel = jnp.where(hit, 1.0, sel)
        imp = jnp.where(hit, -jnp.inf, imp)
    return sel


def _block_importance_t(p_sum, mcs_t, tpos_row, n_sel):
    hi = p_sum.astype(bf16)
    lo = (p_sum - hi.astype(f32)).astype(bf16)
    imp = (lax.dot_general(mcs_t, hi, _NT, preferred_element_type=f32)
           + lax.dot_general(mcs_t, lo, _NT, preferred_element_type=f32))
    jj = lax.broadcasted_iota(jnp.int32, (imp.shape[0], 1), 0)
    forced = jnp.where(jj == 0, 1, jnp.where(jj == (tpos_row >> 6), 1, 0))
    imp = jnp.where(jj * SEL_BLOCK <= tpos_row, imp, NEG)
    imp = jnp.where(forced == 1, BIG, imp)
    return jnp.where(jj < n_sel, imp, -jnp.inf)


def _block_importance(p_sum, mcs, tpos, n_sel):
    imp = _dot_hilo(p_sum, mcs)
    jj = lax.broadcasted_iota(jnp.int32, (1, imp.shape[1]), 1)
    forced = jnp.where(jj == 0, 1, jnp.where(jj == (tpos >> 6), 1, 0))
    imp = jnp.where(jj * SEL_BLOCK <= tpos, imp, NEG)
    imp = jnp.where(forced == 1, BIG, imp)
    return jnp.where(jj < n_sel, imp, -jnp.inf)


def _tile_row(ref, j):
    n, s, d = ref.shape
    return ref.reshape(n * s, d)[pl.ds(j, n, stride=s), :]


def _gate(sg, idx):
    lane = lax.broadcasted_iota(jnp.int32, (1, sg.shape[1]), 1)
    return jnp.sum(jnp.where(lane == idx, sg, 0.0), axis=-1, keepdims=True)


def _softmax_pv(q, k, v, mask):
    s = jnp.where(mask, lax.dot_general(q, k, _NT, preferred_element_type=f32), NEG)
    e = jnp.exp2(s - jnp.max(s, axis=-1, keepdims=True))
    den = jnp.sum(e, axis=-1, keepdims=True)
    return jnp.dot(e.astype(bf16), v, preferred_element_type=f32) / den


def _nsa_prompt_kernel(l_ref, q_ref, ka_ref, kb_ref, va_ref, vb_ref, ks_ref, vs_ref, kw_ref, vw_ref, ag_ref,
                       mcs_ref, e_ref, o_ref, os_scr, *, tq, t_len, wq, wlen):
    g = pl.program_id(1)
    i = pl.program_id(2)
    t0 = i * tq
    tpos = t0 + lax.broadcasted_iota(jnp.int32, (tq, 1), 0)
    n_c = t_len // CMP_STRIDE
    n_sel = t_len // SEL_BLOCK

    kc = _combine_cmp(ka_ref[...], kb_ref[...]).astype(bf16)
    vc = _combine_cmp(va_ref[...], vb_ref[...]).astype(bf16)
    cend = lax.broadcasted_iota(jnp.int32, (1, n_c), 1) * CMP_STRIDE + (CMP_BLOCK - 1)
    mask_c = cend <= tpos

    qs = [(q_ref[:, r * HEAD_DIM:(r + 1) * HEAD_DIM] * (SCALE * LOG2E)).astype(bf16) for r in range(GROUP)]

    p_sum = jnp.zeros((tq, n_c), f32)
    o_c = []
    for r in range(GROUP):
        s = lax.dot_general(qs[r], kc, _NT, preferred_element_type=f32)
        e, den = _masked_exp(s, mask_c, jnp.exp2)
        p = e / jnp.maximum(den, 1e-30)
        p_sum = p_sum + p
        o_c.append(jnp.dot(p.astype(bf16), vc, preferred_element_type=f32))

    tpos_row = t0 + lax.broadcasted_iota(jnp.int32, (1, tq), 1)
    imp_t = _block_importance_t(p_sum, mcs_ref[...], tpos_row, n_sel)
    selb_t = _topk_mask(imp_t, min(N_SEL, n_sel), axis=0).astype(bf16)

    for br in range(t_len // tq):
        @pl.when(i == br)
        def _(br=br):
            klen = (br + 1) * tq
            selk = lax.dot_general(selb_t, e_ref[:, 0:klen], _TN, preferred_element_type=f32)
            kpos = lax.broadcasted_iota(jnp.int32, (1, klen), 1)
            mask_s = jnp.where(kpos <= tpos, selk, 0.0) > 0.5
            ks = ks_ref[0:klen, :].astype(bf16)
            vs = vs_ref[0:klen, :].astype(bf16)
            for r in range(GROUP):
                os_scr[:, r * HEAD_DIM:(r + 1) * HEAD_DIM] = _softmax_pv(qs[r], ks, vs, mask_s)

    sg = _sigmoid(ag_ref[...])
    gates = [[_gate(sg, (g * GROUP + r) * 3 + k) for k in range(3)] for r in range(GROUP)]
    for h in range(tq // wq):
        rs = slice(h * wq, (h + 1) * wq)
        start = pl.multiple_of(jnp.maximum(t0 + (h + 1) * wq - wlen, 0), wq)
        kw = kw_ref[pl.ds(start, wlen), :].astype(bf16)
        vw = vw_ref[pl.ds(start, wlen), :].astype(bf16)
        d = tpos[rs] - (start + lax.broadcasted_iota(jnp.int32, (1, wlen), 1))
        mask_w = jnp.where(d >= 0, d, WINDOW) < WINDOW
        for r in range(GROUP):
            cs = slice(r * HEAD_DIM, (r + 1) * HEAD_DIM)
            o_w = _softmax_pv(qs[r][rs], kw, vw, mask_w)
            out = gates[r][0][rs] * o_c[r][rs] + gates[r][1][rs] * os_scr[rs, cs] + gates[r][2][rs] * o_w
            o_ref[rs, cs] = out.astype(o_ref.dtype)


def _nsa_prompt(larr, z, cmp_a, cmp_b, mcs, emat, nb, t_len, tq):
    m = z.shape[0]
    nq = t_len // tq
    n_c = t_len // CMP_STRIDE
    wq = min(tq, 256)
    wlen = min(WINDOW + wq, t_len)
    gw = GROUP * HEAD_DIM
    kern = functools.partial(_nsa_prompt_kernel, tq=tq, t_len=t_len, wq=wq, wlen=wlen)

    def head_cols(c0):
        return pl.BlockSpec((t_len, HEAD_DIM), lambda b, g, i, l: (b, c0 // HEAD_DIM + g))

    def cmp_spec(off):
        return pl.BlockSpec((n_c, HEAD_DIM), lambda b, g, i, l: (b, off + g))

    return _call(
        kern, grid=(nb, N_KV_HEADS, nq),
        in_specs=[
            pl.BlockSpec((tq, gw), lambda b, g, i, l: (b * nq + i, C_AQ // gw + g)),
            cmp_spec(0), cmp_spec(0), cmp_spec(N_KV_HEADS), cmp_spec(N_KV_HEADS),
            head_cols(C_KS), head_cols(C_VS), head_cols(C_KW), head_cols(C_VW),
            pl.BlockSpec((tq, LANE), lambda b, g, i, l: (b * nq + i, C_AG // LANE)),
            pl.BlockSpec(mcs.shape, lambda b, g, i, l: (0, 0)),
            pl.BlockSpec(emat.shape, lambda b, g, i, l: (0, 0)),
        ],
        out_specs=pl.BlockSpec((tq, gw), lambda b, g, i, l: (b * nq + i, g)),
        out_shape=jax.ShapeDtypeStruct((m, N_HEADS * HEAD_DIM), bf16),
        scratch=[pltpu.VMEM((tq, gw), f32)],
        name="nsa_prompt",
    )(larr, z, cmp_a, cmp_b, cmp_a, cmp_b, z, z, z, z, z, mcs, emat)


def _rows_rt(ref, g_off, tn):
    return jnp.concatenate(
        [ref[:, (g_off + r) * HEAD_DIM:(g_off + r + 1) * HEAD_DIM] for r in range(GROUP)], axis=0)


def _nsa_s_cmp_kernel(l_ref, q_ref, a_ref, b_ref, mcs_ref, oc_ref, sel_ref, *, tn, past, n_sel):
    n_c = a_ref.shape[0]
    rows = GROUP * tn
    tpos_r = past + (lax.broadcasted_iota(jnp.int32, (rows, 1), 0) % tn)
    tpos = past + (lax.broadcasted_iota(jnp.int32, (2 * tn, 1), 0) % tn)
    cend = lax.broadcasted_iota(jnp.int32, (1, n_c), 1) * CMP_STRIDE + (CMP_BLOCK - 1)
    mask_c = cend <= tpos_r
    mcs = mcs_ref[...]
    for g in range(N_KV_HEADS):
        q = (_rows_rt(q_ref, g * GROUP, tn) * SCALE).astype(bf16)
        kc = _combine_cmp(_tile_row(a_ref, g), _tile_row(b_ref, g)).astype(bf16)
        vc = _combine_cmp(_tile_row(a_ref, N_KV_HEADS + g), _tile_row(b_ref, N_KV_HEADS + g)).astype(bf16)
        s = lax.dot_general(q, kc, _NT, preferred_element_type=f32)
        e, den = _masked_exp(s, mask_c)
        p = e / jnp.maximum(den, 1e-30)
        oc_ref[g] = jnp.dot(p.astype(bf16), vc, preferred_element_type=f32)
        p_sum = p[0:tn]
        for r in range(1, GROUP):
            p_sum = p_sum + p[r * tn:(r + 1) * tn]
        p_sum = jnp.concatenate([p_sum, jnp.zeros_like(p_sum)], axis=0)
        imp = _block_importance(p_sum, mcs, tpos, n_sel)
        sel_ref[g] = _topk_mask(imp, min(N_SEL, n_sel))[0:tn]


def _nsa_s_cmp(larr, z, cmp_a, cmp_b, mcs, db, tn, past, n_sel):
    n_c = cmp_a.shape[0] // db
    aq = N_HEADS * HEAD_DIM
    kern = functools.partial(_nsa_s_cmp_kernel, tn=tn, past=past, n_sel=n_sel)
    cmp_spec = pl.BlockSpec((n_c,) + cmp_a.shape[1:], lambda b, l: (b, 0, 0))
    return _call(
        kern, grid=(db,),
        in_specs=[pl.BlockSpec((tn, aq), lambda b, l: (b, C_AQ // aq)), cmp_spec, cmp_spec,
                  pl.BlockSpec(mcs.shape, lambda b, l: (0, 0))],
        out_specs=[pl.BlockSpec((None, N_KV_HEADS, GROUP * tn, HEAD_DIM), lambda b, l: (b, 0, 0, 0)),
                   pl.BlockSpec((None, N_KV_HEADS, tn, mcs.shape[1]), lambda b, l: (b, 0, 0, 0))],
        out_shape=[jax.ShapeDtypeStruct((db, N_KV_HEADS, GROUP * tn, HEAD_DIM), f32),
                   jax.ShapeDtypeStruct((db, N_KV_HEADS, tn, mcs.shape[1]), f32)],
        name="nsa_sample_cmp",
    )(larr, z, cmp_a, cmp_b, mcs)


def _nsa_s_sel_kernel(*refs, tn, pps):
    q_ref, sel_ref, e_ref = refs[2:5]
    page_refs = refs[5:5 + pps]
    acc_ref, m_ref, l_ref = refs[5 + pps:]
    n = pl.program_id(1)

    @pl.when(n == 0)
    def _():
        acc_ref[...] = jnp.zeros_like(acc_ref)
        m_ref[...] = jnp.full_like(m_ref, NEG)
        l_ref[...] = jnp.zeros_like(l_ref)

    emat = e_ref[...]
    for g in range(N_KV_HEADS):
        q = (_rows_rt(q_ref, g * GROUP, tn) * SCALE).astype(bf16)
        k_all = jnp.concatenate([_tile_row(pr, g).astype(bf16) for pr in page_refs], axis=0)
        v_all = jnp.concatenate([_tile_row(pr, N_KV_HEADS + g).astype(bf16) for pr in page_refs], axis=0)
        s = lax.dot_general(q, k_all, _NT, preferred_element_type=f32)
        selg = sel_ref[g]
        selg = jnp.concatenate([selg, jnp.zeros_like(selg)], axis=0).astype(bf16)
        selk = jnp.dot(selg, emat, preferred_element_type=f32)[0:tn]
        mask = jnp.concatenate([selk] * GROUP, axis=0) > 0.5
        s = jnp.where(mask, s, NEG)
        m_old = m_ref[g]
        m_new = jnp.maximum(m_old, jnp.max(s, axis=-1, keepdims=True))
        alpha = jnp.exp(m_old - m_new)
        e = jnp.where(mask, jnp.exp(s - m_new[:, 0:1]), 0.0)
        l_ref[g] = alpha * l_ref[g] + jnp.sum(e, axis=-1, keepdims=True)
        acc_ref[g] = alpha * acc_ref[g] + jnp.dot(e.astype(bf16), v_all, preferred_element_type=f32)
        m_ref[g] = m_new


def _nsa_s_sel(larr, page_table, z, sel, emat_pages, cache, n_pool, tn):
    db, n_pages = page_table.shape
    page, _, sg, d = cache.shape[1:]
    pps = PAGES_PER_STEP
    steps = n_pages // pps
    rows = GROUP * tn
    kern = functools.partial(_nsa_s_sel_kernel, tn=tn, pps=pps)

    def page_spec(k):
        return pl.BlockSpec((None, page, None, sg, d),
                            lambda b, n, l, pt: (l[0] * n_pool + pt[b, n * pps + k], 0, 1, 0, 0))

    st = jax.ShapeDtypeStruct((db, N_KV_HEADS, rows, HEAD_DIM), f32)
    st_spec = pl.BlockSpec((None, N_KV_HEADS, rows, HEAD_DIM), lambda b, n, l, pt: (b, 0, 0, 0))
    return _call(
        kern, grid=(db, steps), nsp=2,
        in_specs=[pl.BlockSpec((tn, N_HEADS * HEAD_DIM), lambda b, n, l, pt: (b, C_AQ // (N_HEADS * HEAD_DIM))),
                  pl.BlockSpec((None, None, N_KV_HEADS, tn, LANE), lambda b, n, l, pt: (b, n, 0, 0, 0)),
                  pl.BlockSpec(emat_pages.shape, lambda b, n, l, pt: (0, 0))]
        + [page_spec(k) for k in range(pps)],
        out_specs=[st_spec, st_spec, st_spec],
        out_shape=[st, st, st], name="nsa_sample_sel",
    )(larr, page_table, z, sel, emat_pages, *([cache] * pps))


def _nsa_s_fin_kernel(l_ref, q_ref, ksn_ref, vsn_ref, kwn_ref, vwn_ref, ag_ref, buf_ref,
                      acc_ref, m_ref, lsum_ref, oc_ref, sel_ref, o_ref, *, tn, past, n_sel):
    rows = GROUP * tn
    wc = buf_ref.shape[0]
    tpos = past + (lax.broadcasted_iota(jnp.int32, (rows, 1), 0) % tn)
    kidx = lax.broadcasted_iota(jnp.int32, (1, LANE), 1)
    widx = lax.broadcasted_iota(jnp.int32, (1, wc + LANE), 1)
    d = tpos - (past - wc + widx)
    mask_w = jnp.where(widx < wc + tn, jnp.where(d >= 0, d, WINDOW), WINDOW) < WINDOW
    sg = _sigmoid(ag_ref[...])

    def new_rows(ref, g):
        a = ref[:, g * HEAD_DIM:(g + 1) * HEAD_DIM]
        return jnp.concatenate([a, jnp.zeros((LANE - tn, HEAD_DIM), a.dtype)], axis=0)

    for g in range(N_KV_HEADS):
        q = (_rows_rt(q_ref, g * GROUP, tn) * SCALE).astype(bf16)

        ksn = new_rows(ksn_ref, g).astype(bf16)
        vsn = new_rows(vsn_ref, g).astype(bf16)
        s = lax.dot_general(q, ksn, _NT, preferred_element_type=f32)
        last_sel = sel_ref[g][:, n_sel - 1:n_sel]
        last_sel = jnp.concatenate([last_sel] * GROUP, axis=0)
        mask = jnp.where(kidx < tn, jnp.where(past + kidx <= tpos, last_sel, 0.0), 0.0) > 0.5
        s = jnp.where(mask, s, NEG)
        m_old = m_ref[g]
        m_new = jnp.maximum(m_old, jnp.max(s, axis=-1, keepdims=True))
        alpha = jnp.exp(m_old - m_new)
        e = jnp.where(mask, jnp.exp(s - m_new[:, 0:1]), 0.0)
        den = alpha * lsum_ref[g] + jnp.sum(e, axis=-1, keepdims=True)
        o_s = (alpha * acc_ref[g] + jnp.dot(e.astype(bf16), vsn, preferred_element_type=f32)) / jnp.maximum(den, 1e-30)

        kw = jnp.concatenate([_tile_row(buf_ref, g), new_rows(kwn_ref, g)], axis=0).astype(bf16)
        vw = jnp.concatenate([_tile_row(buf_ref, N_KV_HEADS + g), new_rows(vwn_ref, g)], axis=0).astype(bf16)
        s = lax.dot_general(q, kw, _NT, preferred_element_type=f32)
        e, den = _masked_exp(s, mask_w)
        o_w = jnp.dot(e.astype(bf16), vw, preferred_element_type=f32) / jnp.maximum(den, 1e-30)

        o_c = oc_ref[g]
        for r in range(GROUP):
            base = (g * GROUP + r) * 3
            sl = slice(r * tn, (r + 1) * tn)
            out = _gate(sg, base) * o_c[sl] + _gate(sg, base + 1) * o_s[sl] + _gate(sg, base + 2) * o_w[sl]
            o_ref[:, (g * GROUP + r) * HEAD_DIM:(g * GROUP + r + 1) * HEAD_DIM] = out


def _nsa_s_fin(larr, z, win_buf, acc, mx, lsum, o_c, sel, db, tn, past, n_sel):
    aq = N_HEADS * HEAD_DIM
    kvw = N_KV_HEADS * HEAD_DIM
    rows = GROUP * tn
    kern = functools.partial(_nsa_s_fin_kernel, tn=tn, past=past, n_sel=n_sel)

    def new_cols(c0):
        return pl.BlockSpec((tn, kvw), lambda b, l: (b, c0 // kvw))

    st_spec = pl.BlockSpec((None, N_KV_HEADS, rows, HEAD_DIM), lambda b, l: (b, 0, 0, 0))
    return _call(
        kern, grid=(db,),
        in_specs=[pl.BlockSpec((tn, aq), lambda b, l: (b, C_AQ // aq)),
                  new_cols(C_KS), new_cols(C_VS), new_cols(C_KW), new_cols(C_VW),
                  pl.BlockSpec((tn, LANE), lambda b, l: (b, C_AG // LANE)),
                  pl.BlockSpec((None,) + win_buf.shape[1:], lambda b, l: (l[0] * db + b, 0, 0, 0)),
                  st_spec, st_spec, st_spec, st_spec,
                  pl.BlockSpec((None, N_KV_HEADS, tn, sel.shape[-1]), lambda b, l: (b, 0, 0, 0))],
        out_specs=pl.BlockSpec((tn, aq), lambda b, l: (b, 0)),
        out_shape=jax.ShapeDtypeStruct((db * tn, aq), f32),
        name="nsa_sample_fin",
    )(larr, z, z, z, z, z, z, win_buf, acc, mx, lsum, o_c, sel)


def _merge_kernel(l_ref, x_ref, or_ref, oa_ref, ga_ref, gb_ref, wpa_ref, wpb_ref, wo_ref, o_ref, acc_ref):
    j = pl.program_id(1)

    @pl.when(j == 0)
    def _():
        acc_ref[...] = jnp.zeros_like(acc_ref)

    pa = jnp.dot(or_ref[...], wpa_ref[...], preferred_element_type=f32)
    pb = jnp.dot(oa_ref[...], wpb_ref[...], preferred_element_type=f32)
    mix = _sigmoid(ga_ref[...]) * pa + _sigmoid(gb_ref[...]) * pb
    acc_ref[...] += jnp.dot(mix.astype(bf16), wo_ref[...], preferred_element_type=f32)

    @pl.when(j == pl.num_programs(1) - 1)
    def _():
        o_ref[...] = x_ref[...] + acc_ref[...]


def _merge(larr, x, o_r, o_a, z, w_pa, w_pb, w_out, tm):
    m = x.shape[0]
    r_v = RET_HEADS * RET_DV
    a_q = N_HEADS * HEAD_DIM
    return _call(
        _merge_kernel, grid=(m // tm, D_MODEL // TN),
        in_specs=[
            pl.BlockSpec((tm, D_MODEL), lambda i, j, l: (i, 0)),
            pl.BlockSpec((tm, r_v), lambda i, j, l: (i, 0)),
            pl.BlockSpec((tm, a_q), lambda i, j, l: (i, 0)),
            pl.BlockSpec((tm, TN), lambda i, j, l: (i, C_GA // TN + j)),
            pl.BlockSpec((tm, TN), lambda i, j, l: (i, C_GB // TN + j)),
            pl.BlockSpec((None, r_v, TN), lambda i, j, l: (l[0], 0, j)),
            pl.BlockSpec((None, a_q, TN), lambda i, j, l: (l[0], 0, j)),
            pl.BlockSpec((None, TN, D_MODEL), lambda i, j, l: (l[0], j, 0)),
        ],
        out_specs=pl.BlockSpec((tm, D_MODEL), lambda i, j, l: (i, 0)),
        out_shape=jax.ShapeDtypeStruct((m, D_MODEL), f32),
        scratch=[pltpu.VMEM((tm, D_MODEL), f32)],
        name="merge",
    )(larr, x, o_r, o_a, z, z, w_pa, w_pb, w_out)


def _rope_tables(pos):
    half = HEAD_DIM // 2
    inv = 1.0 / (ROPE_THETA ** (jnp.arange(half, dtype=f32) / half))
    ang = pos.astype(f32)[:, None] * inv[None, :]
    cos, sin = jnp.cos(ang), jnp.sin(ang)
    return jnp.concatenate([cos, cos], axis=-1), jnp.concatenate([-sin, sin], axis=-1)


def _cmp_to_sel_table(n_c_valid, n_sel, rows, cols):
    cs = np.arange(n_c_valid) * CMP_STRIDE
    ce = cs + CMP_BLOCK - 1
    js = np.arange(n_sel) * SEL_BLOCK
    je = js + SEL_BLOCK - 1
    tab = np.zeros((rows, cols), np.float32)
    tab[:n_c_valid, :n_sel] = (cs[:, None] <= je[None, :]) & (ce[:, None] >= js[None, :])
    return jnp.asarray(tab, dtype=bf16)


def _expand_table(rows, n_keys, first_block=0):
    tab = np.zeros((rows, n_keys), np.float32)
    s = np.arange(n_keys)
    tab[first_block + s // SEL_BLOCK, s] = 1.0
    return tab


def _round_up(a, b):
    return -(-a // b) * b


def kernel(x_prompt, x_sample, cache_kv, state_win, state_ret, page_table, norm_gain, ffn_gate,
           ffn_up, ffn_down, w_in, qk_norm, cmp_w, ret_gn, w_pa, w_pb, w_out):
    nb, t_len, _ = x_prompt.shape
    db, tn, _ = x_sample.shape
    depth, n_pool, page = cache_kv.shape[:3]
    n_pages = page_table.shape[1]
    past = n_pages * page
    wc = state_win.shape[2]
    kvw = N_KV_HEADS * HEAD_DIM
    assert t_len % 512 == 0 and t_len >= WINDOW and wc == WINDOW and tn == 8
    assert n_pages % PAGES_PER_STEP == 0 and n_pages % CMP_PAGES_PER_STEP == 0
    assert past % SEL_BLOCK == 0 and tn <= CMP_STRIDE

    wg, wu, wd = ffn_gate.astype(bf16), ffn_up.astype(bf16), ffn_down.astype(bf16)
    w_in_p = jnp.pad(w_in.astype(bf16), ((0, 0), (0, 0), (0, DZ - w_in.shape[-1])))
    w_in_p = lax.dynamic_update_slice(w_in_p, w_in[..., W_IN_SPLIT:].astype(bf16), (0, 0, C_GA))
    w_pa_b, w_pb_b, w_out_b = w_pa.astype(bf16), w_pb.astype(bf16), w_out.astype(bf16)

    gains = norm_gain.reshape(depth, 3, 1, D_MODEL)
    ones = jnp.ones((depth, HEAD_DIM), f32)
    tile_rows = []
    for j in range(N_ZT):
        c0 = j * TN
        if c0 == C_KC:
            tile_rows.append(qk_norm[:, 1])
        elif c0 == C_KS:
            tile_rows.append(qk_norm[:, 2])
        elif c0 == C_KW:
            tile_rows.append(qk_norm[:, 3])
        elif C_AQ <= c0 < C_KC:
            tile_rows.append(qk_norm[:, 0])
        elif C_RK <= c0 < C_RV:
            tile_rows.append(ones * (RET_DK ** -0.5))
        else:
            tile_rows.append(ones)
    tile_gain = jnp.stack(tile_rows, axis=1).reshape(depth, N_ZT, 1, HEAD_DIM)
    modes = jnp.asarray(_TILE_MODE, jnp.int32)

    w_cmp = jnp.repeat(cmp_w, HEAD_DIM, axis=-1)
    w_cmp = w_cmp.transpose(0, 2, 1, 3).reshape(depth, CMP_BLOCK, 2 * kvw)
    w_cmp_rows = w_cmp.reshape(depth, CMP_BLOCK, 2 * N_KV_HEADS, HEAD_DIM)
    gn = ret_gn.reshape(depth, RET_HEADS, 1, RET_DV)

    cos_p, sin_p = _rope_tables(jnp.arange(t_len, dtype=jnp.int32))
    cos_s, sin_s = _rope_tables(jnp.tile(past + jnp.arange(tn, dtype=jnp.int32), db))

    n_c_p = t_len // CMP_STRIDE
    n_sel_p = t_len // SEL_BLOCK
    mcs_p = _cmp_to_sel_table((t_len - CMP_BLOCK) // CMP_STRIDE + 1, n_sel_p, n_c_p, n_sel_p).T
    emat_p = jnp.asarray(_expand_table(n_sel_p, t_len), dtype=bf16)
    l_full = past + tn
    n_sel_s = -(-l_full // SEL_BLOCK)
    n_c_s = past // CMP_STRIDE
    sel_w = _round_up(n_sel_s, LANE)
    mcs_s = _cmp_to_sel_table((l_full - CMP_BLOCK) // CMP_STRIDE + 1, n_sel_s, n_c_s, sel_w)
    steps = n_pages // PAGES_PER_STEP
    keys_per_step = PAGES_PER_STEP * page
    blocks_per_step = keys_per_step // SEL_BLOCK
    emat_s = jnp.asarray(_expand_table(LANE, keys_per_step), dtype=bf16)

    cache2 = cache_kv.reshape(depth * n_pool, page, 2, 2 * N_KV_HEADS, HEAD_DIM)
    win2 = state_win.reshape(depth * db, wc, 2 * N_KV_HEADS, HEAD_DIM)
    sret2 = state_ret.reshape(depth * db, RET_HEADS, RET_DK, RET_DV)
    zero_state = jnp.zeros((nb, RET_HEADS, RET_DK, RET_DV), f32)

    mp = nb * t_len
    ms = db * tn
    tm_p = 512
    rb = 512
    keep = min(WINDOW, t_len)
    tm_in = 1024
    tq = 512
    c_p = math.gcd(t_len, RET_CHUNK)

    def layer(carry, l):
        xp, xs, kv_buf, win_buf = carry
        larr = jnp.reshape(l, (1,)).astype(jnp.int32)

        xp = _ffn(larr, xp, gains, 0, wg, wu, wd, 0, tm_p)
        xs = _ffn(larr, xs, gains, 0, wg, wu, wd, 0, ms)

        zp = _inproj(larr, modes, xp, gains, w_in_p, tile_gain, cos_p, sin_p, min(tm_in, t_len))
        o_r, s_fin = _retention(larr, zp, zero_state, 0, gn, nb, t_len // c_p, c_p, bf16)
        ca, cb = _compress_prompt(larr, zp, w_cmp, 512)
        o_a = _nsa_prompt(larr, zp, ca, cb, mcs_p, emat_p, nb, t_len, tq)
        xp = _merge(larr, xp, o_r, o_a, zp, w_pa_b, w_pb_b, w_out_b, tm_p)
        kv_buf = _kv_rows(larr, zp, kv_buf, rb, C_KC, 4 * kvw, mp // rb, lambda i: i)
        win_buf = _kv_rows(larr, zp, win_buf, rb, C_KW, 2 * kvw, nb * (keep // rb),
                           lambda i: (i // (keep // rb)) * (t_len // rb) + (t_len - keep) // rb + i % (keep // rb))

        zs = _inproj(larr, modes, xs, gains, w_in_p, tile_gain, cos_s, sin_s, ms)
        o_rs, s_new = _retention(larr, zs, sret2, db, gn, db, 1, tn, f32)
        sa, sb = _compress_pages(larr, page_table, cache2, w_cmp_rows, n_pool)
        o_c, sel = _nsa_s_cmp(larr, zs, sa, sb, mcs_s, db, tn, past, n_sel_s)
        sel_steps = sel[..., :steps * blocks_per_step].reshape(db, N_KV_HEADS, tn, steps, blocks_per_step)
        sel_steps = jnp.pad(sel_steps.transpose(0, 3, 1, 2, 4), ((0, 0),) * 4 + ((0, LANE - blocks_per_step),))
        acc, mx, lsum = _nsa_s_sel(larr, page_table, zs, sel_steps, emat_s, cache2, n_pool, tn)
        o_as = _nsa_s_fin(larr, zs, win2, acc, mx, lsum, o_c, sel, db, tn, past, n_sel_s)
        xs = _merge(larr, xs, o_rs.astype(bf16), o_as.astype(bf16), zs, w_pa_b, w_pb_b, w_out_b, ms)
        kv_s = zs[:, C_KC:C_KW].reshape(db, tn, 4, N_KV_HEADS, HEAD_DIM)
        win_new = zs[:, C_KW:C_AG].reshape(db, tn, 2, N_KV_HEADS, HEAD_DIM)
        win_old = lax.dynamic_index_in_dim(state_win, l, 0, keepdims=False)
        win_s = jnp.concatenate([win_old[:, tn:], win_new], axis=1)

        xp = _ffn(larr, xp, gains, 2, wg, wu, wd, 1, tm_p)
        xs = _ffn(larr, xs, gains, 2, wg, wu, wd, 1, ms)
        return (xp, xs, kv_buf, win_buf), (kv_s, win_s, s_fin, s_new)

    kv_buf0 = jnp.zeros((depth, mp * 4 * kvw // LANE, LANE), f32)
    win_buf0 = jnp.zeros((depth, nb * keep * 2 * kvw // LANE, LANE), f32)
    (xp, xs, kv_buf, win_buf), outs = lax.scan(
        layer, (x_prompt.reshape(mp, D_MODEL), x_sample.reshape(ms, D_MODEL), kv_buf0, win_buf0),
        jnp.arange(depth, dtype=jnp.int32))
    kv_s, win_s, ret_p, ret_s = outs
    kv_p = kv_buf.reshape(depth, nb, t_len, 4, N_KV_HEADS, HEAD_DIM)
    win_p = win_buf.reshape(depth, nb, keep, 2, N_KV_HEADS, HEAD_DIM)
    return (xp.reshape(nb, t_len, D_MODEL), xs.reshape(db, tn, D_MODEL), kv_p, kv_s, win_p, win_s, ret_p, ret_s)
```

```python
import functools
import math

import jax
import jax.numpy as jnp
import numpy as np
from jax import lax
from jax.experimental import pallas as pl
from jax.experimental.pallas import tpu as pltpu

D_MODEL = 2048
D_FF = 5504
RET_HEADS = 8
RET_DK = 128
RET_DV = 256
RET_CHUNK = 128
N_HEADS = 16
N_KV_HEADS = 4
HEAD_DIM = 128
GROUP = N_HEADS // N_KV_HEADS
CMP_BLOCK = 32
CMP_STRIDE = 16
SEL_BLOCK = 64
N_SEL = 16
WINDOW = 512
ROPE_THETA = 10000.0
EPS = 1e-6
NEG = -1e30
BIG = 1e30
SCALE = HEAD_DIM ** -0.5
LOG2E = math.log2(math.e)

LANE = 128
VMEM_LIMIT = 56 * 1024 * 1024

TN = 512
C_RQ, C_RK, C_RV, C_RG = 0, 1024, 2048, 4096
C_AQ = 6144
C_KC, C_VC, C_KS, C_VS, C_KW, C_VW = 8192, 8704, 9216, 9728, 10240, 10752
C_AG = 11264
C_GA, C_GB = 11776, 13824
DZ = 15872
N_ZT = DZ // TN
N_HEAD_TILES = C_GA // TN
W_IN_SPLIT = 11312
_TILE_MODE = [1, 1, 1, 1] + [0] * 8 + [2, 2, 2, 2] + [2, 0, 2, 0, 2, 0] + [0] * 9

TF = 512
F_TILES = -(-D_FF // TF)
PAGES_PER_STEP = 8
CMP_PAGES_PER_STEP = 16

_RET_LOG_G = [float(np.log(np.float32(1.0) - np.float32(2.0) ** np.float32(-5.0 - h))) for h in range(RET_HEADS)]

_NT = (((1,), (1,)), ((), ()))
_TN = (((0,), (0,)), ((), ()))

bf16 = jnp.bfloat16
f32 = jnp.float32


def _sigmoid(x):
    return 1.0 / (1.0 + jnp.exp(-x))


def _call(kernel, *, grid, in_specs, out_specs, out_shape, scratch=(), nsp=1, sem=None, name=None, aliases=None):
    return pl.pallas_call(
        kernel,
        grid_spec=pltpu.PrefetchScalarGridSpec(num_scalar_prefetch=nsp, grid=grid, in_specs=in_specs,
                                               out_specs=out_specs, scratch_shapes=list(scratch)),
        out_shape=out_shape,
        compiler_params=pltpu.CompilerParams(dimension_semantics=sem or ("arbitrary",) * len(grid),
                                             vmem_limit_bytes=VMEM_LIMIT),
        input_output_aliases=aliases or {},
        name=name,
    )


def _ffn_kernel(l_ref, x_ref, g_ref, wg_ref, wu_ref, wd_ref, o_ref, h_ref, acc_ref):
    f = pl.program_id(1)

    @pl.when(f == 0)
    def _():
        x = x_ref[...]
        ms = jnp.mean(x * x, axis=-1, keepdims=True)
        h_ref[...] = (x * lax.rsqrt(ms + EPS) * g_ref[...]).astype(bf16)
        acc_ref[...] = jnp.zeros_like(acc_ref)

    def accumulate(width):
        h = h_ref[...]
        a = jnp.dot(h, wg_ref[:, 0:width], preferred_element_type=f32)
        b = jnp.dot(h, wu_ref[:, 0:width], preferred_element_type=f32)
        s = (a * _sigmoid(a)) * b
        acc_ref[...] += jnp.dot(s.astype(bf16), wd_ref[0:width, :], preferred_element_type=f32)

    last = pl.num_programs(1) - 1

    @pl.when(f < last)
    def _():
        accumulate(TF)

    @pl.when(f == last)
    def _():
        accumulate(D_FF - (F_TILES - 1) * TF)
        o_ref[...] = x_ref[...] + 0.5 * acc_ref[...]


def _ffn(larr, x, gains, which_gain, wg, wu, wd, which_w, tm):
    m = x.shape[0]
    grid = (m // tm, F_TILES)
    return _call(
        _ffn_kernel, grid=grid,
        in_specs=[
            pl.BlockSpec((tm, D_MODEL), lambda i, f, l: (i, 0)),
            pl.BlockSpec((None, None, 1, D_MODEL), lambda i, f, l: (l[0], which_gain, 0, 0)),
            pl.BlockSpec((None, None, D_MODEL, TF), lambda i, f, l: (l[0], which_w, 0, f)),
            pl.BlockSpec((None, None, D_MODEL, TF), lambda i, f, l: (l[0], which_w, 0, f)),
            pl.BlockSpec((None, None, TF, D_MODEL), lambda i, f, l: (l[0], which_w, f, 0)),
        ],
        out_specs=pl.BlockSpec((tm, D_MODEL), lambda i, f, l: (i, 0)),
        out_shape=jax.ShapeDtypeStruct((m, D_MODEL), f32),
        scratch=[pltpu.VMEM((tm, D_MODEL), bf16), pltpu.VMEM((tm, D_MODEL), f32)],
        name="ffn",
    )(larr, x, gains, wg, wu, wd)


def _inproj_kernel(l_ref, mode_ref, x_ref, g_ref, w_ref, wt_ref, gain_ref, cos_ref, sin_ref, o_ref, h_ref,
                   acc_a, acc_b):
    j = pl.program_id(1)
    n_tiles = pl.num_programs(1) - 1
    mode = mode_ref[jnp.maximum(j - 1, 0)]

    def matmul(acc, w):
        acc[...] = jnp.dot(h_ref[...], w[...], preferred_element_type=f32)

    def finish_plain(acc):
        o_ref[...] = acc[...]

    def finish_rotary(acc):
        cos = cos_ref[...]
        sin = sin_ref[...]
        gain = gain_ref[...]
        for hd in range(TN // HEAD_DIM):
            y = acc[:, hd * HEAD_DIM:(hd + 1) * HEAD_DIM]
            ms = jnp.mean(y * y, axis=-1, keepdims=True)
            inv = jnp.where(mode == 2, lax.rsqrt(ms + EPS), 1.0)
            y = y * inv * gain
            o_ref[:, hd * HEAD_DIM:(hd + 1) * HEAD_DIM] = y * cos + pltpu.roll(y, HEAD_DIM // 2, 1) * sin

    @pl.when(j == 0)
    def _():
        x = x_ref[...]
        ms = jnp.mean(x * x, axis=-1, keepdims=True)
        h_ref[...] = (x * lax.rsqrt(ms + EPS) * g_ref[...]).astype(bf16)
        matmul(acc_a, w_ref)

    for parity, (cur, prev) in enumerate(((acc_a, acc_b), (acc_b, acc_a))):
        mid = (j > 0) & (j < n_tiles) & ((j & 1) == parity)

        @pl.when(mid & (mode == 0) & (j < N_HEAD_TILES))
        def _(cur=cur, prev=prev):
            matmul(cur, w_ref)
            finish_plain(prev)

        @pl.when(mid & (mode == 0) & (j >= N_HEAD_TILES))
        def _(cur=cur, prev=prev):
            matmul(cur, wt_ref)
            finish_plain(prev)

        @pl.when(mid & (mode != 0))
        def _(cur=cur, prev=prev):
            matmul(cur, w_ref)
            finish_rotary(prev)

    @pl.when(j == n_tiles)
    def _():
        finish_plain(acc_a if N_ZT % 2 == 1 else acc_b)


def _inproj(larr, modes, x, gains, w_head, w_tail, tile_gain, cos2, sin2, tm):
    assert _TILE_MODE[-1] == 0 and all(md == 0 for md in _TILE_MODE[N_HEAD_TILES - 1:])
    assert w_head.shape[-1] >= N_HEAD_TILES * TN and w_tail.shape[-1] == (N_ZT - N_HEAD_TILES) * TN
    m = x.shape[0]
    nt = cos2.shape[0] // tm
    grid = (m // tm, N_ZT + 1)
    return _call(
        _inproj_kernel, grid=grid, nsp=2,
        in_specs=[
            pl.BlockSpec((tm, D_MODEL), lambda i, j, l, md: (i, 0)),
            pl.BlockSpec((None, None, 1, D_MODEL), lambda i, j, l, md: (l[0], 1, 0, 0)),
            pl.BlockSpec((None, D_MODEL, TN), lambda i, j, l, md: (l[0], 0, jnp.minimum(j, N_HEAD_TILES - 1))),
            pl.BlockSpec((None, D_MODEL, TN),
                         lambda i, j, l, md: (l[0], 0, jnp.clip(j - N_HEAD_TILES, 0, N_ZT - N_HEAD_TILES - 1))),
            pl.BlockSpec((None, None, 1, HEAD_DIM), lambda i, j, l, md: (l[0], jnp.maximum(j - 1, 0), 0, 0)),
            pl.BlockSpec((tm, HEAD_DIM), lambda i, j, l, md: (i % nt, 0)),
            pl.BlockSpec((tm, HEAD_DIM), lambda i, j, l, md: (i % nt, 0)),
        ],
        out_specs=pl.BlockSpec((tm, TN), lambda i, j, l, md: (i, jnp.maximum(j - 1, 0))),
        out_shape=jax.ShapeDtypeStruct((m, DZ), f32),
        scratch=[pltpu.VMEM((tm, D_MODEL), bf16), pltpu.VMEM((tm, TN), f32), pltpu.VMEM((tm, TN), f32)],
        name="inproj",
    )(larr, modes, x, gains, w_head, w_tail, tile_gain, cos2, sin2)


def _kv_rows_kernel(l_ref, z_ref, buf_ref, o_ref):
    del buf_ref
    rows = z_ref.shape[0]
    n = z_ref.shape[1] // LANE
    for c in range(n):
        o_ref[pl.ds(c, rows, stride=n), :] = z_ref[:, c * LANE:(c + 1) * LANE]


def _kv_rows(larr, z, buf, rows, col0, width, n_blocks, in_block):
    n = width // LANE
    return _call(
        _kv_rows_kernel, grid=(n_blocks,),
        in_specs=[pl.BlockSpec((rows, width), lambda i, l: (in_block(i), col0 // width)),
                  pl.BlockSpec(memory_space=pl.ANY)],
        out_specs=pl.BlockSpec((None, rows * n, LANE), lambda i, l: (l[0], i, 0)),
        out_shape=jax.ShapeDtypeStruct(buf.shape, buf.dtype),
        aliases={2: 0}, name="kv_rows",
    )(larr, z, buf)


def _ret_kernel(l_ref, q_ref, k_ref, v_ref, rg_ref, s0_ref, gn_ref, o_ref, sout_ref, s_scr, *, c_true, c_pad):
    c = pl.program_id(1)

    @pl.when(c == 0)
    def _():
        s_scr[...] = s0_ref[...]

    ri = lax.broadcasted_iota(jnp.int32, (c_pad, c_pad), 0)
    ci = lax.broadcasted_iota(jnp.int32, (c_pad, c_pad), 1)
    diff = ri - ci
    row = lax.broadcasted_iota(jnp.int32, (c_pad, 1), 0)

    def padded(a):
        if c_pad == c_true:
            return a
        return jnp.concatenate([a, jnp.zeros((c_pad - c_true, a.shape[1]), a.dtype)], axis=0)

    for h in range(RET_HEADS):
        lg = _RET_LOG_G[h]
        dmat = jnp.where(diff >= 0, jnp.exp(jnp.maximum(diff, 0).astype(f32) * lg), 0.0)
        q = padded(q_ref[:, h * RET_DK:(h + 1) * RET_DK])
        k = padded(k_ref[:, h * RET_DK:(h + 1) * RET_DK])
        v = padded(v_ref[:, h * RET_DV:(h + 1) * RET_DV])
        qb, kb, vb = q.astype(bf16), k.astype(bf16), v.astype(bf16)
        inner = lax.dot_general(qb, kb, _NT, preferred_element_type=f32) * dmat
        xi = jnp.exp((row + 1).astype(f32) * lg)
        state = s_scr[h]
        o = (jnp.dot(inner.astype(bf16), vb, preferred_element_type=f32)
             + jnp.dot(qb, state.astype(bf16), preferred_element_type=f32) * xi)
        wk = jnp.exp((c_true - 1 - row).astype(f32) * lg)
        kw = (k * wk).astype(bf16)
        s_scr[h] = state * float(np.exp(np.float32(c_true * lg))) + lax.dot_general(
            kw, vb, _TN, preferred_element_type=f32)
        o = o[:c_true]
        mu = jnp.mean(o, axis=-1, keepdims=True)
        cen = o - mu
        var = jnp.mean(cen * cen, axis=-1, keepdims=True)
        y = cen * lax.rsqrt(var + EPS) * gn_ref[h]
        rg = rg_ref[:, h * RET_DV:(h + 1) * RET_DV]
        o_ref[:, h * RET_DV:(h + 1) * RET_DV] = (y * (rg * _sigmoid(rg))).astype(o_ref.dtype)

    @pl.when(c == pl.num_programs(1) - 1)
    def _():
        sout_ref[...] = s_scr[...]


def _retention(larr, z, s0, s0_per_layer, ret_gn, nb, nc, c_true, out_dtype):
    c_pad = max(c_true, RET_CHUNK)
    m = z.shape[0]
    r_qk = RET_HEADS * RET_DK
    r_v = RET_HEADS * RET_DV
    kern = functools.partial(_ret_kernel, c_true=c_true, c_pad=c_pad)
    return _call(
        kern, grid=(nb, nc),
        in_specs=[
            pl.BlockSpec((c_true, r_qk), lambda b, c, l: (b * nc + c, C_RQ // r_qk)),
            pl.BlockSpec((c_true, r_qk), lambda b, c, l: (b * nc + c, C_RK // r_qk)),
            pl.BlockSpec((c_true, r_v), lambda b, c, l: (b * nc + c, C_RV // r_v)),
            pl.BlockSpec((c_true, r_v), lambda b, c, l: (b * nc + c, C_RG // r_v)),
            pl.BlockSpec((None, RET_HEADS, RET_DK, RET_DV), lambda b, c, l: (l[0] * s0_per_layer + b, 0, 0, 0)),
            pl.BlockSpec((None, RET_HEADS, 1, RET_DV), lambda b, c, l: (l[0], 0, 0, 0)),
        ],
        out_specs=[
            pl.BlockSpec((c_true, r_v), lambda b, c, l: (b * nc + c, 0)),
            pl.BlockSpec((None, RET_HEADS, RET_DK, RET_DV), lambda b, c, l: (b, 0, 0, 0)),
        ],
        out_shape=[jax.ShapeDtypeStruct((m, r_v), out_dtype),
                   jax.ShapeDtypeStruct((nb, RET_HEADS, RET_DK, RET_DV), f32)],
        scratch=[pltpu.VMEM((RET_HEADS, RET_DK, RET_DV), f32)],
        name="retention",
    )(larr, z, z, z, z, s0, ret_gn)


def _cmp_kernel(*refs, n_in, n_scalar):
    x_refs = refs[n_scalar:n_scalar + n_in]
    w_ref, a_ref, b_ref = refs[n_scalar + n_in:]
    wa = w_ref[0:CMP_STRIDE]
    wb = w_ref[CMP_STRIDE:CMP_BLOCK]
    for k in range(n_in):
        x = x_refs[k][...]
        r = x.shape[0] // CMP_STRIDE
        x3 = x.reshape((r, CMP_STRIDE) + x.shape[1:])
        a_ref[k * r:(k + 1) * r] = jnp.sum(x3 * wa[None], axis=1)
        b_ref[k * r:(k + 1) * r] = jnp.sum(x3 * wb[None], axis=1)


def _compress_prompt(larr, z, w_cmp, rows):
    m = z.shape[0]
    wcols = 2 * N_KV_HEADS * HEAD_DIM
    kern = functools.partial(_cmp_kernel, n_in=1, n_scalar=1)
    shp = jax.ShapeDtypeStruct((m // CMP_STRIDE, wcols), f32)
    return _call(
        kern, grid=(m // rows,),
        in_specs=[pl.BlockSpec((rows, wcols), lambda i, l: (i, C_KC // wcols)),
                  pl.BlockSpec((None, CMP_BLOCK, wcols), lambda i, l: (l[0], 0, 0))],
        out_specs=[pl.BlockSpec((rows // CMP_STRIDE, wcols), lambda i, l: (i, 0))] * 2,
        out_shape=[shp, shp], name="compress_prompt",
    )(larr, z, w_cmp)


def _compress_pages(larr, page_table, cache, w_cmp, n_pool):
    db, n_pages = page_table.shape
    page, _, sg, d = cache.shape[1:]
    pps = CMP_PAGES_PER_STEP
    steps = n_pages // pps
    sub = page // CMP_STRIDE
    kern = functools.partial(_cmp_kernel, n_in=pps, n_scalar=2)

    def page_spec(k):
        return pl.BlockSpec((None, page, None, sg, d),
                            lambda b, n, l, pt: (l[0] * n_pool + pt[b, n * pps + k], 0, 0, 0, 0))

    shp = jax.ShapeDtypeStruct((db * n_pages * sub, sg, d), f32)
    return _call(
        kern, grid=(db, steps), nsp=2,
        in_specs=[page_spec(k) for k in range(pps)]
        + [pl.BlockSpec((None, CMP_BLOCK, sg, d), lambda b, n, l, pt: (l[0], 0, 0, 0))],
        out_specs=[pl.BlockSpec((pps * sub, sg, d), lambda b, n, l, pt: (b * steps + n, 0, 0))] * 2,
        out_shape=[shp, shp], name="compress_pages",
    )(larr, page_table, *([cache] * pps), w_cmp)


def _combine_cmp(a, b):
    n = a.shape[0]
    row = lax.broadcasted_iota(jnp.int32, (n, 1), 0)
    return a + jnp.where(row == n - 1, 0.0, pltpu.roll(b, n - 1, 0))


def _masked_exp(s, mask, exp=jnp.exp):
    s = jnp.where(mask, s, NEG)
    m = jnp.max(s, axis=-1, keepdims=True)
    e = jnp.where(mask, exp(s - m), 0.0)
    return e, jnp.sum(e, axis=-1, keepdims=True)


def _dot_hilo(p, m01):
    hi = p.astype(bf16)
    lo = (p - hi.astype(f32)).astype(bf16)
    return jnp.dot(hi, m01, preferred_element_type=f32) + jnp.dot(lo, m01, preferred_element_type=f32)


def _topk_mask(imp, k, axis=1):
    pos = lax.broadcasted_iota(jnp.int32, imp.shape, axis).astype(f32)
    sel = jnp.zeros(imp.shape, f32)
    for _ in range(k):
        m = jnp.max(imp, axis=axis, keepdims=True)
        idx = jnp.min(jnp.where(imp == m, pos, 1e9), axis=axis, keepdims=True)
        hit = pos == idx
        sel = jnp.where(hit, 1.0, sel)
        imp = jnp.where(hit, -jnp.inf, imp)
    return sel


def _block_importance_t(p_sum, mcs_t, tpos_row, n_sel):
    hi = p_sum.astype(bf16)
    lo = (p_sum - hi.astype(f32)).astype(bf16)
    imp = (lax.dot_general(mcs_t, hi, _NT, preferred_element_type=f32)
           + lax.dot_general(mcs_t, lo, _NT, preferred_element_type=f32))
    jj = lax.broadcasted_iota(jnp.int32, (imp.shape[0], 1), 0)
    forced = jnp.where(jj == 0, 1, jnp.where(jj == (tpos_row >> 6), 1, 0))
    imp = jnp.where(jj * SEL_BLOCK <= tpos_row, imp, NEG)
    imp = jnp.where(forced == 1, BIG, imp)
    return jnp.where(jj < n_sel, imp, -jnp.inf)


def _block_importance(p_sum, mcs, tpos, n_sel):
    imp = _dot_hilo(p_sum, mcs)
    jj = lax.broadcasted_iota(jnp.int32, (1, imp.shape[1]), 1)
    forced = jnp.where(jj == 0, 1, jnp.where(jj == (tpos >> 6), 1, 0))
    imp = jnp.where(jj * SEL_BLOCK <= tpos, imp, NEG)
    imp = jnp.where(forced == 1, BIG, imp)
    return jnp.where(jj < n_sel, imp, -jnp.inf)


def _tile_row(ref, j):
    n, s, d = ref.shape
    return ref.reshape(n * s, d)[pl.ds(j, n, stride=s), :]


def _gate(sg, idx):
    lane = lax.broadcasted_iota(jnp.int32, (1, sg.shape[1]), 1)
    return jnp.sum(jnp.where(lane == idx, sg, 0.0), axis=-1, keepdims=True)


def _softmax_pv(q, k, v, mask):
    s = jnp.where(mask, lax.dot_general(q, k, _NT, preferred_element_type=f32), NEG)
    e = jnp.exp2(s - jnp.max(s, axis=-1, keepdims=True))
    den = jnp.sum(e, axis=-1, keepdims=True)
    return jnp.dot(e.astype(bf16), v, preferred_element_type=f32) / den


def _nsa_prompt_kernel(l_ref, q_ref, ka_ref, kb_ref, va_ref, vb_ref, ks_ref, vs_ref, kw_ref, vw_ref, ag_ref,
                       mcs_ref, e_ref, o_ref, os_scr, *, tq, t_len, wq, wlen):
    g = pl.program_id(1)
    i = pl.program_id(2)
    t0 = i * tq
    tpos = t0 + lax.broadcasted_iota(jnp.int32, (tq, 1), 0)
    n_c = t_len // CMP_STRIDE
    n_sel = t_len // SEL_BLOCK

    kc = _combine_cmp(ka_ref[...], kb_ref[...]).astype(bf16)
    vc = _combine_cmp(va_ref[...], vb_ref[...]).astype(bf16)
    cend = lax.broadcasted_iota(jnp.int32, (1, n_c), 1) * CMP_STRIDE + (CMP_BLOCK - 1)
    mask_c = cend <= tpos

    qs = [(q_ref[:, r * HEAD_DIM:(r + 1) * HEAD_DIM] * (SCALE * LOG2E)).astype(bf16) for r in range(GROUP)]

    p_sum = jnp.zeros((tq, n_c), f32)
    o_c = []
    for r in range(GROUP):
        s = lax.dot_general(qs[r], kc, _NT, preferred_element_type=f32)
        e, den = _masked_exp(s, mask_c, jnp.exp2)
        p = e / jnp.maximum(den, 1e-30)
        p_sum = p_sum + p
        o_c.append(jnp.dot(p.astype(bf16), vc, preferred_element_type=f32))

    tpos_row = t0 + lax.broadcasted_iota(jnp.int32, (1, tq), 1)
    imp_t = _block_importance_t(p_sum, mcs_ref[...], tpos_row, n_sel)
    selb_t = _topk_mask(imp_t, min(N_SEL, n_sel), axis=0).astype(bf16)

    for br in range(t_len // tq):
        @pl.when(i == br)
        def _(br=br):
            klen = (br + 1) * tq
            selk = lax.dot_general(selb_t, e_ref[:, 0:klen], _TN, preferred_element_type=f32)
            kpos = lax.broadcasted_iota(jnp.int32, (1, klen), 1)
            mask_s = jnp.where(kpos <= tpos, selk, 0.0) > 0.5
            ks = ks_ref[0:klen, :].astype(bf16)
            vs = vs_ref[0:klen, :].astype(bf16)
            for r in range(GROUP):
                os_scr[:, r * HEAD_DIM:(r + 1) * HEAD_DIM] = _softmax_pv(qs[r], ks, vs, mask_s)

    sg = _sigmoid(ag_ref[...])
    gates = [[_gate(sg, (g * GROUP + r) * 3 + k) for k in range(3)] for r in range(GROUP)]
    for h in range(tq // wq):
        rs = slice(h * wq, (h + 1) * wq)
        start = pl.multiple_of(jnp.maximum(t0 + (h + 1) * wq - wlen, 0), wq)
        kw = kw_ref[pl.ds(start, wlen), :].astype(bf16)
        vw = vw_ref[pl.ds(start, wlen), :].astype(bf16)
        d = tpos[rs] - (start + lax.broadcasted_iota(jnp.int32, (1, wlen), 1))
        mask_w = jnp.where(d >= 0, d, WINDOW) < WINDOW
        for r in range(GROUP):
            cs = slice(r * HEAD_DIM, (r + 1) * HEAD_DIM)
            o_w = _softmax_pv(qs[r][rs], kw, vw, mask_w)
            out = gates[r][0][rs] * o_c[r][rs] + gates[r][1][rs] * os_scr[rs, cs] + gates[r][2][rs] * o_w
            o_ref[rs, cs] = out.astype(o_ref.dtype)


def _nsa_prompt(larr, z, cmp_a, cmp_b, mcs, emat, nb, t_len, tq):
    m = z.shape[0]
    nq = t_len // tq
    n_c = t_len // CMP_STRIDE
    wq = min(tq, 256)
    wlen = min(WINDOW + wq, t_len)
    gw = GROUP * HEAD_DIM
    kern = functools.partial(_nsa_prompt_kernel, tq=tq, t_len=t_len, wq=wq, wlen=wlen)

    def head_cols(c0):
        return pl.BlockSpec((t_len, HEAD_DIM), lambda b, g, i, l: (b, c0 // HEAD_DIM + g))

    def cmp_spec(off):
        return pl.BlockSpec((n_c, HEAD_DIM), lambda b, g, i, l: (b, off + g))

    return _call(
        kern, grid=(nb, N_KV_HEADS, nq),
        in_specs=[
            pl.BlockSpec((tq, gw), lambda b, g, i, l: (b * nq + i, C_AQ // gw + g)),
            cmp_spec(0), cmp_spec(0), cmp_spec(N_KV_HEADS), cmp_spec(N_KV_HEADS),
            head_cols(C_KS), head_cols(C_VS), head_cols(C_KW), head_cols(C_VW),
            pl.BlockSpec((tq, LANE), lambda b, g, i, l: (b * nq + i, C_AG // LANE)),
            pl.BlockSpec(mcs.shape, lambda b, g, i, l: (0, 0)),
            pl.BlockSpec(emat.shape, lambda b, g, i, l: (0, 0)),
        ],
        out_specs=pl.BlockSpec((tq, gw), lambda b, g, i, l: (b * nq + i, g)),
        out_shape=jax.ShapeDtypeStruct((m, N_HEADS * HEAD_DIM), bf16),
        scratch=[pltpu.VMEM((tq, gw), f32)],
        name="nsa_prompt",
    )(larr, z, cmp_a, cmp_b, cmp_a, cmp_b, z, z, z, z, z, mcs, emat)


def _rows_rt(ref, g_off, tn):
    return jnp.concatenate(
        [ref[:, (g_off + r) * HEAD_DIM:(g_off + r + 1) * HEAD_DIM] for r in range(GROUP)], axis=0)


def _nsa_s_cmp_kernel(l_ref, q_ref, a_ref, b_ref, mcs_ref, oc_ref, sel_ref, *, tn, past, n_sel):
    n_c = a_ref.shape[0]
    rows = GROUP * tn
    tpos_r = past + (lax.broadcasted_iota(jnp.int32, (rows, 1), 0) % tn)
    tpos = past + (lax.broadcasted_iota(jnp.int32, (2 * tn, 1), 0) % tn)
    cend = lax.broadcasted_iota(jnp.int32, (1, n_c), 1) * CMP_STRIDE + (CMP_BLOCK - 1)
    mask_c = cend <= tpos_r
    mcs = mcs_ref[...]
    for g in range(N_KV_HEADS):
        q = (_rows_rt(q_ref, g * GROUP, tn) * SCALE).astype(bf16)
        kc = _combine_cmp(_tile_row(a_ref, g), _tile_row(b_ref, g)).astype(bf16)
        vc = _combine_cmp(_tile_row(a_ref, N_KV_HEADS + g), _tile_row(b_ref, N_KV_HEADS + g)).astype(bf16)
        s = lax.dot_general(q, kc, _NT, preferred_element_type=f32)
        e, den = _masked_exp(s, mask_c)
        p = e / jnp.maximum(den, 1e-30)
        oc_ref[g] = jnp.dot(p.astype(bf16), vc, preferred_element_type=f32)
        p_sum = p[0:tn]
        for r in range(1, GROUP):
            p_sum = p_sum + p[r * tn:(r + 1) * tn]
        p_sum = jnp.concatenate([p_sum, jnp.zeros_like(p_sum)], axis=0)
        imp = _block_importance(p_sum, mcs, tpos, n_sel)
        sel_ref[g] = _topk_mask(imp, min(N_SEL, n_sel))[0:tn]


def _nsa_s_cmp(larr, z, cmp_a, cmp_b, mcs, db, tn, past, n_sel):
    n_c = cmp_a.shape[0] // db
    aq = N_HEADS * HEAD_DIM
    kern = functools.partial(_nsa_s_cmp_kernel, tn=tn, past=past, n_sel=n_sel)
    cmp_spec = pl.BlockSpec((n_c,) + cmp_a.shape[1:], lambda b, l: (b, 0, 0))
    return _call(
        kern, grid=(db,),
        in_specs=[pl.BlockSpec((tn, aq), lambda b, l: (b, C_AQ // aq)), cmp_spec, cmp_spec,
                  pl.BlockSpec(mcs.shape, lambda b, l: (0, 0))],
        out_specs=[pl.BlockSpec((None, N_KV_HEADS, GROUP * tn, HEAD_DIM), lambda b, l: (b, 0, 0, 0)),
                   pl.BlockSpec((None, N_KV_HEADS, tn, mcs.shape[1]), lambda b, l: (b, 0, 0, 0))],
        out_shape=[jax.ShapeDtypeStruct((db, N_KV_HEADS, GROUP * tn, HEAD_DIM), f32),
                   jax.ShapeDtypeStruct((db, N_KV_HEADS, tn, mcs.shape[1]), f32)],
        name="nsa_sample_cmp",
    )(larr, z, cmp_a, cmp_b, mcs)


def _nsa_s_sel_kernel(*refs, tn, pps):
    q_ref, sel_ref, e_ref = refs[2:5]
    page_refs = refs[5:5 + pps]
    acc_ref, m_ref, l_ref = refs[5 + pps:]
    n = pl.program_id(1)

    @pl.when(n == 0)
    def _():
        acc_ref[...] = jnp.zeros_like(acc_ref)
        m_ref[...] = jnp.full_like(m_ref, NEG)
        l_ref[...] = jnp.zeros_like(l_ref)

    emat = e_ref[...]
    for g in range(N_KV_HEADS):
        q = (_rows_rt(q_ref, g * GROUP, tn) * SCALE).astype(bf16)
        k_all = jnp.concatenate([_tile_row(pr, g).astype(bf16) for pr in page_refs], axis=0)
        v_all = jnp.concatenate([_tile_row(pr, N_KV_HEADS + g).astype(bf16) for pr in page_refs], axis=0)
        s = lax.dot_general(q, k_all, _NT, preferred_element_type=f32)
        selg = sel_ref[g]
        selg = jnp.concatenate([selg, jnp.zeros_like(selg)], axis=0).astype(bf16)
        selk = jnp.dot(selg, emat, preferred_element_type=f32)[0:tn]
        mask = jnp.concatenate([selk] * GROUP, axis=0) > 0.5
        s = jnp.where(mask, s, NEG)
        m_old = m_ref[g]
        m_new = jnp.maximum(m_old, jnp.max(s, axis=-1, keepdims=True))
        alpha = jnp.exp(m_old - m_new)
        e = jnp.where(mask, jnp.exp(s - m_new[:, 0:1]), 0.0)
        l_ref[g] = alpha * l_ref[g] + jnp.sum(e, axis=-1, keepdims=True)
        acc_ref[g] = alpha * acc_ref[g] + jnp.dot(e.astype(bf16), v_all, preferred_element_type=f32)
        m_ref[g] = m_new


def _nsa_s_sel(larr, page_table, z, sel, emat_pages, cache, n_pool, tn):
    db, n_pages = page_table.shape
    page, _, sg, d = cache.shape[1:]
    pps = PAGES_PER_STEP
    steps = n_pages // pps
    rows = GROUP * tn
    kern = functools.partial(_nsa_s_sel_kernel, tn=tn, pps=pps)

    def page_spec(k):
        return pl.BlockSpec((None, page, None, sg, d),
                            lambda b, n, l, pt: (l[0] * n_pool + pt[b, n * pps + k], 0, 1, 0, 0))

    st = jax.ShapeDtypeStruct((db, N_KV_HEADS, rows, HEAD_DIM), f32)
    st_spec = pl.BlockSpec((None, N_KV_HEADS, rows, HEAD_DIM), lambda b, n, l, pt: (b, 0, 0, 0))
    return _call(
        kern, grid=(db, steps), nsp=2,
        in_specs=[pl.BlockSpec((tn, N_HEADS * HEAD_DIM), lambda b, n, l, pt: (b, C_AQ // (N_HEADS * HEAD_DIM))),
                  pl.BlockSpec((None, None, N_KV_HEADS, tn, LANE), lambda b, n, l, pt: (b, n, 0, 0, 0)),
                  pl.BlockSpec(emat_pages.shape, lambda b, n, l, pt: (0, 0))]
        + [page_spec(k) for k in range(pps)],
        out_specs=[st_spec, st_spec, st_spec],
        out_shape=[st, st, st], name="nsa_sample_sel",
    )(larr, page_table, z, sel, emat_pages, *([cache] * pps))


def _nsa_s_fin_kernel(l_ref, q_ref, ksn_ref, vsn_ref, kwn_ref, vwn_ref, ag_ref, buf_ref,
                      acc_ref, m_ref, lsum_ref, oc_ref, sel_ref, o_ref, *, tn, past, n_sel):
    rows = GROUP * tn
    wc = buf_ref.shape[0]
    tpos = past + (lax.broadcasted_iota(jnp.int32, (rows, 1), 0) % tn)
    kidx = lax.broadcasted_iota(jnp.int32, (1, LANE), 1)
    widx = lax.broadcasted_iota(jnp.int32, (1, wc + LANE), 1)
    d = tpos - (past - wc + widx)
    mask_w = jnp.where(widx < wc + tn, jnp.where(d >= 0, d, WINDOW), WINDOW) < WINDOW
    sg = _sigmoid(ag_ref[...])

    def new_rows(ref, g):
        a = ref[:, g * HEAD_DIM:(g + 1) * HEAD_DIM]
        return jnp.concatenate([a, jnp.zeros((LANE - tn, HEAD_DIM), a.dtype)], axis=0)

    for g in range(N_KV_HEADS):
        q = (_rows_rt(q_ref, g * GROUP, tn) * SCALE).astype(bf16)

        ksn = new_rows(ksn_ref, g).astype(bf16)
        vsn = new_rows(vsn_ref, g).astype(bf16)
        s = lax.dot_general(q, ksn, _NT, preferred_element_type=f32)
        last_sel = sel_ref[g][:, n_sel - 1:n_sel]
        last_sel = jnp.concatenate([last_sel] * GROUP, axis=0)
        mask = jnp.where(kidx < tn, jnp.where(past + kidx <= tpos, last_sel, 0.0), 0.0) > 0.5
        s = jnp.where(mask, s, NEG)
        m_old = m_ref[g]
        m_new = jnp.maximum(m_old, jnp.max(s, axis=-1, keepdims=True))
        alpha = jnp.exp(m_old - m_new)
        e = jnp.where(mask, jnp.exp(s - m_new[:, 0:1]), 0.0)
        den = alpha * lsum_ref[g] + jnp.sum(e, axis=-1, keepdims=True)
        o_s = (alpha * acc_ref[g] + jnp.dot(e.astype(bf16), vsn, preferred_element_type=f32)) / jnp.maximum(den, 1e-30)

        kw = jnp.concatenate([_tile_row(buf_ref, g), new_rows(kwn_ref, g)], axis=0).astype(bf16)
        vw = jnp.concatenate([_tile_row(buf_ref, N_KV_HEADS + g), new_rows(vwn_ref, g)], axis=0).astype(bf16)
        s = lax.dot_general(q, kw, _NT, preferred_element_type=f32)
        e, den = _masked_exp(s, mask_w)
        o_w = jnp.dot(e.astype(bf16), vw, preferred_element_type=f32) / jnp.maximum(den, 1e-30)

        o_c = oc_ref[g]
        for r in range(GROUP):
            base = (g * GROUP + r) * 3
            sl = slice(r * tn, (r + 1) * tn)
            out = _gate(sg, base) * o_c[sl] + _gate(sg, base + 1) * o_s[sl] + _gate(sg, base + 2) * o_w[sl]
            o_ref[:, (g * GROUP + r) * HEAD_DIM:(g * GROUP + r + 1) * HEAD_DIM] = out


def _nsa_s_fin(larr, z, win_buf, acc, mx, lsum, o_c, sel, db, tn, past, n_sel):
    aq = N_HEADS * HEAD_DIM
    kvw = N_KV_HEADS * HEAD_DIM
    rows = GROUP * tn
    kern = functools.partial(_nsa_s_fin_kernel, tn=tn, past=past, n_sel=n_sel)

    def new_cols(c0):
        return pl.BlockSpec((tn, kvw), lambda b, l: (b, c0 // kvw))

    st_spec = pl.BlockSpec((None, N_KV_HEADS, rows, HEAD_DIM), lambda b, l: (b, 0, 0, 0))
    return _call(
        kern, grid=(db,),
        in_specs=[pl.BlockSpec((tn, aq), lambda b, l: (b, C_AQ // aq)),
                  new_cols(C_KS), new_cols(C_VS), new_cols(C_KW), new_cols(C_VW),
                  pl.BlockSpec((tn, LANE), lambda b, l: (b, C_AG // LANE)),
                  pl.BlockSpec((None,) + win_buf.shape[1:], lambda b, l: (l[0] * db + b, 0, 0, 0)),
                  st_spec, st_spec, st_spec, st_spec,
                  pl.BlockSpec((None, N_KV_HEADS, tn, sel.shape[-1]), lambda b, l: (b, 0, 0, 0))],
        out_specs=pl.BlockSpec((tn, aq), lambda b, l: (b, 0)),
        out_shape=jax.ShapeDtypeStruct((db * tn, aq), f32),
        name="nsa_sample_fin",
    )(larr, z, z, z, z, z, z, win_buf, acc, mx, lsum, o_c, sel)


def _merge_kernel(l_ref, x_ref, or_ref, oa_ref, ga_ref, gb_ref, wpa_ref, wpb_ref, wo_ref, o_ref, acc_ref):
    j = pl.program_id(1)

    @pl.when(j == 0)
    def _():
        acc_ref[...] = jnp.zeros_like(acc_ref)

    pa = jnp.dot(or_ref[...], wpa_ref[...], preferred_element_type=f32)
    pb = jnp.dot(oa_ref[...], wpb_ref[...], preferred_element_type=f32)
    mix = _sigmoid(ga_ref[...]) * pa + _sigmoid(gb_ref[...]) * pb
    acc_ref[...] += jnp.dot(mix.astype(bf16), wo_ref[...], preferred_element_type=f32)

    @pl.when(j == pl.num_programs(1) - 1)
    def _():
        o_ref[...] = x_ref[...] + acc_ref[...]


def _merge(larr, x, o_r, o_a, z, w_pa, w_pb, w_out, tm):
    m = x.shape[0]
    r_v = RET_HEADS * RET_DV
    a_q = N_HEADS * HEAD_DIM
    return _call(
        _merge_kernel, grid=(m // tm, D_MODEL // TN),
        in_specs=[
            pl.BlockSpec((tm, D_MODEL), lambda i, j, l: (i, 0)),
            pl.BlockSpec((tm, r_v), lambda i, j, l: (i, 0)),
            pl.BlockSpec((tm, a_q), lambda i, j, l: (i, 0)),
            pl.BlockSpec((tm, TN), lambda i, j, l: (i, C_GA // TN + j)),
            pl.BlockSpec((tm, TN), lambda i, j, l: (i, C_GB // TN + j)),
            pl.BlockSpec((None, r_v, TN), lambda i, j, l: (l[0], 0, j)),
            pl.BlockSpec((None, a_q, TN), lambda i, j, l: (l[0], 0, j)),
            pl.BlockSpec((None, TN, D_MODEL), lambda i, j, l: (l[0], j, 0)),
        ],
        out_specs=pl.BlockSpec((tm, D_MODEL), lambda i, j, l: (i, 0)),
        out_shape=jax.ShapeDtypeStruct((m, D_MODEL), f32),
        scratch=[pltpu.VMEM((tm, D_MODEL), f32)],
        name="merge",
    )(larr, x, o_r, o_a, z, z, w_pa, w_pb, w_out)


def _rope_tables(pos):
    half = HEAD_DIM // 2
    inv = 1.0 / (ROPE_THETA ** (jnp.arange(half, dtype=f32) / half))
    ang = pos.astype(f32)[:, None] * inv[None, :]
    cos, sin = jnp.cos(ang), jnp.sin(ang)
    return jnp.concatenate([cos, cos], axis=-1), jnp.concatenate([-sin, sin], axis=-1)


def _cmp_to_sel_table(n_c_valid, n_sel, rows, cols):
    cs = np.arange(n_c_valid) * CMP_STRIDE
    ce = cs + CMP_BLOCK - 1
    js = np.arange(n_sel) * SEL_BLOCK
    je = js + SEL_BLOCK - 1
    tab = np.zeros((rows, cols), np.float32)
    tab[:n_c_valid, :n_sel] = (cs[:, None] <= je[None, :]) & (ce[:, None] >= js[None, :])
    return jnp.asarray(tab, dtype=bf16)


def _expand_table(rows, n_keys, first_block=0):
    tab = np.zeros((rows, n_keys), np.float32)
    s = np.arange(n_keys)
    tab[first_block + s // SEL_BLOCK, s] = 1.0
    return tab


def _round_up(a, b):
    return -(-a // b) * b


def kernel(x_prompt, x_sample, cache_kv, state_win, state_ret, page_table, norm_gain, ffn_gate,
           ffn_up, ffn_down, w_in, qk_norm, cmp_w, ret_gn, w_pa, w_pb, w_out):
    nb, t_len, _ = x_prompt.shape
    db, tn, _ = x_sample.shape
    depth, n_pool, page = cache_kv.shape[:3]
    n_pages = page_table.shape[1]
    past = n_pages * page
    wc = state_win.shape[2]
    kvw = N_KV_HEADS * HEAD_DIM
    assert t_len % 512 == 0 and t_len >= WINDOW and wc == WINDOW and tn == 8
    assert n_pages % PAGES_PER_STEP == 0 and n_pages % CMP_PAGES_PER_STEP == 0
    assert past % SEL_BLOCK == 0 and tn <= CMP_STRIDE

    wg, wu, wd = ffn_gate.astype(bf16), ffn_up.astype(bf16), ffn_down.astype(bf16)
    w_head = w_in.astype(bf16)
    w_tail = w_in[..., W_IN_SPLIT:].astype(bf16)
    w_pa_b, w_pb_b, w_out_b = w_pa.astype(bf16), w_pb.astype(bf16), w_out.astype(bf16)

    gains = norm_gain.reshape(depth, 3, 1, D_MODEL)
    ones = jnp.ones((depth, HEAD_DIM), f32)
    tile_rows = []
    for j in range(N_ZT):
        c0 = j * TN
        if c0 == C_KC:
            tile_rows.append(qk_norm[:, 1])
        elif c0 == C_KS:
            tile_rows.append(qk_norm[:, 2])
        elif c0 == C_KW:
            tile_rows.append(qk_norm[:, 3])
        elif C_AQ <= c0 < C_KC:
            tile_rows.append(qk_norm[:, 0])
        elif C_RK <= c0 < C_RV:
            tile_rows.append(ones * (RET_DK ** -0.5))
        else:
            tile_rows.append(ones)
    tile_gain = jnp.stack(tile_rows, axis=1).reshape(depth, N_ZT, 1, HEAD_DIM)
    modes = jnp.asarray(_TILE_MODE, jnp.int32)

    w_cmp = jnp.repeat(cmp_w, HEAD_DIM, axis=-1)
    w_cmp = w_cmp.transpose(0, 2, 1, 3).reshape(depth, CMP_BLOCK, 2 * kvw)
    w_cmp_rows = w_cmp.reshape(depth, CMP_BLOCK, 2 * N_KV_HEADS, HEAD_DIM)
    gn = ret_gn.reshape(depth, RET_HEADS, 1, RET_DV)

    cos_p, sin_p = _rope_tables(jnp.arange(t_len, dtype=jnp.int32))
    cos_s, sin_s = _rope_tables(jnp.tile(past + jnp.arange(tn, dtype=jnp.int32), db))

    n_c_p = t_len // CMP_STRIDE
    n_sel_p = t_len // SEL_BLOCK
    mcs_p = _cmp_to_sel_table((t_len - CMP_BLOCK) // CMP_STRIDE + 1, n_sel_p, n_c_p, n_sel_p).T
    emat_p = jnp.asarray(_expand_table(n_sel_p, t_len), dtype=bf16)
    l_full = past + tn
    n_sel_s = -(-l_full // SEL_BLOCK)
    n_c_s = past // CMP_STRIDE
    sel_w = _round_up(n_sel_s, LANE)
    mcs_s = _cmp_to_sel_table((l_full - CMP_BLOCK) // CMP_STRIDE + 1, n_sel_s, n_c_s, sel_w)
    steps = n_pages // PAGES_PER_STEP
    keys_per_step = PAGES_PER_STEP * page
    blocks_per_step = keys_per_step // SEL_BLOCK
    emat_s = jnp.asarray(_expand_table(LANE, keys_per_step), dtype=bf16)

    cache2 = cache_kv.reshape(depth * n_pool, page, 2, 2 * N_KV_HEADS, HEAD_DIM)
    win2 = state_win.reshape(depth * db, wc, 2 * N_KV_HEADS, HEAD_DIM)
    sret2 = state_ret.reshape(depth * db, RET_HEADS, RET_DK, RET_DV)
    zero_state = jnp.zeros((nb, RET_HEADS, RET_DK, RET_DV), f32)

    mp = nb * t_len
    ms = db * tn
    tm_p = 512
    rb = 512
    keep = min(WINDOW, t_len)
    tm_in = 1024
    tq = 512
    c_p = math.gcd(t_len, RET_CHUNK)

    def layer(carry, l):
        xp, xs, kv_buf, win_buf = carry
        larr = jnp.reshape(l, (1,)).astype(jnp.int32)

        xp = _ffn(larr, xp, gains, 0, wg, wu, wd, 0, tm_p)
        xs = _ffn(larr, xs, gains, 0, wg, wu, wd, 0, ms)

        zp = _inproj(larr, modes, xp, gains, w_head, w_tail, tile_gain, cos_p, sin_p, min(tm_in, t_len))
        o_r, s_fin = _retention(larr, zp, zero_state, 0, gn, nb, t_len // c_p, c_p, bf16)
        ca, cb = _compress_prompt(larr, zp, w_cmp, 512)
        o_a = _nsa_prompt(larr, zp, ca, cb, mcs_p, emat_p, nb, t_len, tq)
        xp = _merge(larr, xp, o_r, o_a, zp, w_pa_b, w_pb_b, w_out_b, tm_p)
        kv_buf = _kv_rows(larr, zp, kv_buf, rb, C_KC, 4 * kvw, mp // rb, lambda i: i)
        win_buf = _kv_rows(larr, zp, win_buf, rb, C_KW, 2 * kvw, nb * (keep // rb),
                           lambda i: (i // (keep // rb)) * (t_len // rb) + (t_len - keep) // rb + i % (keep // rb))

        zs = _inproj(larr, modes, xs, gains, w_head, w_tail, tile_gain, cos_s, sin_s, ms)
        o_rs, s_new = _retention(larr, zs, sret2, db, gn, db, 1, tn, f32)
        sa, sb = _compress_pages(larr, page_table, cache2, w_cmp_rows, n_pool)
        o_c, sel = _nsa_s_cmp(larr, zs, sa, sb, mcs_s, db, tn, past, n_sel_s)
        sel_steps = sel[..., :steps * blocks_per_step].reshape(db, N_KV_HEADS, tn, steps, blocks_per_step)
        sel_steps = jnp.pad(sel_steps.transpose(0, 3, 1, 2, 4), ((0, 0),) * 4 + ((0, LANE - blocks_per_step),))
        acc, mx, lsum = _nsa_s_sel(larr, page_table, zs, sel_steps, emat_s, cache2, n_pool, tn)
        o_as = _nsa_s_fin(larr, zs, win2, acc, mx, lsum, o_c, sel, db, tn, past, n_sel_s)
        xs = _merge(larr, xs, o_rs.astype(bf16), o_as.astype(bf16), zs, w_pa_b, w_pb_b, w_out_b, ms)
        kv_s = zs[:, C_KC:C_KW].reshape(db, tn, 4, N_KV_HEADS, HEAD_DIM)
        win_new = zs[:, C_KW:C_AG].reshape(db, tn, 2, N_KV_HEADS, HEAD_DIM)
        win_old = lax.dynamic_index_in_dim(state_win, l, 0, keepdims=False)
        win_s = jnp.concatenate([win_old[:, tn:], win_new], axis=1)

        xp = _ffn(larr, xp, gains, 2, wg, wu, wd, 1, tm_p)
        xs = _ffn(larr, xs, gains, 2, wg, wu, wd, 1, ms)
        return (xp, xs, kv_buf, win_buf), (kv_s, win_s, s_fin, s_new)

    kv_buf0 = jnp.zeros((depth, mp * 4 * kvw // LANE, LANE), f32)
    win_buf0 = jnp.zeros((depth, nb * keep * 2 * kvw // LANE, LANE), f32)
    (xp, xs, kv_buf, win_buf), outs = lax.scan(
        layer, (x_prompt.reshape(mp, D_MODEL), x_sample.reshape(ms, D_MODEL), kv_buf0, win_buf0),
        jnp.arange(depth, dtype=jnp.int32))
    kv_s, win_s, ret_p, ret_s = outs
    kv_p = kv_buf.reshape(depth, nb, t_len, 4, N_KV_HEADS, HEAD_DIM)
    win_p = win_buf.reshape(depth, nb, keep, 2, N_KV_HEADS, HEAD_DIM)
    return (xp.reshape(nb, t_len, D_MODEL), xs.reshape(db, tn, D_MODEL), kv_p, kv_s, win_p, win_s, ret_p, ret_s)
```

```python
import functools
import math

import jax
import jax.numpy as jnp
import numpy as np
from jax import lax
from jax.experimental import pallas as pl
from jax.experimental.pallas import tpu as pltpu

D_MODEL = 2048
D_FF = 5504
RET_HEADS = 8
RET_DK = 128
RET_DV = 256
RET_CHUNK = 128
N_HEADS = 16
N_KV_HEADS = 4
HEAD_DIM = 128
GROUP = N_HEADS // N_KV_HEADS
CMP_BLOCK = 32
CMP_STRIDE = 16
SEL_BLOCK = 64
N_SEL = 16
WINDOW = 512
ROPE_THETA = 10000.0
EPS = 1e-6
NEG = -1e30
BIG = 1e30
SCALE = HEAD_DIM ** -0.5
LOG2E = math.log2(math.e)

LANE = 128
VMEM_LIMIT = 56 * 1024 * 1024

TN = 512
C_RQ, C_RK, C_RV, C_RG = 0, 1024, 2048, 4096
C_AQ = 6144
C_KC, C_VC, C_KS, C_VS, C_KW, C_VW = 8192, 8704, 9216, 9728, 10240, 10752
C_AG = 11264
C_GA, C_GB = 11776, 13824
DZ = 15872
N_ZT = DZ // TN
N_HEAD_TILES = C_GA // TN
W_IN_SPLIT = 11312
_TILE_MODE = [1, 1, 1, 1] + [0] * 8 + [2, 2, 2, 2] + [2, 0, 2, 0, 2, 0] + [0] * 9

TF = 512
F_TILES = -(-D_FF // TF)
PAGES_PER_STEP = 16
CMP_PAGES_PER_STEP = 16

_RET_LOG_G = [float(np.log(np.float32(1.0) - np.float32(2.0) ** np.float32(-5.0 - h))) for h in range(RET_HEADS)]

_NT = (((1,), (1,)), ((), ()))
_TN = (((0,), (0,)), ((), ()))

bf16 = jnp.bfloat16
f32 = jnp.float32


def _sigmoid(x):
    return 1.0 / (1.0 + jnp.exp(-x))


def _call(kernel, *, grid, in_specs, out_specs, out_shape, scratch=(), nsp=1, sem=None, name=None, aliases=None):
    return pl.pallas_call(
        kernel,
        grid_spec=pltpu.PrefetchScalarGridSpec(num_scalar_prefetch=nsp, grid=grid, in_specs=in_specs,
                                               out_specs=out_specs, scratch_shapes=list(scratch)),
        out_shape=out_shape,
        compiler_params=pltpu.CompilerParams(dimension_semantics=sem or ("arbitrary",) * len(grid),
                                             vmem_limit_bytes=VMEM_LIMIT),
        input_output_aliases=aliases or {},
        name=name,
    )


def _ffn_kernel(l_ref, x_ref, g_ref, wg_ref, wu_ref, wd_ref, o_ref, h_ref, acc_ref):
    f = pl.program_id(1)

    @pl.when(f == 0)
    def _():
        x = x_ref[...]
        ms = jnp.mean(x * x, axis=-1, keepdims=True)
        h_ref[...] = (x * lax.rsqrt(ms + EPS) * g_ref[...]).astype(bf16)
        acc_ref[...] = jnp.zeros_like(acc_ref)

    def accumulate(width):
        h = h_ref[...]
        a = jnp.dot(h, wg_ref[:, 0:width], preferred_element_type=f32)
        b = jnp.dot(h, wu_ref[:, 0:width], preferred_element_type=f32)
        s = (a * _sigmoid(a)) * b
        acc_ref[...] += jnp.dot(s.astype(bf16), wd_ref[0:width, :], preferred_element_type=f32)

    last = pl.num_programs(1) - 1

    @pl.when(f < last)
    def _():
        accumulate(TF)

    @pl.when(f == last)
    def _():
        accumulate(D_FF - (F_TILES - 1) * TF)
        o_ref[...] = x_ref[...] + 0.5 * acc_ref[...]


def _ffn(larr, x, gains, which_gain, wg, wu, wd, which_w, tm):
    m = x.shape[0]
    grid = (m // tm, F_TILES)
    return _call(
        _ffn_kernel, grid=grid,
        in_specs=[
            pl.BlockSpec((tm, D_MODEL), lambda i, f, l: (i, 0)),
            pl.BlockSpec((None, None, 1, D_MODEL), lambda i, f, l: (l[0], which_gain, 0, 0)),
            pl.BlockSpec((None, None, D_MODEL, TF), lambda i, f, l: (l[0], which_w, 0, f)),
            pl.BlockSpec((None, None, D_MODEL, TF), lambda i, f, l: (l[0], which_w, 0, f)),
            pl.BlockSpec((None, None, TF, D_MODEL), lambda i, f, l: (l[0], which_w, f, 0)),
        ],
        out_specs=pl.BlockSpec((tm, D_MODEL), lambda i, f, l: (i, 0)),
        out_shape=jax.ShapeDtypeStruct((m, D_MODEL), f32),
        scratch=[pltpu.VMEM((tm, D_MODEL), bf16), pltpu.VMEM((tm, D_MODEL), f32)],
        name="ffn",
    )(larr, x, gains, wg, wu, wd)


def _inproj_kernel(l_ref, mode_ref, x_ref, g_ref, w_ref, wt_ref, gain_ref, cos_ref, sin_ref, o_ref, h_ref,
                   acc_a, acc_b):
    j = pl.program_id(1)
    n_tiles = pl.num_programs(1) - 1
    mode = mode_ref[jnp.maximum(j - 1, 0)]

    def matmul(acc, w):
        acc[...] = jnp.dot(h_ref[...], w[...], preferred_element_type=f32)

    def finish_plain(acc):
        o_ref[...] = acc[...]

    def finish_rotary(acc):
        cos = cos_ref[...]
        sin = sin_ref[...]
        gain = gain_ref[...]
        for hd in range(TN // HEAD_DIM):
            y = acc[:, hd * HEAD_DIM:(hd + 1) * HEAD_DIM]
            ms = jnp.mean(y * y, axis=-1, keepdims=True)
            inv = jnp.where(mode == 2, lax.rsqrt(ms + EPS), 1.0)
            y = y * inv * gain
            o_ref[:, hd * HEAD_DIM:(hd + 1) * HEAD_DIM] = y * cos + pltpu.roll(y, HEAD_DIM // 2, 1) * sin

    @pl.when(j == 0)
    def _():
        x = x_ref[...]
        ms = jnp.mean(x * x, axis=-1, keepdims=True)
        h_ref[...] = (x * lax.rsqrt(ms + EPS) * g_ref[...]).astype(bf16)
        matmul(acc_a, w_ref)

    for parity, (cur, prev) in enumerate(((acc_a, acc_b), (acc_b, acc_a))):
        mid = (j > 0) & (j < n_tiles) & ((j & 1) == parity)

        @pl.when(mid & (mode == 0) & (j < N_HEAD_TILES))
        def _(cur=cur, prev=prev):
            matmul(cur, w_ref)
            finish_plain(prev)

        @pl.when(mid & (mode == 0) & (j >= N_HEAD_TILES))
        def _(cur=cur, prev=prev):
            matmul(cur, wt_ref)
            finish_plain(prev)

        @pl.when(mid & (mode != 0))
        def _(cur=cur, prev=prev):
            matmul(cur, w_ref)
            finish_rotary(prev)

    @pl.when(j == n_tiles)
    def _():
        finish_plain(acc_a if N_ZT % 2 == 1 else acc_b)


def _inproj(larr, modes, x, gains, w_head, w_tail, tile_gain, cos2, sin2, tm):
    assert _TILE_MODE[-1] == 0 and all(md == 0 for md in _TILE_MODE[N_HEAD_TILES - 1:])
    assert w_head.shape[-1] >= N_HEAD_TILES * TN and w_tail.shape[-1] == (N_ZT - N_HEAD_TILES) * TN
    m = x.shape[0]
    nt = cos2.shape[0] // tm
    grid = (m // tm, N_ZT + 1)
    return _call(
        _inproj_kernel, grid=grid, nsp=2,
        in_specs=[
            pl.BlockSpec((tm, D_MODEL), lambda i, j, l, md: (i, 0)),
            pl.BlockSpec((None, None, 1, D_MODEL), lambda i, j, l, md: (l[0], 1, 0, 0)),
            pl.BlockSpec((None, D_MODEL, TN), lambda i, j, l, md: (l[0], 0, jnp.minimum(j, N_HEAD_TILES - 1))),
            pl.BlockSpec((None, D_MODEL, TN),
                         lambda i, j, l, md: (l[0], 0, jnp.clip(j - N_HEAD_TILES, 0, N_ZT - N_HEAD_TILES - 1))),
            pl.BlockSpec((None, None, 1, HEAD_DIM), lambda i, j, l, md: (l[0], jnp.maximum(j - 1, 0), 0, 0)),
            pl.BlockSpec((tm, HEAD_DIM), lambda i, j, l, md: (i % nt, 0)),
            pl.BlockSpec((tm, HEAD_DIM), lambda i, j, l, md: (i % nt, 0)),
        ],
        out_specs=pl.BlockSpec((tm, TN), lambda i, j, l, md: (i, jnp.maximum(j - 1, 0))),
        out_shape=jax.ShapeDtypeStruct((m, DZ), f32),
        scratch=[pltpu.VMEM((tm, D_MODEL), bf16), pltpu.VMEM((tm, TN), f32), pltpu.VMEM((tm, TN), f32)],
        name="inproj",
    )(larr, modes, x, gains, w_head, w_tail, tile_gain, cos2, sin2)


def _kv_rows_kernel(l_ref, z_ref, buf_ref, o_ref):
    del buf_ref
    rows = z_ref.shape[0]
    n = z_ref.shape[1] // LANE
    for c in range(n):
        o_ref[pl.ds(c, rows, stride=n), :] = z_ref[:, c * LANE:(c + 1) * LANE]


def _kv_rows(larr, z, buf, rows, col0, width, n_blocks, in_block):
    n = width // LANE
    return _call(
        _kv_rows_kernel, grid=(n_blocks,),
        in_specs=[pl.BlockSpec((rows, width), lambda i, l: (in_block(i), col0 // width)),
                  pl.BlockSpec(memory_space=pl.ANY)],
        out_specs=pl.BlockSpec((None, rows * n, LANE), lambda i, l: (l[0], i, 0)),
        out_shape=jax.ShapeDtypeStruct(buf.shape, buf.dtype),
        aliases={2: 0}, name="kv_rows",
    )(larr, z, buf)


def _ret_kernel(l_ref, q_ref, k_ref, v_ref, rg_ref, s0_ref, gn_ref, o_ref, sout_ref, s_scr, *, c_true, c_pad):
    c = pl.program_id(1)

    @pl.when(c == 0)
    def _():
        s_scr[...] = s0_ref[...]

    ri = lax.broadcasted_iota(jnp.int32, (c_pad, c_pad), 0)
    ci = lax.broadcasted_iota(jnp.int32, (c_pad, c_pad), 1)
    diff = ri - ci
    row = lax.broadcasted_iota(jnp.int32, (c_pad, 1), 0)

    def padded(a):
        if c_pad == c_true:
            return a
        return jnp.concatenate([a, jnp.zeros((c_pad - c_true, a.shape[1]), a.dtype)], axis=0)

    for h in range(RET_HEADS):
        lg = _RET_LOG_G[h]
        dmat = jnp.where(diff >= 0, jnp.exp(jnp.maximum(diff, 0).astype(f32) * lg), 0.0)
        q = padded(q_ref[:, h * RET_DK:(h + 1) * RET_DK])
        k = padded(k_ref[:, h * RET_DK:(h + 1) * RET_DK])
        v = padded(v_ref[:, h * RET_DV:(h + 1) * RET_DV])
        qb, kb, vb = q.astype(bf16), k.astype(bf16), v.astype(bf16)
        inner = lax.dot_general(qb, kb, _NT, preferred_element_type=f32) * dmat
        xi = jnp.exp((row + 1).astype(f32) * lg)
        state = s_scr[h]
        o = (jnp.dot(inner.astype(bf16), vb, preferred_element_type=f32)
             + jnp.dot(qb, state.astype(bf16), preferred_element_type=f32) * xi)
        wk = jnp.exp((c_true - 1 - row).astype(f32) * lg)
        kw = (k * wk).astype(bf16)
        s_scr[h] = state * float(np.exp(np.float32(c_true * lg))) + lax.dot_general(
            kw, vb, _TN, preferred_element_type=f32)
        o = o[:c_true]
        mu = jnp.mean(o, axis=-1, keepdims=True)
        cen = o - mu
        var = jnp.mean(cen * cen, axis=-1, keepdims=True)
        y = cen * lax.rsqrt(var + EPS) * gn_ref[h]
        rg = rg_ref[:, h * RET_DV:(h + 1) * RET_DV]
        o_ref[:, h * RET_DV:(h + 1) * RET_DV] = (y * (rg * _sigmoid(rg))).astype(o_ref.dtype)

    @pl.when(c == pl.num_programs(1) - 1)
    def _():
        sout_ref[...] = s_scr[...]


def _retention(larr, z, s0, s0_per_layer, ret_gn, nb, nc, c_true, out_dtype):
    c_pad = max(c_true, RET_CHUNK)
    m = z.shape[0]
    r_qk = RET_HEADS * RET_DK
    r_v = RET_HEADS * RET_DV
    kern = functools.partial(_ret_kernel, c_true=c_true, c_pad=c_pad)
    return _call(
        kern, grid=(nb, nc),
        in_specs=[
            pl.BlockSpec((c_true, r_qk), lambda b, c, l: (b * nc + c, C_RQ // r_qk)),
            pl.BlockSpec((c_true, r_qk), lambda b, c, l: (b * nc + c, C_RK // r_qk)),
            pl.BlockSpec((c_true, r_v), lambda b, c, l: (b * nc + c, C_RV // r_v)),
            pl.BlockSpec((c_true, r_v), lambda b, c, l: (b * nc + c, C_RG // r_v)),
            pl.BlockSpec((None, RET_HEADS, RET_DK, RET_DV), lambda b, c, l: (l[0] * s0_per_layer + b, 0, 0, 0)),
            pl.BlockSpec((None, RET_HEADS, 1, RET_DV), lambda b, c, l: (l[0], 0, 0, 0)),
        ],
        out_specs=[
            pl.BlockSpec((c_true, r_v), lambda b, c, l: (b * nc + c, 0)),
            pl.BlockSpec((None, RET_HEADS, RET_DK, RET_DV), lambda b, c, l: (b, 0, 0, 0)),
        ],
        out_shape=[jax.ShapeDtypeStruct((m, r_v), out_dtype),
                   jax.ShapeDtypeStruct((nb, RET_HEADS, RET_DK, RET_DV), f32)],
        scratch=[pltpu.VMEM((RET_HEADS, RET_DK, RET_DV), f32)],
        name="retention",
    )(larr, z, z, z, z, s0, ret_gn)


def _cmp_kernel(*refs, n_in, n_scalar):
    x_refs = refs[n_scalar:n_scalar + n_in]
    w_ref, a_ref, b_ref = refs[n_scalar + n_in:]
    wa = w_ref[0:CMP_STRIDE]
    wb = w_ref[CMP_STRIDE:CMP_BLOCK]
    for k in range(n_in):
        x = x_refs[k][...]
        r = x.shape[0] // CMP_STRIDE
        x3 = x.reshape((r, CMP_STRIDE) + x.shape[1:])
        a_ref[k * r:(k + 1) * r] = jnp.sum(x3 * wa[None], axis=1)
        b_ref[k * r:(k + 1) * r] = jnp.sum(x3 * wb[None], axis=1)


def _compress_prompt(larr, z, w_cmp, rows):
    m = z.shape[0]
    wcols = 2 * N_KV_HEADS * HEAD_DIM
    kern = functools.partial(_cmp_kernel, n_in=1, n_scalar=1)
    shp = jax.ShapeDtypeStruct((m // CMP_STRIDE, wcols), f32)
    return _call(
        kern, grid=(m // rows,),
        in_specs=[pl.BlockSpec((rows, wcols), lambda i, l: (i, C_KC // wcols)),
                  pl.BlockSpec((None, CMP_BLOCK, wcols), lambda i, l: (l[0], 0, 0))],
        out_specs=[pl.BlockSpec((rows // CMP_STRIDE, wcols), lambda i, l: (i, 0))] * 2,
        out_shape=[shp, shp], name="compress_prompt",
    )(larr, z, w_cmp)


def _compress_pages(larr, page_table, cache, w_cmp, n_pool):
    db, n_pages = page_table.shape
    page, _, sg, d = cache.shape[1:]
    pps = CMP_PAGES_PER_STEP
    steps = n_pages // pps
    sub = page // CMP_STRIDE
    kern = functools.partial(_cmp_kernel, n_in=pps, n_scalar=2)

    def page_spec(k):
        return pl.BlockSpec((None, page, None, sg, d),
                            lambda b, n, l, pt: (l[0] * n_pool + pt[b, n * pps + k], 0, 0, 0, 0))

    shp = jax.ShapeDtypeStruct((db * n_pages * sub, sg, d), f32)
    return _call(
        kern, grid=(db, steps), nsp=2,
        in_specs=[page_spec(k) for k in range(pps)]
        + [pl.BlockSpec((None, CMP_BLOCK, sg, d), lambda b, n, l, pt: (l[0], 0, 0, 0))],
        out_specs=[pl.BlockSpec((pps * sub, sg, d), lambda b, n, l, pt: (b * steps + n, 0, 0))] * 2,
        out_shape=[shp, shp], name="compress_pages",
    )(larr, page_table, *([cache] * pps), w_cmp)


def _combine_cmp(a, b):
    n = a.shape[0]
    row = lax.broadcasted_iota(jnp.int32, (n, 1), 0)
    return a + jnp.where(row == n - 1, 0.0, pltpu.roll(b, n - 1, 0))


def _masked_exp(s, mask, exp=jnp.exp):
    s = jnp.where(mask, s, NEG)
    m = jnp.max(s, axis=-1, keepdims=True)
    e = jnp.where(mask, exp(s - m), 0.0)
    return e, jnp.sum(e, axis=-1, keepdims=True)


def _dot_hilo(p, m01):
    hi = p.astype(bf16)
    lo = (p - hi.astype(f32)).astype(bf16)
    return jnp.dot(hi, m01, preferred_element_type=f32) + jnp.dot(lo, m01, preferred_element_type=f32)


def _topk_mask(imp, k, axis=1):
    pos = lax.broadcasted_iota(jnp.int32, imp.shape, axis).astype(f32)
    sel = jnp.zeros(imp.shape, f32)
    for _ in range(k):
        m = jnp.max(imp, axis=axis, keepdims=True)
        idx = jnp.min(jnp.where(imp == m, pos, 1e9), axis=axis, keepdims=True)
        hit = pos == idx
        sel = jnp.where(hit, 1.0, sel)
        imp = jnp.where(hit, -jnp.inf, imp)
    return sel


def _block_importance_t(p_sum, mcs_t, tpos_row, n_sel):
    hi = p_sum.astype(bf16)
    lo = (p_sum - hi.astype(f32)).astype(bf16)
    imp = (lax.dot_general(mcs_t, hi, _NT, preferred_element_type=f32)
           + lax.dot_general(mcs_t, lo, _NT, preferred_element_type=f32))
    jj = lax.broadcasted_iota(jnp.int32, (imp.shape[0], 1), 0)
    forced = jnp.where(jj == 0, 1, jnp.where(jj == (tpos_row >> 6), 1, 0))
    imp = jnp.where(jj * SEL_BLOCK <= tpos_row, imp, NEG)
    imp = jnp.where(forced == 1, BIG, imp)
    return jnp.where(jj < n_sel, imp, -jnp.inf)


def _block_importance(p_sum, mcs, tpos, n_sel):
    imp = _dot_hilo(p_sum, mcs)
    jj = lax.broadcasted_iota(jnp.int32, (1, imp.shape[1]), 1)
    forced = jnp.where(jj == 0, 1, jnp.where(jj == (tpos >> 6), 1, 0))
    imp = jnp.where(jj * SEL_BLOCK <= tpos, imp, NEG)
    imp = jnp.where(forced == 1, BIG, imp)
    return jnp.where(jj < n_sel, imp, -jnp.inf)


def _tile_row(ref, j):
    n, s, d = ref.shape
    return ref.reshape(n * s, d)[pl.ds(j, n, stride=s), :]


def _gate(sg, idx):
    lane = lax.broadcasted_iota(jnp.int32, (1, sg.shape[1]), 1)
    return jnp.sum(jnp.where(lane == idx, sg, 0.0), axis=-1, keepdims=True)


def _softmax_pv(q, k, v, mask):
    s = jnp.where(mask, lax.dot_general(q, k, _NT, preferred_element_type=f32), NEG)
    e = jnp.exp2(s - jnp.max(s, axis=-1, keepdims=True))
    den = jnp.sum(e, axis=-1, keepdims=True)
    return jnp.dot(e.astype(bf16), v, preferred_element_type=f32) / den


def _nsa_prompt_kernel(l_ref, q_ref, ka_ref, kb_ref, va_ref, vb_ref, ks_ref, vs_ref, kw_ref, vw_ref, ag_ref,
                       mcs_ref, e_ref, o_ref, os_scr, *, tq, t_len, wq, wlen):
    g = pl.program_id(1)
    i = pl.program_id(2)
    t0 = i * tq
    tpos = t0 + lax.broadcasted_iota(jnp.int32, (tq, 1), 0)
    n_c = t_len // CMP_STRIDE
    n_sel = t_len // SEL_BLOCK

    kc = _combine_cmp(ka_ref[...], kb_ref[...]).astype(bf16)
    vc = _combine_cmp(va_ref[...], vb_ref[...]).astype(bf16)
    cend = lax.broadcasted_iota(jnp.int32, (1, n_c), 1) * CMP_STRIDE + (CMP_BLOCK - 1)
    mask_c = cend <= tpos

    qs = [(q_ref[:, r * HEAD_DIM:(r + 1) * HEAD_DIM] * (SCALE * LOG2E)).astype(bf16) for r in range(GROUP)]

    p_sum = jnp.zeros((tq, n_c), f32)
    o_c = []
    for r in range(GROUP):
        s = lax.dot_general(qs[r], kc, _NT, preferred_element_type=f32)
        e, den = _masked_exp(s, mask_c, jnp.exp2)
        p = e / jnp.maximum(den, 1e-30)
        p_sum = p_sum + p
        o_c.append(jnp.dot(p.astype(bf16), vc, preferred_element_type=f32))

    tpos_row = t0 + lax.broadcasted_iota(jnp.int32, (1, tq), 1)
    imp_t = _block_importance_t(p_sum, mcs_ref[...], tpos_row, n_sel)
    selb_t = _topk_mask(imp_t, min(N_SEL, n_sel), axis=0).astype(bf16)

    for br in range(t_len // tq):
        @pl.when(i == br)
        def _(br=br):
            klen = (br + 1) * tq
            selk = lax.dot_general(selb_t, e_ref[:, 0:klen], _TN, preferred_element_type=f32)
            kpos = lax.broadcasted_iota(jnp.int32, (1, klen), 1)
            mask_s = jnp.where(kpos <= tpos, selk, 0.0) > 0.5
            ks = ks_ref[0:klen, :].astype(bf16)
            vs = vs_ref[0:klen, :].astype(bf16)
            for r in range(GROUP):
                os_scr[:, r * HEAD_DIM:(r + 1) * HEAD_DIM] = _softmax_pv(qs[r], ks, vs, mask_s)

    sg = _sigmoid(ag_ref[...])
    gates = [[_gate(sg, (g * GROUP + r) * 3 + k) for k in range(3)] for r in range(GROUP)]
    for h in range(tq // wq):
        rs = slice(h * wq, (h + 1) * wq)
        start = pl.multiple_of(jnp.maximum(t0 + (h + 1) * wq - wlen, 0), wq)
        kw = kw_ref[pl.ds(start, wlen), :].astype(bf16)
        vw = vw_ref[pl.ds(start, wlen), :].astype(bf16)
        d = tpos[rs] - (start + lax.broadcasted_iota(jnp.int32, (1, wlen), 1))
        mask_w = jnp.where(d >= 0, d, WINDOW) < WINDOW
        for r in range(GROUP):
            cs = slice(r * HEAD_DIM, (r + 1) * HEAD_DIM)
            o_w = _softmax_pv(qs[r][rs], kw, vw, mask_w)
            out = gates[r][0][rs] * o_c[r][rs] + gates[r][1][rs] * os_scr[rs, cs] + gates[r][2][rs] * o_w
            o_ref[rs, cs] = out.astype(o_ref.dtype)


def _nsa_prompt(larr, z, cmp_a, cmp_b, mcs, emat, nb, t_len, tq):
    m = z.shape[0]
    nq = t_len // tq
    n_c = t_len // CMP_STRIDE
    wq = min(tq, 256)
    wlen = min(WINDOW + wq, t_len)
    gw = GROUP * HEAD_DIM
    kern = functools.partial(_nsa_prompt_kernel, tq=tq, t_len=t_len, wq=wq, wlen=wlen)

    def head_cols(c0):
        return pl.BlockSpec((t_len, HEAD_DIM), lambda b, g, i, l: (b, c0 // HEAD_DIM + g))

    def cmp_spec(off):
        return pl.BlockSpec((n_c, HEAD_DIM), lambda b, g, i, l: (b, off + g))

    return _call(
        kern, grid=(nb, N_KV_HEADS, nq),
        in_specs=[
            pl.BlockSpec((tq, gw), lambda b, g, i, l: (b * nq + i, C_AQ // gw + g)),
            cmp_spec(0), cmp_spec(0), cmp_spec(N_KV_HEADS), cmp_spec(N_KV_HEADS),
            head_cols(C_KS), head_cols(C_VS), head_cols(C_KW), head_cols(C_VW),
            pl.BlockSpec((tq, LANE), lambda b, g, i, l: (b * nq + i, C_AG // LANE)),
            pl.BlockSpec(mcs.shape, lambda b, g, i, l: (0, 0)),
            pl.BlockSpec(emat.shape, lambda b, g, i, l: (0, 0)),
        ],
        out_specs=pl.BlockSpec((tq, gw), lambda b, g, i, l: (b * nq + i, g)),
        out_shape=jax.ShapeDtypeStruct((m, N_HEADS * HEAD_DIM), bf16),
        scratch=[pltpu.VMEM((tq, gw), f32)],
        name="nsa_prompt",
    )(larr, z, cmp_a, cmp_b, cmp_a, cmp_b, z, z, z, z, z, mcs, emat)


def _rows_rt(ref, g_off, tn):
    return jnp.concatenate(
        [ref[:, (g_off + r) * HEAD_DIM:(g_off + r + 1) * HEAD_DIM] for r in range(GROUP)], axis=0)


def _nsa_s_cmp_kernel(l_ref, q_ref, a_ref, b_ref, mcs_ref, oc_ref, sel_ref, *, tn, past, n_sel):
    n_c = a_ref.shape[0]
    rows = GROUP * tn
    tpos_r = past + (lax.broadcasted_iota(jnp.int32, (rows, 1), 0) % tn)
    tpos = past + (lax.broadcasted_iota(jnp.int32, (2 * tn, 1), 0) % tn)
    cend = lax.broadcasted_iota(jnp.int32, (1, n_c), 1) * CMP_STRIDE + (CMP_BLOCK - 1)
    mask_c = cend <= tpos_r
    mcs = mcs_ref[...]
    for g in range(N_KV_HEADS):
        q = (_rows_rt(q_ref, g * GROUP, tn) * SCALE).astype(bf16)
        kc = _combine_cmp(_tile_row(a_ref, g), _tile_row(b_ref, g)).astype(bf16)
        vc = _combine_cmp(_tile_row(a_ref, N_KV_HEADS + g), _tile_row(b_ref, N_KV_HEADS + g)).astype(bf16)
        s = lax.dot_general(q, kc, _NT, preferred_element_type=f32)
        e, den = _masked_exp(s, mask_c)
        p = e / jnp.maximum(den, 1e-30)
        oc_ref[g] = jnp.dot(p.astype(bf16), vc, preferred_element_type=f32)
        p_sum = p[0:tn]
        for r in range(1, GROUP):
            p_sum = p_sum + p[r * tn:(r + 1) * tn]
        p_sum = jnp.concatenate([p_sum, jnp.zeros_like(p_sum)], axis=0)
        imp = _block_importance(p_sum, mcs, tpos, n_sel)
        sel_ref[g] = _topk_mask(imp, min(N_SEL, n_sel))[0:tn]


def _nsa_s_cmp(larr, z, cmp_a, cmp_b, mcs, db, tn, past, n_sel):
    n_c = cmp_a.shape[0] // db
    aq = N_HEADS * HEAD_DIM
    kern = functools.partial(_nsa_s_cmp_kernel, tn=tn, past=past, n_sel=n_sel)
    cmp_spec = pl.BlockSpec((n_c,) + cmp_a.shape[1:], lambda b, l: (b, 0, 0))
    return _call(
        kern, grid=(db,),
        in_specs=[pl.BlockSpec((tn, aq), lambda b, l: (b, C_AQ // aq)), cmp_spec, cmp_spec,
                  pl.BlockSpec(mcs.shape, lambda b, l: (0, 0))],
        out_specs=[pl.BlockSpec((None, N_KV_HEADS, GROUP * tn, HEAD_DIM), lambda b, l: (b, 0, 0, 0)),
                   pl.BlockSpec((None, N_KV_HEADS, tn, mcs.shape[1]), lambda b, l: (b, 0, 0, 0))],
        out_shape=[jax.ShapeDtypeStruct((db, N_KV_HEADS, GROUP * tn, HEAD_DIM), f32),
                   jax.ShapeDtypeStruct((db, N_KV_HEADS, tn, mcs.shape[1]), f32)],
        name="nsa_sample_cmp",
    )(larr, z, cmp_a, cmp_b, mcs)


def _nsa_s_sel_kernel(*refs, tn, pps):
    q_ref, sel_ref, e_ref = refs[2:5]
    page_refs = refs[5:5 + pps]
    acc_ref, m_ref, l_ref = refs[5 + pps:]
    n = pl.program_id(1)

    @pl.when(n == 0)
    def _():
        acc_ref[...] = jnp.zeros_like(acc_ref)
        m_ref[...] = jnp.full_like(m_ref, NEG)
        l_ref[...] = jnp.zeros_like(l_ref)

    emat = e_ref[...]
    for g in range(N_KV_HEADS):
        q = (_rows_rt(q_ref, g * GROUP, tn) * SCALE).astype(bf16)
        k_all = jnp.concatenate([_tile_row(pr, g).astype(bf16) for pr in page_refs], axis=0)
        v_all = jnp.concatenate([_tile_row(pr, N_KV_HEADS + g).astype(bf16) for pr in page_refs], axis=0)
        s = lax.dot_general(q, k_all, _NT, preferred_element_type=f32)
        selg = sel_ref[g]
        selg = jnp.concatenate([selg, jnp.zeros_like(selg)], axis=0).astype(bf16)
        selk = jnp.dot(selg, emat, preferred_element_type=f32)[0:tn]
        mask = jnp.concatenate([selk] * GROUP, axis=0) > 0.5
        s = jnp.where(mask, s, NEG)
        m_old = m_ref[g]
        m_new = jnp.maximum(m_old, jnp.max(s, axis=-1, keepdims=True))
        alpha = jnp.exp(m_old - m_new)
        e = jnp.where(mask, jnp.exp(s - m_new[:, 0:1]), 0.0)
        l_ref[g] = alpha * l_ref[g] + jnp.sum(e, axis=-1, keepdims=True)
        acc_ref[g] = alpha * acc_ref[g] + jnp.dot(e.astype(bf16), v_all, preferred_element_type=f32)
        m_ref[g] = m_new


def _nsa_s_sel(larr, page_table, z, sel, emat_pages, cache, n_pool, tn):
    db, n_pages = page_table.shape
    page, _, sg, d = cache.shape[1:]
    pps = PAGES_PER_STEP
    steps = n_pages // pps
    rows = GROUP * tn
    kern = functools.partial(_nsa_s_sel_kernel, tn=tn, pps=pps)

    def page_spec(k):
        return pl.BlockSpec((None, page, None, sg, d),
                            lambda b, n, l, pt: (l[0] * n_pool + pt[b, n * pps + k], 0, 1, 0, 0))

    st = jax.ShapeDtypeStruct((db, N_KV_HEADS, rows, HEAD_DIM), f32)
    st_spec = pl.BlockSpec((None, N_KV_HEADS, rows, HEAD_DIM), lambda b, n, l, pt: (b, 0, 0, 0))
    return _call(
        kern, grid=(db, steps), nsp=2,
        in_specs=[pl.BlockSpec((tn, N_HEADS * HEAD_DIM), lambda b, n, l, pt: (b, C_AQ // (N_HEADS * HEAD_DIM))),
                  pl.BlockSpec((None, None, N_KV_HEADS, tn, LANE), lambda b, n, l, pt: (b, n, 0, 0, 0)),
                  pl.BlockSpec(emat_pages.shape, lambda b, n, l, pt: (0, 0))]
        + [page_spec(k) for k in range(pps)],
        out_specs=[st_spec, st_spec, st_spec],
        out_shape=[st, st, st], name="nsa_sample_sel",
    )(larr, page_table, z, sel, emat_pages, *([cache] * pps))


def _nsa_s_fin_kernel(l_ref, q_ref, ksn_ref, vsn_ref, kwn_ref, vwn_ref, ag_ref, buf_ref,
                      acc_ref, m_ref, lsum_ref, oc_ref, sel_ref, o_ref, *, tn, past, n_sel):
    rows = GROUP * tn
    wc = buf_ref.shape[0]
    tpos = past + (lax.broadcasted_iota(jnp.int32, (rows, 1), 0) % tn)
    kidx = lax.broadcasted_iota(jnp.int32, (1, LANE), 1)
    widx = lax.broadcasted_iota(jnp.int32, (1, wc + LANE), 1)
    d = tpos - (past - wc + widx)
    mask_w = jnp.where(widx < wc + tn, jnp.where(d >= 0, d, WINDOW), WINDOW) < WINDOW
    sg = _sigmoid(ag_ref[...])

    def new_rows(ref, g):
        a = ref[:, g * HEAD_DIM:(g + 1) * HEAD_DIM]
        return jnp.concatenate([a, jnp.zeros((LANE - tn, HEAD_DIM), a.dtype)], axis=0)

    for g in range(N_KV_HEADS):
        q = (_rows_rt(q_ref, g * GROUP, tn) * SCALE).astype(bf16)

        ksn = new_rows(ksn_ref, g).astype(bf16)
        vsn = new_rows(vsn_ref, g).astype(bf16)
        s = lax.dot_general(q, ksn, _NT, preferred_element_type=f32)
        last_sel = sel_ref[g][:, n_sel - 1:n_sel]
        last_sel = jnp.concatenate([last_sel] * GROUP, axis=0)
        mask = jnp.where(kidx < tn, jnp.where(past + kidx <= tpos, last_sel, 0.0), 0.0) > 0.5
        s = jnp.where(mask, s, NEG)
        m_old = m_ref[g]
        m_new = jnp.maximum(m_old, jnp.max(s, axis=-1, keepdims=True))
        alpha = jnp.exp(m_old - m_new)
        e = jnp.where(mask, jnp.exp(s - m_new[:, 0:1]), 0.0)
        den = alpha * lsum_ref[g] + jnp.sum(e, axis=-1, keepdims=True)
        o_s = (alpha * acc_ref[g] + jnp.dot(e.astype(bf16), vsn, preferred_element_type=f32)) / jnp.maximum(den, 1e-30)

        kw = jnp.concatenate([_tile_row(buf_ref, g), new_rows(kwn_ref, g)], axis=0).astype(bf16)
        vw = jnp.concatenate([_tile_row(buf_ref, N_KV_HEADS + g), new_rows(vwn_ref, g)], axis=0).astype(bf16)
        s = lax.dot_general(q, kw, _NT, preferred_element_type=f32)
        e, den = _masked_exp(s, mask_w)
        o_w = jnp.dot(e.astype(bf16), vw, preferred_element_type=f32) / jnp.maximum(den, 1e-30)

        o_c = oc_ref[g]
        for r in range(GROUP):
            base = (g * GROUP + r) * 3
            sl = slice(r * tn, (r + 1) * tn)
            out = _gate(sg, base) * o_c[sl] + _gate(sg, base + 1) * o_s[sl] + _gate(sg, base + 2) * o_w[sl]
            o_ref[:, (g * GROUP + r) * HEAD_DIM:(g * GROUP + r + 1) * HEAD_DIM] = out


def _nsa_s_fin(larr, z, win_buf, acc, mx, lsum, o_c, sel, db, tn, past, n_sel):
    aq = N_HEADS * HEAD_DIM
    kvw = N_KV_HEADS * HEAD_DIM
    rows = GROUP * tn
    kern = functools.partial(_nsa_s_fin_kernel, tn=tn, past=past, n_sel=n_sel)

    def new_cols(c0):
        return pl.BlockSpec((tn, kvw), lambda b, l: (b, c0 // kvw))

    st_spec = pl.BlockSpec((None, N_KV_HEADS, rows, HEAD_DIM), lambda b, l: (b, 0, 0, 0))
    return _call(
        kern, grid=(db,),
        in_specs=[pl.BlockSpec((tn, aq), lambda b, l: (b, C_AQ // aq)),
                  new_cols(C_KS), new_cols(C_VS), new_cols(C_KW), new_cols(C_VW),
                  pl.BlockSpec((tn, LANE), lambda b, l: (b, C_AG // LANE)),
                  pl.BlockSpec((None,) + win_buf.shape[1:], lambda b, l: (l[0] * db + b, 0, 0, 0)),
                  st_spec, st_spec, st_spec, st_spec,
                  pl.BlockSpec((None, N_KV_HEADS, tn, sel.shape[-1]), lambda b, l: (b, 0, 0, 0))],
        out_specs=pl.BlockSpec((tn, aq), lambda b, l: (b, 0)),
        out_shape=jax.ShapeDtypeStruct((db * tn, aq), f32),
        name="nsa_sample_fin",
    )(larr, z, z, z, z, z, z, win_buf, acc, mx, lsum, o_c, sel)


def _merge_kernel(l_ref, x_ref, or_ref, oa_ref, ga_ref, gb_ref, wpa_ref, wpb_ref, wo_ref, o_ref, acc_ref):
    j = pl.program_id(1)

    @pl.when(j == 0)
    def _():
        acc_ref[...] = jnp.zeros_like(acc_ref)

    pa = jnp.dot(or_ref[...], wpa_ref[...], preferred_element_type=f32)
    pb = jnp.dot(oa_ref[...], wpb_ref[...], preferred_element_type=f32)
    mix = _sigmoid(ga_ref[...]) * pa + _sigmoid(gb_ref[...]) * pb
    acc_ref[...] += jnp.dot(mix.astype(bf16), wo_ref[...], preferred_element_type=f32)

    @pl.when(j == pl.num_programs(1) - 1)
    def _():
        o_ref[...] = x_ref[...] + acc_ref[...]


def _merge(larr, x, o_r, o_a, z, w_pa, w_pb, w_out, tm):
    m = x.shape[0]
    r_v = RET_HEADS * RET_DV
    a_q = N_HEADS * HEAD_DIM
    return _call(
        _merge_kernel, grid=(m // tm, D_MODEL // TN),
        in_specs=[
            pl.BlockSpec((tm, D_MODEL), lambda i, j, l: (i, 0)),
            pl.BlockSpec((tm, r_v), lambda i, j, l: (i, 0)),
            pl.BlockSpec((tm, a_q), lambda i, j, l: (i, 0)),
            pl.BlockSpec((tm, TN), lambda i, j, l: (i, C_GA // TN + j)),
            pl.BlockSpec((tm, TN), lambda i, j, l: (i, C_GB // TN + j)),
            pl.BlockSpec((None, r_v, TN), lambda i, j, l: (l[0], 0, j)),
            pl.BlockSpec((None, a_q, TN), lambda i, j, l: (l[0], 0, j)),
            pl.BlockSpec((None, TN, D_MODEL), lambda i, j, l: (l[0], j, 0)),
        ],
        out_specs=pl.BlockSpec((tm, D_MODEL), lambda i, j, l: (i, 0)),
        out_shape=jax.ShapeDtypeStruct((m, D_MODEL), f32),
        scratch=[pltpu.VMEM((tm, D_MODEL), f32)],
        name="merge",
    )(larr, x, o_r, o_a, z, z, w_pa, w_pb, w_out)


def _rope_tables(pos):
    half = HEAD_DIM // 2
    inv = 1.0 / (ROPE_THETA ** (jnp.arange(half, dtype=f32) / half))
    ang = pos.astype(f32)[:, None] * inv[None, :]
    cos, sin = jnp.cos(ang), jnp.sin(ang)
    return jnp.concatenate([cos, cos], axis=-1), jnp.concatenate([-sin, sin], axis=-1)


def _cmp_to_sel_table(n_c_valid, n_sel, rows, cols):
    cs = np.arange(n_c_valid) * CMP_STRIDE
    ce = cs + CMP_BLOCK - 1
    js = np.arange(n_sel) * SEL_BLOCK
    je = js + SEL_BLOCK - 1
    tab = np.zeros((rows, cols), np.float32)
    tab[:n_c_valid, :n_sel] = (cs[:, None] <= je[None, :]) & (ce[:, None] >= js[None, :])
    return jnp.asarray(tab, dtype=bf16)


def _expand_table(rows, n_keys, first_block=0):
    tab = np.zeros((rows, n_keys), np.float32)
    s = np.arange(n_keys)
    tab[first_block + s // SEL_BLOCK, s] = 1.0
    return tab


def _round_up(a, b):
    return -(-a // b) * b


def kernel(x_prompt, x_sample, cache_kv, state_win, state_ret, page_table, norm_gain, ffn_gate,
           ffn_up, ffn_down, w_in, qk_norm, cmp_w, ret_gn, w_pa, w_pb, w_out):
    nb, t_len, _ = x_prompt.shape
    db, tn, _ = x_sample.shape
    depth, n_pool, page = cache_kv.shape[:3]
    n_pages = page_table.shape[1]
    past = n_pages * page
    wc = state_win.shape[2]
    kvw = N_KV_HEADS * HEAD_DIM
    assert t_len % 512 == 0 and t_len >= WINDOW and wc == WINDOW and tn == 8
    assert n_pages % PAGES_PER_STEP == 0 and n_pages % CMP_PAGES_PER_STEP == 0
    assert past % SEL_BLOCK == 0 and tn <= CMP_STRIDE

    wg, wu, wd = ffn_gate.astype(bf16), ffn_up.astype(bf16), ffn_down.astype(bf16)
    w_head = w_in.astype(bf16)
    w_tail = w_in[..., W_IN_SPLIT:].astype(bf16)
    w_pa_b, w_pb_b, w_out_b = w_pa.astype(bf16), w_pb.astype(bf16), w_out.astype(bf16)

    gains = norm_gain.reshape(depth, 3, 1, D_MODEL)
    ones = jnp.ones((depth, HEAD_DIM), f32)
    tile_rows = []
    for j in range(N_ZT):
        c0 = j * TN
        if c0 == C_KC:
            tile_rows.append(qk_norm[:, 1])
        elif c0 == C_KS:
            tile_rows.append(qk_norm[:, 2])
        elif c0 == C_KW:
            tile_rows.append(qk_norm[:, 3])
        elif C_AQ <= c0 < C_KC:
            tile_rows.append(qk_norm[:, 0])
        elif C_RK <= c0 < C_RV:
            tile_rows.append(ones * (RET_DK ** -0.5))
        else:
            tile_rows.append(ones)
    tile_gain = jnp.stack(tile_rows, axis=1).reshape(depth, N_ZT, 1, HEAD_DIM)
    modes = jnp.asarray(_TILE_MODE, jnp.int32)

    w_cmp = jnp.repeat(cmp_w, HEAD_DIM, axis=-1)
    w_cmp = w_cmp.transpose(0, 2, 1, 3).reshape(depth, CMP_BLOCK, 2 * kvw)
    w_cmp_rows = w_cmp.reshape(depth, CMP_BLOCK, 2 * N_KV_HEADS, HEAD_DIM)
    gn = ret_gn.reshape(depth, RET_HEADS, 1, RET_DV)

    cos_p, sin_p = _rope_tables(jnp.arange(t_len, dtype=jnp.int32))
    cos_s, sin_s = _rope_tables(jnp.tile(past + jnp.arange(tn, dtype=jnp.int32), db))

    n_c_p = t_len // CMP_STRIDE
    n_sel_p = t_len // SEL_BLOCK
    mcs_p = _cmp_to_sel_table((t_len - CMP_BLOCK) // CMP_STRIDE + 1, n_sel_p, n_c_p, n_sel_p).T
    emat_p = jnp.asarray(_expand_table(n_sel_p, t_len), dtype=bf16)
    l_full = past + tn
    n_sel_s = -(-l_full // SEL_BLOCK)
    n_c_s = past // CMP_STRIDE
    sel_w = _round_up(n_sel_s, LANE)
    mcs_s = _cmp_to_sel_table((l_full - CMP_BLOCK) // CMP_STRIDE + 1, n_sel_s, n_c_s, sel_w)
    steps = n_pages // PAGES_PER_STEP
    keys_per_step = PAGES_PER_STEP * page
    blocks_per_step = keys_per_step // SEL_BLOCK
    emat_s = jnp.asarray(_expand_table(LANE, keys_per_step), dtype=bf16)

    cache2 = cache_kv.reshape(depth * n_pool, page, 2, 2 * N_KV_HEADS, HEAD_DIM)
    win2 = state_win.reshape(depth * db, wc, 2 * N_KV_HEADS, HEAD_DIM)
    sret2 = state_ret.reshape(depth * db, RET_HEADS, RET_DK, RET_DV)
    zero_state = jnp.zeros((nb, RET_HEADS, RET_DK, RET_DV), f32)

    mp = nb * t_len
    ms = db * tn
    tm_p = 512
    rb = 512
    keep = min(WINDOW, t_len)
    tm_in = 1024
    tq = 512
    c_p = math.gcd(t_len, RET_CHUNK)

    def layer(carry, l):
        xp, xs, kv_buf, win_buf = carry
        larr = jnp.reshape(l, (1,)).astype(jnp.int32)

        xp = _ffn(larr, xp, gains, 0, wg, wu, wd, 0, tm_p)
        xs = _ffn(larr, xs, gains, 0, wg, wu, wd, 0, ms)

        zp = _inproj(larr, modes, xp, gains, w_head, w_tail, tile_gain, cos_p, sin_p, min(tm_in, t_len))
        o_r, s_fin = _retention(larr, zp, zero_state, 0, gn, nb, t_len // c_p, c_p, bf16)
        ca, cb = _compress_prompt(larr, zp, w_cmp, 512)
        o_a = _nsa_prompt(larr, zp, ca, cb, mcs_p, emat_p, nb, t_len, tq)
        xp = _merge(larr, xp, o_r, o_a, zp, w_pa_b, w_pb_b, w_out_b, tm_p)
        kv_buf = _kv_rows(larr, zp, kv_buf, rb, C_KC, 4 * kvw, mp // rb, lambda i: i)
        win_buf = _kv_rows(larr, zp, win_buf, rb, C_KW, 2 * kvw, nb * (keep // rb),
                           lambda i: (i // (keep // rb)) * (t_len // rb) + (t_len - keep) // rb + i % (keep // rb))

        zs = _inproj(larr, modes, xs, gains, w_head, w_tail, tile_gain, cos_s, sin_s, ms)
        o_rs, s_new = _retention(larr, zs, sret2, db, gn, db, 1, tn, f32)
        sa, sb = _compress_pages(larr, page_table, cache2, w_cmp_rows, n_pool)
        o_c, sel = _nsa_s_cmp(larr, zs, sa, sb, mcs_s, db, tn, past, n_sel_s)
        sel_steps = sel[..., :steps * blocks_per_step].reshape(db, N_KV_HEADS, tn, steps, blocks_per_step)
        sel_steps = jnp.pad(sel_steps.transpose(0, 3, 1, 2, 4), ((0, 0),) * 4 + ((0, LANE - blocks_per_step),))
        acc, mx, lsum = _nsa_s_sel(larr, page_table, zs, sel_steps, emat_s, cache2, n_pool, tn)
        o_as = _nsa_s_fin(larr, zs, win2, acc, mx, lsum, o_c, sel, db, tn, past, n_sel_s)
        xs = _merge(larr, xs, o_rs.astype(bf16), o_as.astype(bf16), zs, w_pa_b, w_pb_b, w_out_b, ms)
        kv_s = zs[:, C_KC:C_KW].reshape(db, tn, 4, N_KV_HEADS, HEAD_DIM)
        win_new = zs[:, C_KW:C_AG].reshape(db, tn, 2, N_KV_HEADS, HEAD_DIM)
        win_old = lax.dynamic_index_in_dim(state_win, l, 0, keepdims=False)
        win_s = jnp.concatenate([win_old[:, tn:], win_new], axis=1)

        xp = _ffn(larr, xp, gains, 2, wg, wu, wd, 1, tm_p)
        xs = _ffn(larr, xs, gains, 2, wg, wu, wd, 1, ms)
        return (xp, xs, kv_buf, win_buf), (kv_s, win_s, s_fin, s_new)

    kv_buf0 = jnp.zeros((depth, mp * 4 * kvw // LANE, LANE), f32)
    win_buf0 = jnp.zeros((depth, nb * keep * 2 * kvw // LANE, LANE), f32)
    (xp, xs, kv_buf, win_buf), outs = lax.scan(
        layer, (x_prompt.reshape(mp, D_MODEL), x_sample.reshape(ms, D_MODEL), kv_buf0, win_buf0),
        jnp.arange(depth, dtype=jnp.int32))
    kv_s, win_s, ret_p, ret_s = outs
    kv_p = kv_buf.reshape(depth, nb, t_len, 4, N_KV_HEADS, HEAD_DIM)
    win_p = win_buf.reshape(depth, nb, keep, 2, N_KV_HEADS, HEAD_DIM)
    return (xp.reshape(nb, t_len, D_MODEL), xs.reshape(db, tn, D_MODEL), kv_p, kv_s, win_p, win_s, ret_p, ret_s)
```

```python
import functools
import math

import jax
import jax.numpy as jnp
import numpy as np
from jax import lax
from jax.experimental import pallas as pl
from jax.experimental.pallas import tpu as pltpu

D_MODEL = 2048
D_FF = 5504
RET_HEADS = 8
RET_DK = 128
RET_DV = 256
RET_CHUNK = 128
N_HEADS = 16
N_KV_HEADS = 4
HEAD_DIM = 128
GROUP = N_HEADS // N_KV_HEADS
CMP_BLOCK = 32
CMP_STRIDE = 16
SEL_BLOCK = 64
N_SEL = 16
WINDOW = 512
ROPE_THETA = 10000.0
EPS = 1e-6
NEG = -1e30
BIG = 1e30
SCALE = HEAD_DIM ** -0.5
LOG2E = math.log2(math.e)

LANE = 128
VMEM_LIMIT = 56 * 1024 * 1024

TN = 512
C_RQ, C_RK, C_RV, C_RG = 0, 1024, 2048, 4096
C_AQ = 6144
C_KC, C_VC, C_KS, C_VS, C_KW, C_VW = 8192, 8704, 9216, 9728, 10240, 10752
C_AG = 11264
C_GA, C_GB = 11776, 13824
DZ = 15872
N_ZT = DZ // TN
N_HEAD_TILES = C_GA // TN
W_IN_SPLIT = 11312
_TILE_MODE = [1, 1, 1, 1] + [0] * 8 + [2, 2, 2, 2] + [2, 0, 2, 0, 2, 0] + [0] * 9

TF = 512
F_TILES = -(-D_FF // TF)
PAGES_PER_STEP = 32
CMP_PAGES_PER_STEP = 32

_RET_LOG_G = [float(np.log(np.float32(1.0) - np.float32(2.0) ** np.float32(-5.0 - h))) for h in range(RET_HEADS)]

_NT = (((1,), (1,)), ((), ()))
_TN = (((0,), (0,)), ((), ()))

bf16 = jnp.bfloat16
f32 = jnp.float32


def _sigmoid(x):
    return 1.0 / (1.0 + jnp.exp(-x))


def _call(kernel, *, grid, in_specs, out_specs, out_shape, scratch=(), nsp=1, sem=None, name=None, aliases=None):
    return pl.pallas_call(
        kernel,
        grid_spec=pltpu.PrefetchScalarGridSpec(num_scalar_prefetch=nsp, grid=grid, in_specs=in_specs,
                                               out_specs=out_specs, scratch_shapes=list(scratch)),
        out_shape=out_shape,
        compiler_params=pltpu.CompilerParams(dimension_semantics=sem or ("arbitrary",) * len(grid),
                                             vmem_limit_bytes=VMEM_LIMIT),
        input_output_aliases=aliases or {},
        name=name,
    )


def _ffn_kernel(l_ref, x_ref, g_ref, wg_ref, wu_ref, wd_ref, o_ref, h_ref, acc_ref):
    f = pl.program_id(1)

    @pl.when(f == 0)
    def _():
        x = x_ref[...]
        ms = jnp.mean(x * x, axis=-1, keepdims=True)
        h_ref[...] = (x * lax.rsqrt(ms + EPS) * g_ref[...]).astype(bf16)
        acc_ref[...] = jnp.zeros_like(acc_ref)

    def accumulate(width):
        h = h_ref[...]
        a = jnp.dot(h, wg_ref[:, 0:width], preferred_element_type=f32)
        b = jnp.dot(h, wu_ref[:, 0:width], preferred_element_type=f32)
        s = (a * _sigmoid(a)) * b
        acc_ref[...] += jnp.dot(s.astype(bf16), wd_ref[0:width, :], preferred_element_type=f32)

    last = pl.num_programs(1) - 1

    @pl.when(f < last)
    def _():
        accumulate(TF)

    @pl.when(f == last)
    def _():
        accumulate(D_FF - (F_TILES - 1) * TF)
        o_ref[...] = x_ref[...] + 0.5 * acc_ref[...]


def _ffn(larr, x, gains, which_gain, wg, wu, wd, which_w, tm):
    m = x.shape[0]
    grid = (m // tm, F_TILES)
    return _call(
        _ffn_kernel, grid=grid,
        in_specs=[
            pl.BlockSpec((tm, D_MODEL), lambda i, f, l: (i, 0)),
            pl.BlockSpec((None, None, 1, D_MODEL), lambda i, f, l: (l[0], which_gain, 0, 0)),
            pl.BlockSpec((None, None, D_MODEL, TF), lambda i, f, l: (l[0], which_w, 0, f)),
            pl.BlockSpec((None, None, D_MODEL, TF), lambda i, f, l: (l[0], which_w, 0, f)),
            pl.BlockSpec((None, None, TF, D_MODEL), lambda i, f, l: (l[0], which_w, f, 0)),
        ],
        out_specs=pl.BlockSpec((tm, D_MODEL), lambda i, f, l: (i, 0)),
        out_shape=jax.ShapeDtypeStruct((m, D_MODEL), f32),
        scratch=[pltpu.VMEM((tm, D_MODEL), bf16), pltpu.VMEM((tm, D_MODEL), f32)],
        name="ffn",
    )(larr, x, gains, wg, wu, wd)


def _inproj_kernel(l_ref, mode_ref, x_ref, g_ref, w_ref, wt_ref, gain_ref, cos_ref, sin_ref, o_ref, h_ref,
                   acc_a, acc_b):
    j = pl.program_id(1)
    n_tiles = pl.num_programs(1) - 1
    mode = mode_ref[jnp.maximum(j - 1, 0)]

    def matmul(acc, w):
        acc[...] = jnp.dot(h_ref[...], w[...], preferred_element_type=f32)

    def finish_plain(acc):
        o_ref[...] = acc[...]

    def finish_rotary(acc):
        cos = cos_ref[...]
        sin = sin_ref[...]
        gain = gain_ref[...]
        for hd in range(TN // HEAD_DIM):
            y = acc[:, hd * HEAD_DIM:(hd + 1) * HEAD_DIM]
            ms = jnp.mean(y * y, axis=-1, keepdims=True)
            inv = jnp.where(mode == 2, lax.rsqrt(ms + EPS), 1.0)
            y = y * inv * gain
            o_ref[:, hd * HEAD_DIM:(hd + 1) * HEAD_DIM] = y * cos + pltpu.roll(y, HEAD_DIM // 2, 1) * sin

    @pl.when(j == 0)
    def _():
        x = x_ref[...]
        ms = jnp.mean(x * x, axis=-1, keepdims=True)
        h_ref[...] = (x * lax.rsqrt(ms + EPS) * g_ref[...]).astype(bf16)
        matmul(acc_a, w_ref)

    for parity, (cur, prev) in enumerate(((acc_a, acc_b), (acc_b, acc_a))):
        mid = (j > 0) & (j < n_tiles) & ((j & 1) == parity)

        @pl.when(mid & (mode == 0) & (j < N_HEAD_TILES))
        def _(cur=cur, prev=prev):
            matmul(cur, w_ref)
            finish_plain(prev)

        @pl.when(mid & (mode == 0) & (j >= N_HEAD_TILES))
        def _(cur=cur, prev=prev):
            matmul(cur, wt_ref)
            finish_plain(prev)

        @pl.when(mid & (mode != 0))
        def _(cur=cur, prev=prev):
            matmul(cur, w_ref)
            finish_rotary(prev)

    @pl.when(j == n_tiles)
    def _():
        finish_plain(acc_a if N_ZT % 2 == 1 else acc_b)


def _inproj(larr, modes, x, gains, w_head, w_tail, tile_gain, cos2, sin2, tm):
    assert _TILE_MODE[-1] == 0 and all(md == 0 for md in _TILE_MODE[N_HEAD_TILES - 1:])
    assert w_head.shape[-1] >= N_HEAD_TILES * TN and w_tail.shape[-1] == (N_ZT - N_HEAD_TILES) * TN
    m = x.shape[0]
    nt = cos2.shape[0] // tm
    grid = (m // tm, N_ZT + 1)
    return _call(
        _inproj_kernel, grid=grid, nsp=2,
        in_specs=[
            pl.BlockSpec((tm, D_MODEL), lambda i, j, l, md: (i, 0)),
            pl.BlockSpec((None, None, 1, D_MODEL), lambda i, j, l, md: (l[0], 1, 0, 0)),
            pl.BlockSpec((None, D_MODEL, TN), lambda i, j, l, md: (l[0], 0, jnp.minimum(j, N_HEAD_TILES - 1))),
            pl.BlockSpec((None, D_MODEL, TN),
                         lambda i, j, l, md: (l[0], 0, jnp.clip(j - N_HEAD_TILES, 0, N_ZT - N_HEAD_TILES - 1))),
            pl.BlockSpec((None, None, 1, HEAD_DIM), lambda i, j, l, md: (l[0], jnp.maximum(j - 1, 0), 0, 0)),
            pl.BlockSpec((tm, HEAD_DIM), lambda i, j, l, md: (i % nt, 0)),
            pl.BlockSpec((tm, HEAD_DIM), lambda i, j, l, md: (i % nt, 0)),
        ],
        out_specs=pl.BlockSpec((tm, TN), lambda i, j, l, md: (i, jnp.maximum(j - 1, 0))),
        out_shape=jax.ShapeDtypeStruct((m, DZ), f32),
        scratch=[pltpu.VMEM((tm, D_MODEL), bf16), pltpu.VMEM((tm, TN), f32), pltpu.VMEM((tm, TN), f32)],
        name="inproj",
    )(larr, modes, x, gains, w_head, w_tail, tile_gain, cos2, sin2)


def _kv_rows_kernel(l_ref, z_ref, buf_ref, o_ref):
    del buf_ref
    rows = z_ref.shape[0]
    n = z_ref.shape[1] // LANE
    for c in range(n):
        o_ref[pl.ds(c, rows, stride=n), :] = z_ref[:, c * LANE:(c + 1) * LANE]


def _kv_rows(larr, z, buf, rows, col0, width, n_blocks, in_block):
    n = width // LANE
    return _call(
        _kv_rows_kernel, grid=(n_blocks,),
        in_specs=[pl.BlockSpec((rows, width), lambda i, l: (in_block(i), col0 // width)),
                  pl.BlockSpec(memory_space=pl.ANY)],
        out_specs=pl.BlockSpec((None, rows * n, LANE), lambda i, l: (l[0], i, 0)),
        out_shape=jax.ShapeDtypeStruct(buf.shape, buf.dtype),
        aliases={2: 0}, name="kv_rows",
    )(larr, z, buf)


def _ret_kernel(l_ref, q_ref, k_ref, v_ref, rg_ref, s0_ref, gn_ref, o_ref, sout_ref, s_scr, *, c_true, c_pad):
    c = pl.program_id(1)

    @pl.when(c == 0)
    def _():
        s_scr[...] = s0_ref[...]

    ri = lax.broadcasted_iota(jnp.int32, (c_pad, c_pad), 0)
    ci = lax.broadcasted_iota(jnp.int32, (c_pad, c_pad), 1)
    diff = ri - ci
    row = lax.broadcasted_iota(jnp.int32, (c_pad, 1), 0)

    def padded(a):
        if c_pad == c_true:
            return a
        return jnp.concatenate([a, jnp.zeros((c_pad - c_true, a.shape[1]), a.dtype)], axis=0)

    for h in range(RET_HEADS):
        lg = _RET_LOG_G[h]
        dmat = jnp.where(diff >= 0, jnp.exp(jnp.maximum(diff, 0).astype(f32) * lg), 0.0)
        q = padded(q_ref[:, h * RET_DK:(h + 1) * RET_DK])
        k = padded(k_ref[:, h * RET_DK:(h + 1) * RET_DK])
        v = padded(v_ref[:, h * RET_DV:(h + 1) * RET_DV])
        qb, kb, vb = q.astype(bf16), k.astype(bf16), v.astype(bf16)
        inner = lax.dot_general(qb, kb, _NT, preferred_element_type=f32) * dmat
        xi = jnp.exp((row + 1).astype(f32) * lg)
        state = s_scr[h]
        o = (jnp.dot(inner.astype(bf16), vb, preferred_element_type=f32)
             + jnp.dot(qb, state.astype(bf16), preferred_element_type=f32) * xi)
        wk = jnp.exp((c_true - 1 - row).astype(f32) * lg)
        kw = (k * wk).astype(bf16)
        s_scr[h] = state * float(np.exp(np.float32(c_true * lg))) + lax.dot_general(
            kw, vb, _TN, preferred_element_type=f32)
        o = o[:c_true]
        mu = jnp.mean(o, axis=-1, keepdims=True)
        cen = o - mu
        var = jnp.mean(cen * cen, axis=-1, keepdims=True)
        y = cen * lax.rsqrt(var + EPS) * gn_ref[h]
        rg = rg_ref[:, h * RET_DV:(h + 1) * RET_DV]
        o_ref[:, h * RET_DV:(h + 1) * RET_DV] = (y * (rg * _sigmoid(rg))).astype(o_ref.dtype)

    @pl.when(c == pl.num_programs(1) - 1)
    def _():
        sout_ref[...] = s_scr[...]


def _retention(larr, z, s0, s0_per_layer, ret_gn, nb, nc, c_true, out_dtype):
    c_pad = max(c_true, RET_CHUNK)
    m = z.shape[0]
    r_qk = RET_HEADS * RET_DK
    r_v = RET_HEADS * RET_DV
    kern = functools.partial(_ret_kernel, c_true=c_true, c_pad=c_pad)
    return _call(
        kern, grid=(nb, nc),
        in_specs=[
            pl.BlockSpec((c_true, r_qk), lambda b, c, l: (b * nc + c, C_RQ // r_qk)),
            pl.BlockSpec((c_true, r_qk), lambda b, c, l: (b * nc + c, C_RK // r_qk)),
            pl.BlockSpec((c_true, r_v), lambda b, c, l: (b * nc + c, C_RV // r_v)),
            pl.BlockSpec((c_true, r_v), lambda b, c, l: (b * nc + c, C_RG // r_v)),
            pl.BlockSpec((None, RET_HEADS, RET_DK, RET_DV), lambda b, c, l: (l[0] * s0_per_layer + b, 0, 0, 0)),
            pl.BlockSpec((None, RET_HEADS, 1, RET_DV), lambda b, c, l: (l[0], 0, 0, 0)),
        ],
        out_specs=[
            pl.BlockSpec((c_true, r_v), lambda b, c, l: (b * nc + c, 0)),
            pl.BlockSpec((None, RET_HEADS, RET_DK, RET_DV), lambda b, c, l: (b, 0, 0, 0)),
        ],
        out_shape=[jax.ShapeDtypeStruct((m, r_v), out_dtype),
                   jax.ShapeDtypeStruct((nb, RET_HEADS, RET_DK, RET_DV), f32)],
        scratch=[pltpu.VMEM((RET_HEADS, RET_DK, RET_DV), f32)],
        name="retention",
    )(larr, z, z, z, z, s0, ret_gn)


def _cmp_kernel(*refs, n_in, n_scalar):
    x_refs = refs[n_scalar:n_scalar + n_in]
    w_ref, a_ref, b_ref = refs[n_scalar + n_in:]
    wa = w_ref[0:CMP_STRIDE]
    wb = w_ref[CMP_STRIDE:CMP_BLOCK]
    for k in range(n_in):
        x = x_refs[k][...]
        r = x.shape[0] // CMP_STRIDE
        x3 = x.reshape((r, CMP_STRIDE) + x.shape[1:])
        a_ref[k * r:(k + 1) * r] = jnp.sum(x3 * wa[None], axis=1)
        b_ref[k * r:(k + 1) * r] = jnp.sum(x3 * wb[None], axis=1)


def _compress_prompt(larr, z, w_cmp, rows):
    m = z.shape[0]
    wcols = 2 * N_KV_HEADS * HEAD_DIM
    kern = functools.partial(_cmp_kernel, n_in=1, n_scalar=1)
    shp = jax.ShapeDtypeStruct((m // CMP_STRIDE, wcols), f32)
    return _call(
        kern, grid=(m // rows,),
        in_specs=[pl.BlockSpec((rows, wcols), lambda i, l: (i, C_KC // wcols)),
                  pl.BlockSpec((None, CMP_BLOCK, wcols), lambda i, l: (l[0], 0, 0))],
        out_specs=[pl.BlockSpec((rows // CMP_STRIDE, wcols), lambda i, l: (i, 0))] * 2,
        out_shape=[shp, shp], name="compress_prompt",
    )(larr, z, w_cmp)


def _compress_pages(larr, page_table, cache, w_cmp, n_pool):
    db, n_pages = page_table.shape
    page, _, sg, d = cache.shape[1:]
    pps = CMP_PAGES_PER_STEP
    steps = n_pages // pps
    sub = page // CMP_STRIDE
    kern = functools.partial(_cmp_kernel, n_in=pps, n_scalar=2)

    def page_spec(k):
        return pl.BlockSpec((None, page, None, sg, d),
                            lambda b, n, l, pt: (l[0] * n_pool + pt[b, n * pps + k], 0, 0, 0, 0))

    shp = jax.ShapeDtypeStruct((db * n_pages * sub, sg, d), f32)
    return _call(
        kern, grid=(db, steps), nsp=2,
        in_specs=[page_spec(k) for k in range(pps)]
        + [pl.BlockSpec((None, CMP_BLOCK, sg, d), lambda b, n, l, pt: (l[0], 0, 0, 0))],
        out_specs=[pl.BlockSpec((pps * sub, sg, d), lambda b, n, l, pt: (b * steps + n, 0, 0))] * 2,
        out_shape=[shp, shp], name="compress_pages",
    )(larr, page_table, *([cache] * pps), w_cmp)


def _combine_cmp(a, b):
    n = a.shape[0]
    row = lax.broadcasted_iota(jnp.int32, (n, 1), 0)
    return a + jnp.where(row == n - 1, 0.0, pltpu.roll(b, n - 1, 0))


def _masked_exp(s, mask, exp=jnp.exp):
    s = jnp.where(mask, s, NEG)
    m = jnp.max(s, axis=-1, keepdims=True)
    e = jnp.where(mask, exp(s - m), 0.0)
    return e, jnp.sum(e, axis=-1, keepdims=True)


def _dot_hilo(p, m01):
    hi = p.astype(bf16)
    lo = (p - hi.astype(f32)).astype(bf16)
    return jnp.dot(hi, m01, preferred_element_type=f32) + jnp.dot(lo, m01, preferred_element_type=f32)


def _topk_mask(imp, k, axis=1):
    pos = lax.broadcasted_iota(jnp.int32, imp.shape, axis).astype(f32)
    sel = jnp.zeros(imp.shape, f32)
    for _ in range(k):
        m = jnp.max(imp, axis=axis, keepdims=True)
        idx = jnp.min(jnp.where(imp == m, pos, 1e9), axis=axis, keepdims=True)
        hit = pos == idx
        sel = jnp.where(hit, 1.0, sel)
        imp = jnp.where(hit, -jnp.inf, imp)
    return sel


def _block_importance_t(p_sum, mcs_t, tpos_row, n_sel):
    hi = p_sum.astype(bf16)
    lo = (p_sum - hi.astype(f32)).astype(bf16)
    imp = (lax.dot_general(mcs_t, hi, _NT, preferred_element_type=f32)
           + lax.dot_general(mcs_t, lo, _NT, preferred_element_type=f32))
    jj = lax.broadcasted_iota(jnp.int32, (imp.shape[0], 1), 0)
    forced = jnp.where(jj == 0, 1, jnp.where(jj == (tpos_row >> 6), 1, 0))
    imp = jnp.where(jj * SEL_BLOCK <= tpos_row, imp, NEG)
    imp = jnp.where(forced == 1, BIG, imp)
    return jnp.where(jj < n_sel, imp, -jnp.inf)


def _block_importance(p_sum, mcs, tpos, n_sel):
    imp = _dot_hilo(p_sum, mcs)
    jj = lax.broadcasted_iota(jnp.int32, (1, imp.shape[1]), 1)
    forced = jnp.where(jj == 0, 1, jnp.where(jj == (tpos >> 6), 1, 0))
    imp = jnp.where(jj * SEL_BLOCK <= tpos, imp, NEG)
    imp = jnp.where(forced == 1, BIG, imp)
    return jnp.where(jj < n_sel, imp, -jnp.inf)


def _tile_row(ref, j):
    n, s, d = ref.shape
    return ref.reshape(n * s, d)[pl.ds(j, n, stride=s), :]


def _gate(sg, idx):
    lane = lax.broadcasted_iota(jnp.int32, (1, sg.shape[1]), 1)
    return jnp.sum(jnp.where(lane == idx, sg, 0.0), axis=-1, keepdims=True)


def _softmax_pv(q, k, v, mask):
    s = jnp.where(mask, lax.dot_general(q, k, _NT, preferred_element_type=f32), NEG)
    e = jnp.exp2(s - jnp.max(s, axis=-1, keepdims=True))
    den = jnp.sum(e, axis=-1, keepdims=True)
    return jnp.dot(e.astype(bf16), v, preferred_element_type=f32) / den


def _nsa_prompt_kernel(l_ref, q_ref, ka_ref, kb_ref, va_ref, vb_ref, ks_ref, vs_ref, kw_ref, vw_ref, ag_ref,
                       mcs_ref, e_ref, o_ref, os_scr, *, tq, t_len, wq, wlen):
    g = pl.program_id(1)
    i = pl.program_id(2)
    t0 = i * tq
    tpos = t0 + lax.broadcasted_iota(jnp.int32, (tq, 1), 0)
    n_c = t_len // CMP_STRIDE
    n_sel = t_len // SEL_BLOCK

    kc = _combine_cmp(ka_ref[...], kb_ref[...]).astype(bf16)
    vc = _combine_cmp(va_ref[...], vb_ref[...]).astype(bf16)
    cend = lax.broadcasted_iota(jnp.int32, (1, n_c), 1) * CMP_STRIDE + (CMP_BLOCK - 1)
    mask_c = cend <= tpos

    qs = [(q_ref[:, r * HEAD_DIM:(r + 1) * HEAD_DIM] * (SCALE * LOG2E)).astype(bf16) for r in range(GROUP)]

    p_sum = jnp.zeros((tq, n_c), f32)
    o_c = []
    for r in range(GROUP):
        s = lax.dot_general(qs[r], kc, _NT, preferred_element_type=f32)
        e, den = _masked_exp(s, mask_c, jnp.exp2)
        p = e / jnp.maximum(den, 1e-30)
        p_sum = p_sum + p
        o_c.append(jnp.dot(p.astype(bf16), vc, preferred_element_type=f32))

    tpos_row = t0 + lax.broadcasted_iota(jnp.int32, (1, tq), 1)
    imp_t = _block_importance_t(p_sum, mcs_ref[...], tpos_row, n_sel)
    selb_t = _topk_mask(imp_t, min(N_SEL, n_sel), axis=0).astype(bf16)

    for br in range(t_len // tq):
        @pl.when(i == br)
        def _(br=br):
            klen = (br + 1) * tq
            selk = lax.dot_general(selb_t, e_ref[:, 0:klen], _TN, preferred_element_type=f32)
            kpos = lax.broadcasted_iota(jnp.int32, (1, klen), 1)
            mask_s = jnp.where(kpos <= tpos, selk, 0.0) > 0.5
            ks = ks_ref[0:klen, :].astype(bf16)
            vs = vs_ref[0:klen, :].astype(bf16)
            for r in range(GROUP):
                os_scr[:, r * HEAD_DIM:(r + 1) * HEAD_DIM] = _softmax_pv(qs[r], ks, vs, mask_s)

    sg = _sigmoid(ag_ref[...])
    gates = [[_gate(sg, (g * GROUP + r) * 3 + k) for k in range(3)] for r in range(GROUP)]
    for h in range(tq // wq):
        rs = slice(h * wq, (h + 1) * wq)
        start = pl.multiple_of(jnp.maximum(t0 + (h + 1) * wq - wlen, 0), wq)
        kw = kw_ref[pl.ds(start, wlen), :].astype(bf16)
        vw = vw_ref[pl.ds(start, wlen), :].astype(bf16)
        d = tpos[rs] - (start + lax.broadcasted_iota(jnp.int32, (1, wlen), 1))
        mask_w = jnp.where(d >= 0, d, WINDOW) < WINDOW
        for r in range(GROUP):
            cs = slice(r * HEAD_DIM, (r + 1) * HEAD_DIM)
            o_w = _softmax_pv(qs[r][rs], kw, vw, mask_w)
            out = gates[r][0][rs] * o_c[r][rs] + gates[r][1][rs] * os_scr[rs, cs] + gates[r][2][rs] * o_w
            o_ref[rs, cs] = out.astype(o_ref.dtype)


def _nsa_prompt(larr, z, cmp_a, cmp_b, mcs, emat, nb, t_len, tq):
    m = z.shape[0]
    nq = t_len // tq
    n_c = t_len // CMP_STRIDE
    wq = min(tq, 256)
    wlen = min(WINDOW + wq, t_len)
    gw = GROUP * HEAD_DIM
    kern = functools.partial(_nsa_prompt_kernel, tq=tq, t_len=t_len, wq=wq, wlen=wlen)

    def head_cols(c0):
        return pl.BlockSpec((t_len, HEAD_DIM), lambda b, g, i, l: (b, c0 // HEAD_DIM + g))

    def cmp_spec(off):
        return pl.BlockSpec((n_c, HEAD_DIM), lambda b, g, i, l: (b, off + g))

    return _call(
        kern, grid=(nb, N_KV_HEADS, nq),
        in_specs=[
            pl.BlockSpec((tq, gw), lambda b, g, i, l: (b * nq + i, C_AQ // gw + g)),
            cmp_spec(0), cmp_spec(0), cmp_spec(N_KV_HEADS), cmp_spec(N_KV_HEADS),
            head_cols(C_KS), head_cols(C_VS), head_cols(C_KW), head_cols(C_VW),
            pl.BlockSpec((tq, LANE), lambda b, g, i, l: (b * nq + i, C_AG // LANE)),
            pl.BlockSpec(mcs.shape, lambda b, g, i, l: (0, 0)),
            pl.BlockSpec(emat.shape, lambda b, g, i, l: (0, 0)),
        ],
        out_specs=pl.BlockSpec((tq, gw), lambda b, g, i, l: (b * nq + i, g)),
        out_shape=jax.ShapeDtypeStruct((m, N_HEADS * HEAD_DIM), bf16),
        scratch=[pltpu.VMEM((tq, gw), f32)],
        name="nsa_prompt",
    )(larr, z, cmp_a, cmp_b, cmp_a, cmp_b, z, z, z, z, z, mcs, emat)


def _rows_rt(ref, g_off, tn):
    return jnp.concatenate(
        [ref[:, (g_off + r) * HEAD_DIM:(g_off + r + 1) * HEAD_DIM] for r in range(GROUP)], axis=0)


def _nsa_s_cmp_kernel(l_ref, q_ref, a_ref, b_ref, mcs_ref, oc_ref, sel_ref, *, tn, past, n_sel):
    n_c = a_ref.shape[0]
    rows = GROUP * tn
    tpos_r = past + (lax.broadcasted_iota(jnp.int32, (rows, 1), 0) % tn)
    tpos = past + (lax.broadcasted_iota(jnp.int32, (2 * tn, 1), 0) % tn)
    cend = lax.broadcasted_iota(jnp.int32, (1, n_c), 1) * CMP_STRIDE + (CMP_BLOCK - 1)
    mask_c = cend <= tpos_r
    mcs = mcs_ref[...]
    for g in range(N_KV_HEADS):
        q = (_rows_rt(q_ref, g * GROUP, tn) * SCALE).astype(bf16)
        kc = _combine_cmp(_tile_row(a_ref, g), _tile_row(b_ref, g)).astype(bf16)
        vc = _combine_cmp(_tile_row(a_ref, N_KV_HEADS + g), _tile_row(b_ref, N_KV_HEADS + g)).astype(bf16)
        s = lax.dot_general(q, kc, _NT, preferred_element_type=f32)
        e, den = _masked_exp(s, mask_c)
        p = e / jnp.maximum(den, 1e-30)
        oc_ref[g] = jnp.dot(p.astype(bf16), vc, preferred_element_type=f32)
        p_sum = p[0:tn]
        for r in range(1, GROUP):
            p_sum = p_sum + p[r * tn:(r + 1) * tn]
        p_sum = jnp.concatenate([p_sum, jnp.zeros_like(p_sum)], axis=0)
        imp = _block_importance(p_sum, mcs, tpos, n_sel)
        sel_ref[g] = _topk_mask(imp, min(N_SEL, n_sel))[0:tn]


def _nsa_s_cmp(larr, z, cmp_a, cmp_b, mcs, db, tn, past, n_sel):
    n_c = cmp_a.shape[0] // db
    aq = N_HEADS * HEAD_DIM
    kern = functools.partial(_nsa_s_cmp_kernel, tn=tn, past=past, n_sel=n_sel)
    cmp_spec = pl.BlockSpec((n_c,) + cmp_a.shape[1:], lambda b, l: (b, 0, 0))
    return _call(
        kern, grid=(db,),
        in_specs=[pl.BlockSpec((tn, aq), lambda b, l: (b, C_AQ // aq)), cmp_spec, cmp_spec,
                  pl.BlockSpec(mcs.shape, lambda b, l: (0, 0))],
        out_specs=[pl.BlockSpec((None, N_KV_HEADS, GROUP * tn, HEAD_DIM), lambda b, l: (b, 0, 0, 0)),
                   pl.BlockSpec((None, N_KV_HEADS, tn, mcs.shape[1]), lambda b, l: (b, 0, 0, 0))],
        out_shape=[jax.ShapeDtypeStruct((db, N_KV_HEADS, GROUP * tn, HEAD_DIM), f32),
                   jax.ShapeDtypeStruct((db, N_KV_HEADS, tn, mcs.shape[1]), f32)],
        name="nsa_sample_cmp",
    )(larr, z, cmp_a, cmp_b, mcs)


def _nsa_s_sel_kernel(*refs, tn, pps):
    q_ref, sel_ref, e_ref = refs[2:5]
    page_refs = refs[5:5 + pps]
    acc_ref, m_ref, l_ref = refs[5 + pps:]
    n = pl.program_id(1)

    @pl.when(n == 0)
    def _():
        acc_ref[...] = jnp.zeros_like(acc_ref)
        m_ref[...] = jnp.full_like(m_ref, NEG)
        l_ref[...] = jnp.zeros_like(l_ref)

    emat = e_ref[...]
    for g in range(N_KV_HEADS):
        q = (_rows_rt(q_ref, g * GROUP, tn) * SCALE).astype(bf16)
        k_all = jnp.concatenate([_tile_row(pr, g).astype(bf16) for pr in page_refs], axis=0)
        v_all = jnp.concatenate([_tile_row(pr, N_KV_HEADS + g).astype(bf16) for pr in page_refs], axis=0)
        s = lax.dot_general(q, k_all, _NT, preferred_element_type=f32)
        selg = sel_ref[g]
        selg = jnp.concatenate([selg, jnp.zeros_like(selg)], axis=0).astype(bf16)
        selk = jnp.dot(selg, emat, preferred_element_type=f32)[0:tn]
        mask = jnp.concatenate([selk] * GROUP, axis=0) > 0.5
        s = jnp.where(mask, s, NEG)
        m_old = m_ref[g]
        m_new = jnp.maximum(m_old, jnp.max(s, axis=-1, keepdims=True))
        alpha = jnp.exp(m_old - m_new)
        e = jnp.where(mask, jnp.exp(s - m_new[:, 0:1]), 0.0)
        l_ref[g] = alpha * l_ref[g] + jnp.sum(e, axis=-1, keepdims=True)
        acc_ref[g] = alpha * acc_ref[g] + jnp.dot(e.astype(bf16), v_all, preferred_element_type=f32)
        m_ref[g] = m_new


def _nsa_s_sel(larr, page_table, z, sel, emat_pages, cache, n_pool, tn):
    db, n_pages = page_table.shape
    page, _, sg, d = cache.shape[1:]
    pps = PAGES_PER_STEP
    steps = n_pages // pps
    rows = GROUP * tn
    kern = functools.partial(_nsa_s_sel_kernel, tn=tn, pps=pps)

    def page_spec(k):
        return pl.BlockSpec((None, page, None, sg, d),
                            lambda b, n, l, pt: (l[0] * n_pool + pt[b, n * pps + k], 0, 1, 0, 0))

    st = jax.ShapeDtypeStruct((db, N_KV_HEADS, rows, HEAD_DIM), f32)
    st_spec = pl.BlockSpec((None, N_KV_HEADS, rows, HEAD_DIM), lambda b, n, l, pt: (b, 0, 0, 0))
    return _call(
        kern, grid=(db, steps), nsp=2,
        in_specs=[pl.BlockSpec((tn, N_HEADS * HEAD_DIM), lambda b, n, l, pt: (b, C_AQ // (N_HEADS * HEAD_DIM))),
                  pl.BlockSpec((None, None, N_KV_HEADS, tn, LANE), lambda b, n, l, pt: (b, n, 0, 0, 0)),
                  pl.BlockSpec(emat_pages.shape, lambda b, n, l, pt: (0, 0))]
        + [page_spec(k) for k in range(pps)],
        out_specs=[st_spec, st_spec, st_spec],
        out_shape=[st, st, st], name="nsa_sample_sel",
    )(larr, page_table, z, sel, emat_pages, *([cache] * pps))


def _nsa_s_fin_kernel(l_ref, q_ref, ksn_ref, vsn_ref, kwn_ref, vwn_ref, ag_ref, buf_ref,
                      acc_ref, m_ref, lsum_ref, oc_ref, sel_ref, o_ref, *, tn, past, n_sel):
    rows = GROUP * tn
    wc = buf_ref.shape[0]
    tpos = past + (lax.broadcasted_iota(jnp.int32, (rows, 1), 0) % tn)
    kidx = lax.broadcasted_iota(jnp.int32, (1, LANE), 1)
    widx = lax.broadcasted_iota(jnp.int32, (1, wc + LANE), 1)
    d = tpos - (past - wc + widx)
    mask_w = jnp.where(widx < wc + tn, jnp.where(d >= 0, d, WINDOW), WINDOW) < WINDOW
    sg = _sigmoid(ag_ref[...])

    def new_rows(ref, g):
        a = ref[:, g * HEAD_DIM:(g + 1) * HEAD_DIM]
        return jnp.concatenate([a, jnp.zeros((LANE - tn, HEAD_DIM), a.dtype)], axis=0)

    for g in range(N_KV_HEADS):
        q = (_rows_rt(q_ref, g * GROUP, tn) * SCALE).astype(bf16)

        ksn = new_rows(ksn_ref, g).astype(bf16)
        vsn = new_rows(vsn_ref, g).astype(bf16)
        s = lax.dot_general(q, ksn, _NT, preferred_element_type=f32)
        last_sel = sel_ref[g][:, n_sel - 1:n_sel]
        last_sel = jnp.concatenate([last_sel] * GROUP, axis=0)
        mask = jnp.where(kidx < tn, jnp.where(past + kidx <= tpos, last_sel, 0.0), 0.0) > 0.5
        s = jnp.where(mask, s, NEG)
        m_old = m_ref[g]
        m_new = jnp.maximum(m_old, jnp.max(s, axis=-1, keepdims=True))
        alpha = jnp.exp(m_old - m_new)
        e = jnp.where(mask, jnp.exp(s - m_new[:, 0:1]), 0.0)
        den = alpha * lsum_ref[g] + jnp.sum(e, axis=-1, keepdims=True)
        o_s = (alpha * acc_ref[g] + jnp.dot(e.astype(bf16), vsn, preferred_element_type=f32)) / jnp.maximum(den, 1e-30)

        kw = jnp.concatenate([_tile_row(buf_ref, g), new_rows(kwn_ref, g)], axis=0).astype(bf16)
        vw = jnp.concatenate([_tile_row(buf_ref, N_KV_HEADS + g), new_rows(vwn_ref, g)], axis=0).astype(bf16)
        s = lax.dot_general(q, kw, _NT, preferred_element_type=f32)
        e, den = _masked_exp(s, mask_w)
        o_w = jnp.dot(e.astype(bf16), vw, preferred_element_type=f32) / jnp.maximum(den, 1e-30)

        o_c = oc_ref[g]
        for r in range(GROUP):
            base = (g * GROUP + r) * 3
            sl = slice(r * tn, (r + 1) * tn)
            out = _gate(sg, base) * o_c[sl] + _gate(sg, base + 1) * o_s[sl] + _gate(sg, base + 2) * o_w[sl]
            o_ref[:, (g * GROUP + r) * HEAD_DIM:(g * GROUP + r + 1) * HEAD_DIM] = out


def _nsa_s_fin(larr, z, win_buf, acc, mx, lsum, o_c, sel, db, tn, past, n_sel):
    aq = N_HEADS * HEAD_DIM
    kvw = N_KV_HEADS * HEAD_DIM
    rows = GROUP * tn
    kern = functools.partial(_nsa_s_fin_kernel, tn=tn, past=past, n_sel=n_sel)

    def new_cols(c0):
        return pl.BlockSpec((tn, kvw), lambda b, l: (b, c0 // kvw))

    st_spec = pl.BlockSpec((None, N_KV_HEADS, rows, HEAD_DIM), lambda b, l: (b, 0, 0, 0))
    return _call(
        kern, grid=(db,),
        in_specs=[pl.BlockSpec((tn, aq), lambda b, l: (b, C_AQ // aq)),
                  new_cols(C_KS), new_cols(C_VS), new_cols(C_KW), new_cols(C_VW),
                  pl.BlockSpec((tn, LANE), lambda b, l: (b, C_AG // LANE)),
                  pl.BlockSpec((None,) + win_buf.shape[1:], lambda b, l: (l[0] * db + b, 0, 0, 0)),
                  st_spec, st_spec, st_spec, st_spec,
                  pl.BlockSpec((None, N_KV_HEADS, tn, sel.shape[-1]), lambda b, l: (b, 0, 0, 0))],
        out_specs=pl.BlockSpec((tn, aq), lambda b, l: (b, 0)),
        out_shape=jax.ShapeDtypeStruct((db * tn, aq), f32),
        name="nsa_sample_fin",
    )(larr, z, z, z, z, z, z, win_buf, acc, mx, lsum, o_c, sel)


def _merge_kernel(l_ref, x_ref, or_ref, oa_ref, ga_ref, gb_ref, wpa_ref, wpb_ref, wo_ref, o_ref, acc_ref):
    j = pl.program_id(1)

    @pl.when(j == 0)
    def _():
        acc_ref[...] = jnp.zeros_like(acc_ref)

    pa = jnp.dot(or_ref[...], wpa_ref[...], preferred_element_type=f32)
    pb = jnp.dot(oa_ref[...], wpb_ref[...], preferred_element_type=f32)
    mix = _sigmoid(ga_ref[...]) * pa + _sigmoid(gb_ref[...]) * pb
    acc_ref[...] += jnp.dot(mix.astype(bf16), wo_ref[...], preferred_element_type=f32)

    @pl.when(j == pl.num_programs(1) - 1)
    def _():
        o_ref[...] = x_ref[...] + acc_ref[...]


def _merge(larr, x, o_r, o_a, z, w_pa, w_pb, w_out, tm):
    m = x.shape[0]
    r_v = RET_HEADS * RET_DV
    a_q = N_HEADS * HEAD_DIM
    return _call(
        _merge_kernel, grid=(m // tm, D_MODEL // TN),
        in_specs=[
            pl.BlockSpec((tm, D_MODEL), lambda i, j, l: (i, 0)),
            pl.BlockSpec((tm, r_v), lambda i, j, l: (i, 0)),
            pl.BlockSpec((tm, a_q), lambda i, j, l: (i, 0)),
            pl.BlockSpec((tm, TN), lambda i, j, l: (i, C_GA // TN + j)),
            pl.BlockSpec((tm, TN), lambda i, j, l: (i, C_GB // TN + j)),
            pl.BlockSpec((None, r_v, TN), lambda i, j, l: (l[0], 0, j)),
            pl.BlockSpec((None, a_q, TN), lambda i, j, l: (l[0], 0, j)),
            pl.BlockSpec((None, TN, D_MODEL), lambda i, j, l: (l[0], j, 0)),
        ],
        out_specs=pl.BlockSpec((tm, D_MODEL), lambda i, j, l: (i, 0)),
        out_shape=jax.ShapeDtypeStruct((m, D_MODEL), f32),
        scratch=[pltpu.VMEM((tm, D_MODEL), f32)],
        name="merge",
    )(larr, x, o_r, o_a, z, z, w_pa, w_pb, w_out)


def _rope_tables(pos):
    half = HEAD_DIM // 2
    inv = 1.0 / (ROPE_THETA ** (jnp.arange(half, dtype=f32) / half))
    ang = pos.astype(f32)[:, None] * inv[None, :]
    cos, sin = jnp.cos(ang), jnp.sin(ang)
    return jnp.concatenate([cos, cos], axis=-1), jnp.concatenate([-sin, sin], axis=-1)


def _cmp_to_sel_table(n_c_valid, n_sel, rows, cols):
    cs = np.arange(n_c_valid) * CMP_STRIDE
    ce = cs + CMP_BLOCK - 1
    js = np.arange(n_sel) * SEL_BLOCK
    je = js + SEL_BLOCK - 1
    tab = np.zeros((rows, cols), np.float32)
    tab[:n_c_valid, :n_sel] = (cs[:, None] <= je[None, :]) & (ce[:, None] >= js[None, :])
    return jnp.asarray(tab, dtype=bf16)


def _expand_table(rows, n_keys, first_block=0):
    tab = np.zeros((rows, n_keys), np.float32)
    s = np.arange(n_keys)
    tab[first_block + s // SEL_BLOCK, s] = 1.0
    return tab


def _round_up(a, b):
    return -(-a // b) * b


def kernel(x_prompt, x_sample, cache_kv, state_win, state_ret, page_table, norm_gain, ffn_gate,
           ffn_up, ffn_down, w_in, qk_norm, cmp_w, ret_gn, w_pa, w_pb, w_out):
    nb, t_len, _ = x_prompt.shape
    db, tn, _ = x_sample.shape
    depth, n_pool, page = cache_kv.shape[:3]
    n_pages = page_table.shape[1]
    past = n_pages * page
    wc = state_win.shape[2]
    kvw = N_KV_HEADS * HEAD_DIM
    assert t_len % 512 == 0 and t_len >= WINDOW and wc == WINDOW and tn == 8
    assert n_pages % PAGES_PER_STEP == 0 and n_pages % CMP_PAGES_PER_STEP == 0
    assert past % SEL_BLOCK == 0 and tn <= CMP_STRIDE

    wg, wu, wd = ffn_gate.astype(bf16), ffn_up.astype(bf16), ffn_down.astype(bf16)
    w_head = w_in.astype(bf16)
    w_tail = w_in[..., W_IN_SPLIT:].astype(bf16)
    w_pa_b, w_pb_b, w_out_b = w_pa.astype(bf16), w_pb.astype(bf16), w_out.astype(bf16)

    gains = norm_gain.reshape(depth, 3, 1, D_MODEL)
    ones = jnp.ones((depth, HEAD_DIM), f32)
    tile_rows = []
    for j in range(N_ZT):
        c0 = j * TN
        if c0 == C_KC:
            tile_rows.append(qk_norm[:, 1])
        elif c0 == C_KS:
            tile_rows.append(qk_norm[:, 2])
        elif c0 == C_KW:
            tile_rows.append(qk_norm[:, 3])
        elif C_AQ <= c0 < C_KC:
            tile_rows.append(qk_norm[:, 0])
        elif C_RK <= c0 < C_RV:
            tile_rows.append(ones * (RET_DK ** -0.5))
        else:
            tile_rows.append(ones)
    tile_gain = jnp.stack(tile_rows, axis=1).reshape(depth, N_ZT, 1, HEAD_DIM)
    modes = jnp.asarray(_TILE_MODE, jnp.int32)

    w_cmp = jnp.repeat(cmp_w, HEAD_DIM, axis=-1)
    w_cmp = w_cmp.transpose(0, 2, 1, 3).reshape(depth, CMP_BLOCK, 2 * kvw)
    w_cmp_rows = w_cmp.reshape(depth, CMP_BLOCK, 2 * N_KV_HEADS, HEAD_DIM)
    gn = ret_gn.reshape(depth, RET_HEADS, 1, RET_DV)

    cos_p, sin_p = _rope_tables(jnp.arange(t_len, dtype=jnp.int32))
    cos_s, sin_s = _rope_tables(jnp.tile(past + jnp.arange(tn, dtype=jnp.int32), db))

    n_c_p = t_len // CMP_STRIDE
    n_sel_p = t_len // SEL_BLOCK
    mcs_p = _cmp_to_sel_table((t_len - CMP_BLOCK) // CMP_STRIDE + 1, n_sel_p, n_c_p, n_sel_p).T
    emat_p = jnp.asarray(_expand_table(n_sel_p, t_len), dtype=bf16)
    l_full = past + tn
    n_sel_s = -(-l_full // SEL_BLOCK)
    n_c_s = past // CMP_STRIDE
    sel_w = _round_up(n_sel_s, LANE)
    mcs_s = _cmp_to_sel_table((l_full - CMP_BLOCK) // CMP_STRIDE + 1, n_sel_s, n_c_s, sel_w)
    steps = n_pages // PAGES_PER_STEP
    keys_per_step = PAGES_PER_STEP * page
    blocks_per_step = keys_per_step // SEL_BLOCK
    emat_s = jnp.asarray(_expand_table(LANE, keys_per_step), dtype=bf16)

    cache2 = cache_kv.reshape(depth * n_pool, page, 2, 2 * N_KV_HEADS, HEAD_DIM)
    win2 = state_win.reshape(depth * db, wc, 2 * N_KV_HEADS, HEAD_DIM)
    sret2 = state_ret.reshape(depth * db, RET_HEADS, RET_DK, RET_DV)
    zero_state = jnp.zeros((nb, RET_HEADS, RET_DK, RET_DV), f32)

    mp = nb * t_len
    ms = db * tn
    tm_p = 512
    rb = 512
    keep = min(WINDOW, t_len)
    tm_in = 1024
    tq = 512
    c_p = math.gcd(t_len, RET_CHUNK)

    def layer(carry, l):
        xp, xs, kv_buf, win_buf = carry
        larr = jnp.reshape(l, (1,)).astype(jnp.int32)

        xp = _ffn(larr, xp, gains, 0, wg, wu, wd, 0, tm_p)
        xs = _ffn(larr, xs, gains, 0, wg, wu, wd, 0, ms)

        zp = _inproj(larr, modes, xp, gains, w_head, w_tail, tile_gain, cos_p, sin_p, min(tm_in, t_len))
        o_r, s_fin = _retention(larr, zp, zero_state, 0, gn, nb, t_len // c_p, c_p, bf16)
        ca, cb = _compress_prompt(larr, zp, w_cmp, 512)
        o_a = _nsa_prompt(larr, zp, ca, cb, mcs_p, emat_p, nb, t_len, tq)
        xp = _merge(larr, xp, o_r, o_a, zp, w_pa_b, w_pb_b, w_out_b, tm_p)
        kv_buf = _kv_rows(larr, zp, kv_buf, rb, C_KC, 4 * kvw, mp // rb, lambda i: i)
        win_buf = _kv_rows(larr, zp, win_buf, rb, C_KW, 2 * kvw, nb * (keep // rb),
                           lambda i: (i // (keep // rb)) * (t_len // rb) + (t_len - keep) // rb + i % (keep // rb))

        zs = _inproj(larr, modes, xs, gains, w_head, w_tail, tile_gain, cos_s, sin_s, ms)
        o_rs, s_new = _retention(larr, zs, sret2, db, gn, db, 1, tn, f32)
        sa, sb = _compress_pages(larr, page_table, cache2, w_cmp_rows, n_pool)
        o_c, sel = _nsa_s_cmp(larr, zs, sa, sb, mcs_s, db, tn, past, n_sel_s)
        sel_steps = sel[..., :steps * blocks_per_step].reshape(db, N_KV_HEADS, tn, steps, blocks_per_step)
        sel_steps = jnp.pad(sel_steps.transpose(0, 3, 1, 2, 4), ((0, 0),) * 4 + ((0, LANE - blocks_per_step),))
        acc, mx, lsum = _nsa_s_sel(larr, page_table, zs, sel_steps, emat_s, cache2, n_pool, tn)
        o_as = _nsa_s_fin(larr, zs, win2, acc, mx, lsum, o_c, sel, db, tn, past, n_sel_s)
        xs = _merge(larr, xs, o_rs.astype(bf16), o_as.astype(bf16), zs, w_pa_b, w_pb_b, w_out_b, ms)
        kv_s = zs[:, C_KC:C_KW].reshape(db, tn, 4, N_KV_HEADS, HEAD_DIM)
        win_new = zs[:, C_KW:C_AG].reshape(db, tn, 2, N_KV_HEADS, HEAD_DIM)
        win_old = lax.dynamic_index_in_dim(state_win, l, 0, keepdims=False)
        win_s = jnp.concatenate([win_old[:, tn:], win_new], axis=1)

        xp = _ffn(larr, xp, gains, 2, wg, wu, wd, 1, tm_p)
        xs = _ffn(larr, xs, gains, 2, wg, wu, wd, 1, ms)
        return (xp, xs, kv_buf, win_buf), (kv_s, win_s, s_fin, s_new)

    kv_buf0 = jnp.zeros((depth, mp * 4 * kvw // LANE, LANE), f32)
    win_buf0 = jnp.zeros((depth, nb * keep * 2 * kvw // LANE, LANE), f32)
    (xp, xs, kv_buf, win_buf), outs = lax.scan(
        layer, (x_prompt.reshape(mp, D_MODEL), x_sample.reshape(ms, D_MODEL), kv_buf0, win_buf0),
        jnp.arange(depth, dtype=jnp.int32))
    kv_s, win_s, ret_p, ret_s = outs
    kv_p = kv_buf.reshape(depth, nb, t_len, 4, N_KV_HEADS, HEAD_DIM)
    win_p = win_buf.reshape(depth, nb, keep, 2, N_KV_HEADS, HEAD_DIM)
    return (xp.reshape(nb, t_len, D_MODEL), xs.reshape(db, tn, D_MODEL), kv_p, kv_s, win_p, win_s, ret_p, ret_s)
```
